```python
import math
import jax, jax.numpy as jnp
from jax import lax
import numpy as np

D_MODEL = 1024
BATCH = 4
SEQ = 8192
DEPTH = 1
DEC_BATCH = 8
DEC_SEQ = 16
PAST_LEN = 1024

CHUNK = 64
HEAD_DIM = 64
EPS = 1e-6
NEG_INF = -1e30

A_HEADS = 8
A_KV_HEADS = 2
A_GROUP = A_HEADS // A_KV_HEADS
A_WINDOW = 128
A_PREV = (A_WINDOW // CHUNK) * CHUNK
T5_BUCKETS = 32
T5_MAX_DIST = 128

B_HEADS = 8
B_PREV_CHUNKS = 8
B_PREV = B_PREV_CHUNKS * CHUNK
B_REL_CLIP = 128

A_QW = A_HEADS * HEAD_DIM
A_KVW = A_KV_HEADS * HEAD_DIM
B_W = B_HEADS * HEAD_DIM
IN_WIDTH = A_QW + 2 * A_KVW + 3 * B_W + 2 * D_MODEL

D_FF = 3072
CONV_WIDTH = 3

kernel_name = "hybrid_chunk_stream_swa_sink_chunkband_convffn"


def rmsnorm(x, g):
    xf = x.astype(jnp.float32)
    y = xf * lax.rsqrt(jnp.mean(xf * xf, axis=-1, keepdims=True) + EPS)
    return (y * g.astype(jnp.float32)).astype(x.dtype)


def rel_dist(n_q, n_k, n_prev):
    return jnp.arange(n_q)[:, None] - jnp.arange(n_k)[None, :] + n_prev


def t5_bucket(n):
    half = T5_BUCKETS // 2
    max_exact = half // 2
    ret = jnp.where(n < 0, half, 0)
    a = jnp.abs(n)
    af = jnp.maximum(a, 1).astype(jnp.float32)
    large = max_exact + (jnp.log(af / max_exact) / math.log(T5_MAX_DIST / max_exact)
                         * (half - max_exact)).astype(jnp.int32)
    large = jnp.minimum(large, half - 1)
    return ret + jnp.where(a < max_exact, a, large)


def bias_a(t5_table, n_q, n_k, n_prev):
    b = t5_table[:, t5_bucket(rel_dist(n_q, n_k, n_prev))]
    return b.astype(jnp.float32).reshape(A_KV_HEADS, A_GROUP, n_q, n_k)


def bias_b(rel_table, n_q, n_k, n_prev):
    idx = jnp.clip(rel_dist(n_q, n_k, n_prev), -B_REL_CLIP, B_REL_CLIP) + B_REL_CLIP
    return rel_table[:, idx].astype(jnp.float32)[:, None]


def band_attention(q, k, v, bias, valid, sink):
    s = jnp.einsum("bqhgd,bkhd->bhgqk", q, k).astype(jnp.float32) * (HEAD_DIM ** -0.5) + bias
    s = jnp.where(valid, s, NEG_INF)
    m = jnp.max(s, axis=-1, keepdims=True)
    if sink is not None:
        sk = sink.astype(jnp.float32)[None, :, :, None, None]
        m = jnp.maximum(m, sk)
        p = jnp.exp(s - m)
        denom = jnp.sum(p, axis=-1, keepdims=True) + jnp.exp(sk - m)
    else:
        p = jnp.exp(s - m)
        denom = jnp.sum(p, axis=-1, keepdims=True)
    w = (p / denom).astype(v.dtype)
    return jnp.einsum("bhgqk,bkhd->bqhgd", w, v)


def prompt_band(q, k, v, n_prev, bias, sink):
    b, s = q.shape[0], q.shape[1]
    pad = jnp.zeros((b, n_prev) + k.shape[2:], k.dtype)
    kp = jnp.concatenate([pad, k], axis=1)
    vp = jnp.concatenate([pad, v], axis=1)
    band = n_prev + CHUNK

    def one_chunk(c):
        start = c * CHUNK
        qc = lax.dynamic_slice_in_dim(q, start, CHUNK, axis=1)
        kc = lax.dynamic_slice_in_dim(kp, start, band, axis=1)
        vc = lax.dynamic_slice_in_dim(vp, start, band, axis=1)
        valid = (start - n_prev + jnp.arange(band) >= 0)[None, :]
        return band_attention(qc, kc, vc, bias, valid, sink)

    o = lax.map(one_chunk, jnp.arange(s // CHUNK))
    return jnp.moveaxis(o, 0, 1).reshape(b, s, -1)


def sample_band(q, k_new, v_new, k_cache, v_cache, bias, sink):
    b, s = q.shape[0], q.shape[1]
    k = jnp.concatenate([k_cache, k_new], axis=1)
    v = jnp.concatenate([v_cache, v_new], axis=1)
    valid = jnp.ones((1, k.shape[1]), dtype=bool)
    return band_attention(q, k, v, bias, valid, sink).reshape(b, s, -1)


def project(h, w_in):
    b, s = h.shape[0], h.shape[1]
    p = h @ w_in
    cuts = [A_QW, A_QW + A_KVW, A_QW + 2 * A_KVW, A_QW + 2 * A_KVW + B_W,
            A_QW + 2 * A_KVW + 2 * B_W, A_QW + 2 * A_KVW + 3 * B_W,
            A_QW + 2 * A_KVW + 3 * B_W + D_MODEL]
    qa, ka, va, qb, kb, vb, ga, gb = jnp.split(p, cuts, axis=-1)
    qa = qa.reshape(b, s, A_KV_HEADS, A_GROUP, HEAD_DIM)
    ka = ka.reshape(b, s, A_KV_HEADS, HEAD_DIM)
    va = va.reshape(b, s, A_KV_HEADS, HEAD_DIM)
    qb = qb.reshape(b, s, B_HEADS, 1, HEAD_DIM)
    kb = kb.reshape(b, s, B_HEADS, HEAD_DIM)
    vb = vb.reshape(b, s, B_HEADS, HEAD_DIM)
    return qa, ka, va, qb, kb, vb, ga, gb


def conv_ffn(h, conv_buf, w_upg, conv_w, conv_b, w_down):
    u, g = jnp.split(h @ w_upg, 2, axis=-1)
    s = u.shape[1]
    upad = jnp.concatenate([conv_buf, u], axis=1)
    c = conv_b
    for tap in range(CONV_WIDTH):
        c = c + conv_w[tap] * upad[:, tap:tap + s]
    y = (jax.nn.gelu(c, approximate=True) * g) @ w_down
    return y, upad[:, -(CONV_WIDTH - 1):]


def layer(x, a_cache, b_cache, conv_buf, w_in, w_oa, w_ob, w_out, sink, t5_table, rel_table,
          g_pre_mix, g_post_mix, g_pre_ffn, g_post_ffn, w_upg, conv_w, conv_b, w_down):
    s = x.shape[1]
    h = rmsnorm(x, g_pre_mix)
    qa, ka, va, qb, kb, vb, ga, gb = project(h, w_in)
    sink_g = sink.reshape(A_KV_HEADS, A_GROUP)
    if a_cache is None:
        oa = prompt_band(qa, ka, va, A_PREV, bias_a(t5_table, CHUNK, A_PREV + CHUNK, A_PREV), sink_g)
        ob = prompt_band(qb, kb, vb, B_PREV, bias_b(rel_table, CHUNK, B_PREV + CHUNK, B_PREV), None)
        new_rows = (ka[:, -A_PREV:], va[:, -A_PREV:], kb[:, -B_PREV:], vb[:, -B_PREV:])
    else:
        la = a_cache[0].shape[1]
        lb = b_cache[0].shape[1]
        oa = sample_band(qa, ka, va, a_cache[0], a_cache[1], bias_a(t5_table, s, la + s, la), sink_g)
        ob = sample_band(qb, kb, vb, b_cache[0], b_cache[1], bias_b(rel_table, s, lb + s, lb), None)
        new_rows = (ka, va, kb, vb)
    merged = jax.nn.sigmoid(ga) * (oa @ w_oa) + jax.nn.sigmoid(gb) * (ob @ w_ob)
    x = x + rmsnorm(merged @ w_out, g_post_mix)
    f, conv_new = conv_ffn(rmsnorm(x, g_pre_ffn), conv_buf, w_upg, conv_w, conv_b, w_down)
    x = x + rmsnorm(f, g_post_ffn)
    return x, new_rows + (conv_new,)


def setup_inputs(seed: int = 0) -> dict:
    key = jax.random.key(seed)
    ks = jax.random.split(key, 24)
    f32 = jnp.float32

    def nrm(k, shape, scale):
        return jax.random.normal(k, shape, f32) * scale

    a_len = min(A_PREV, PAST_LEN)
    b_len = min(B_PREV, PAST_LEN)
    return {
        "x_prompt": nrm(ks[0], (BATCH, SEQ, D_MODEL), 1.0),
        "x_sample": nrm(ks[1], (DEC_BATCH, DEC_SEQ, D_MODEL), 1.0),
        "cache_a_k": nrm(ks[2], (DEPTH, DEC_BATCH, a_len, A_KV_HEADS, HEAD_DIM), 1.0),
        "cache_a_v": nrm(ks[3], (DEPTH, DEC_BATCH, a_len, A_KV_HEADS, HEAD_DIM), 1.0),
        "cache_b_k": nrm(ks[4], (DEPTH, DEC_BATCH, b_len, B_HEADS, HEAD_DIM), 1.0),
        "cache_b_v": nrm(ks[5], (DEPTH, DEC_BATCH, b_len, B_HEADS, HEAD_DIM), 1.0),
        "state_conv": nrm(ks[6], (DEPTH, DEC_BATCH, CONV_WIDTH - 1, D_FF), 1.0),
        "w_in": nrm(ks[7], (DEPTH, D_MODEL, IN_WIDTH), D_MODEL ** -0.5),
        "w_oa": nrm(ks[8], (DEPTH, A_QW, D_MODEL), A_QW ** -0.5),
        "w_ob": nrm(ks[9], (DEPTH, B_W, D_MODEL), B_W ** -0.5),
        "w_out": nrm(ks[10], (DEPTH, D_MODEL, D_MODEL), D_MODEL ** -0.5),
        "sink_a": nrm(ks[11], (DEPTH, A_HEADS), 1.0),
        "t5_table": nrm(ks[12], (A_HEADS, T5_BUCKETS), 0.5),
        "rel_table_b": nrm(ks[13], (DEPTH, B_HEADS, 2 * B_REL_CLIP + 1), 0.5),
        "g_pre_mix": 1.0 + nrm(ks[14], (DEPTH, D_MODEL), 0.05),
        "g_post_mix": 1.0 + nrm(ks[15], (DEPTH, D_MODEL), 0.05),
        "g_pre_ffn": 1.0 + nrm(ks[16], (DEPTH, D_MODEL), 0.05),
        "g_post_ffn": 1.0 + nrm(ks[17], (DEPTH, D_MODEL), 0.05),
        "w_upg": nrm(ks[18], (DEPTH, D_MODEL, 2 * D_FF), D_MODEL ** -0.5),
        "conv_w": nrm(ks[19], (DEPTH, CONV_WIDTH, D_FF), CONV_WIDTH ** -0.5),
        "conv_b": nrm(ks[20], (DEPTH, D_FF), 0.01),
        "w_down": nrm(ks[21], (DEPTH, D_FF, D_MODEL), D_FF ** -0.5),
    }


def reference(x_prompt, x_sample, cache_a_k, cache_a_v, cache_b_k, cache_b_v, state_conv,
              w_in, w_oa, w_ob, w_out, sink_a, t5_table, rel_table_b,
              g_pre_mix, g_post_mix, g_pre_ffn, g_post_ffn, w_upg, conv_w, conv_b, w_down):
    xp, xs = x_prompt, x_sample
    prompt_states = [[] for _ in range(5)]
    sample_states = [[] for _ in range(5)]
    for l in range(DEPTH):
        weights = (w_in[l], w_oa[l], w_ob[l], w_out[l], sink_a[l], t5_table, rel_table_b[l],
                   g_pre_mix[l], g_post_mix[l], g_pre_ffn[l], g_post_ffn[l],
                   w_upg[l], conv_w[l], conv_b[l], w_down[l])
        zero_buf = jnp.zeros((xp.shape[0], CONV_WIDTH - 1, D_FF), xp.dtype)
        xp, st_p = layer(xp, None, None, zero_buf, *weights)
        xs, st_s = layer(xs, (cache_a_k[l], cache_a_v[l]), (cache_b_k[l], cache_b_v[l]),
                         state_conv[l], *weights)
        for i in range(5):
            prompt_states[i].append(st_p[i])
            sample_states[i].append(st_s[i])
    ps = [jnp.stack(a, axis=0) for a in prompt_states]
    ss = [jnp.stack(a, axis=0) for a in sample_states]
    return (xp, xs, ps[0], ps[1], ps[2], ps[3], ps[4], ss[0], ss[1], ss[2], ss[3], ss[4])
```

```python
import functools
import math

import numpy as np
import jax
import jax.numpy as jnp
from jax import lax
from jax.experimental import pallas as pl
from jax.experimental.pallas import tpu as pltpu

F32 = jnp.float32
BF16 = jnp.bfloat16

D_MODEL = 1024
CHUNK = 64
HEAD_DIM = 64
EPS = 1e-6
NEG_INF = -1e30
QK_SCALE = HEAD_DIM ** -0.5

A_HEADS = 8
A_KV_HEADS = 2
A_GROUP = A_HEADS // A_KV_HEADS
A_PREV = 128
T5_BUCKETS = 32
T5_MAX_DIST = 128

B_HEADS = 8
B_PREV = 512
B_REL_CLIP = 128

A_QW = A_HEADS * HEAD_DIM
A_KVW = A_KV_HEADS * HEAD_DIM
B_W = B_HEADS * HEAD_DIM
D_FF = 3072

_C_QA = 0
_C_KA = _C_QA + A_QW
_C_VA = _C_KA + A_KVW
_C_QB = _C_VA + A_KVW
_C_KB = _C_QB + B_W
_C_VB = _C_KB + B_W
_C_GA = _C_VB + B_W
_C_GB = _C_GA + D_MODEL
IN_WIDTH = _C_GB + D_MODEL

LANES = 128
GROUP = 2 * CHUNK
A_WIN = A_PREV + GROUP
B_WIN = B_PREV + GROUP
SEQ_TILE = 512
FF_CHUNK = 512
N_FF = D_FF // FF_CHUNK
VMEM_LIMIT = 56 * 1024 * 1024


def _rms(x, g):
    y = x * lax.rsqrt(jnp.mean(x * x, axis=-1, keepdims=True) + EPS)
    return y * g


def _dot(a, b):
    return jnp.dot(a, b, preferred_element_type=F32)


def _dot_nt(a, b):
    return lax.dot_general(a, b, (((1,), (1,)), ((), ())), preferred_element_type=F32)


def _t5_bucket(n):
    half = T5_BUCKETS // 2
    max_exact = half // 2
    ret = jnp.where(n < 0, half, 0)
    a = jnp.abs(n)
    af = jnp.maximum(a, 1).astype(jnp.float32)
    large = max_exact + (jnp.log(af / max_exact) / math.log(T5_MAX_DIST / max_exact)
                         * (half - max_exact)).astype(jnp.int32)
    large = jnp.minimum(large, half - 1)
    return ret + jnp.where(a < max_exact, a, large)


def _prompt_rel(n_prev, win):
    j = np.arange(win)[:, None]
    c = np.arange(GROUP)[None, :]
    rel = c - (j - n_prev)
    ci = c // CHUNK
    valid = (j >= CHUNK * ci) & (j < CHUNK * ci + n_prev + CHUNK)
    return rel, valid


def _bias_kernel(t5_ref, rel_ref, ia_ref, ib_ref, isa_ref, isb_ref,
                 bta_ref, btb_ref, bsa_ref, bsb_ref, *, b_ranges, sb_range):
    def lookup(idx, tab_ref, h, lo, hi):
        init = jnp.where(idx < 0, NEG_INF, 0.0).astype(F32)

        def body(b, acc):
            return jnp.where(idx == b, tab_ref[h, b], acc)

        return lax.fori_loop(lo, hi, body, init)

    ia = ia_ref[...]
    for h in range(A_HEADS):
        bta_ref[:, LANES * h:LANES * (h + 1)] = lookup(ia, t5_ref, h, 0, T5_BUCKETS)
    for r, (lo, hi) in enumerate(b_ranges):
        ib = ib_ref[LANES * r:LANES * (r + 1), :]
        for h in range(B_HEADS):
            btb_ref[h, LANES * r:LANES * (r + 1), :] = lookup(ib, rel_ref, h, lo, hi)
    isa = isa_ref[...]
    for h in range(A_HEADS):
        bsa_ref[h] = lookup(isa, t5_ref, h, 0, T5_BUCKETS)
    isb = isb_ref[...]
    for h in range(B_HEADS):
        bsb_ref[h] = lookup(isb, rel_ref, h, sb_range[0], sb_range[1])


def _index_range(idx, valid):
    if not valid.any():
        return (0, 0)
    return (int(idx[valid].min()), int(idx[valid].max()) + 1)


def _build_bias(t5_table, rel_table, dec_seq, la, lb, ka_pad, kb_pad):
    rel_a, valid_a = _prompt_rel(A_PREV, A_WIN)
    ia = jnp.where(jnp.asarray(valid_a), _t5_bucket(jnp.asarray(rel_a)), -1).astype(jnp.int32)
    rel_b, valid_b = _prompt_rel(B_PREV, B_WIN)
    ib_np = np.where(valid_b, np.clip(rel_b, -B_REL_CLIP, B_REL_CLIP) + B_REL_CLIP, -1).astype(np.int32)
    b_ranges = tuple(
        _index_range(ib_np[LANES * r:LANES * (r + 1)], valid_b[LANES * r:LANES * (r + 1)])
        for r in range(B_WIN // LANES))

    q = np.arange(dec_seq)[:, None]
    ka = np.arange(ka_pad)[None, :]
    valid_sa = np.broadcast_to(ka < la + dec_seq, (dec_seq, ka_pad))
    isa = jnp.where(jnp.asarray(valid_sa), _t5_bucket(jnp.asarray(q - ka + la)), -1).astype(jnp.int32)
    kb = np.arange(kb_pad)[None, :]
    valid_sb = np.broadcast_to(kb < lb + dec_seq, (dec_seq, kb_pad))
    isb_np = np.where(valid_sb, np.clip(q - kb + lb, -B_REL_CLIP, B_REL_CLIP) + B_REL_CLIP, -1).astype(np.int32)
    sb_range = _index_range(isb_np, valid_sb)

    smem = pl.BlockSpec(memory_space=pltpu.SMEM)
    vmem = pl.BlockSpec(memory_space=pltpu.VMEM)
    return pl.pallas_call(
        functools.partial(_bias_kernel, b_ranges=b_ranges, sb_range=sb_range),
        out_shape=(
            jax.ShapeDtypeStruct((A_WIN, A_HEADS * GROUP), F32),
            jax.ShapeDtypeStruct((B_HEADS, B_WIN, GROUP), F32),
            jax.ShapeDtypeStruct((A_HEADS, dec_seq, ka_pad), F32),
            jax.ShapeDtypeStruct((B_HEADS, dec_seq, kb_pad), F32),
        ),
        in_specs=[smem, smem, vmem, vmem, vmem, vmem],
        out_specs=(vmem, vmem, vmem, vmem),
        name="bias_tables",
    )(t5_table, rel_table, ia, jnp.asarray(ib_np), isa, jnp.asarray(isb_np))


def _attend_a(g, skip, ka_s, vat_s, qt_s, bta_ref, sink_row, oa_s):
    tok = slice(GROUP * g, GROUP * (g + 1))
    r0 = GROUP * g + skip
    nrows = A_WIN - skip
    kwin = ka_s[r0:r0 + nrows, :]
    q = qt_s[0:A_QW, tok]
    zero = jnp.zeros((HEAD_DIM, GROUP), BF16)
    cols = []
    for i in range(A_HEADS):
        qi = q[HEAD_DIM * i:HEAD_DIM * (i + 1), :]
        cols.append(jnp.concatenate([qi, zero] if i < A_GROUP else [zero, qi], axis=0))
    qbd = jnp.concatenate(cols, axis=1)
    st = _dot(kwin, qbd) + bta_ref[skip:A_WIN, :]
    m = jnp.maximum(jnp.max(st, axis=0, keepdims=True), sink_row)
    p = jnp.exp(st - m)
    l = jnp.sum(p, axis=0, keepdims=True) + jnp.exp(sink_row - m)
    vwin = vat_s[:, r0:r0 + nrows]
    ot = _dot(vwin, p.astype(BF16)) * (1.0 / l)
    for j in range(A_HEADS // 2):
        parts = []
        for i in (2 * j, 2 * j + 1):
            rows = slice(0, HEAD_DIM) if i < A_GROUP else slice(HEAD_DIM, 2 * HEAD_DIM)
            parts.append(ot[rows, GROUP * i:GROUP * (i + 1)])
        blk = jnp.concatenate(parts, axis=0)
        oa_s[tok, LANES * j:LANES * (j + 1)] = blk.T.astype(BF16)


def _attend_b(g, skip, kb_s, vbt_s, qt_s, btb_ref, ob_s):
    tok = slice(GROUP * g, GROUP * (g + 1))
    r0 = GROUP * g + skip
    nrows = B_WIN - skip
    zero = jnp.zeros((HEAD_DIM, GROUP), BF16)
    for p2 in range(B_HEADS // 2):
        lanes = slice(LANES * p2, LANES * (p2 + 1))
        kwin = kb_s[r0:r0 + nrows, lanes]
        qrow = A_QW + LANES * p2
        qe = qt_s[qrow:qrow + HEAD_DIM, tok]
        qo = qt_s[qrow + HEAD_DIM:qrow + 2 * HEAD_DIM, tok]
        qbd = jnp.concatenate([jnp.concatenate([qe, zero], axis=0),
                               jnp.concatenate([zero, qo], axis=0)], axis=1)
        bias = jnp.concatenate([btb_ref[2 * p2, skip:B_WIN, :], btb_ref[2 * p2 + 1, skip:B_WIN, :]], axis=1)
        st = _dot(kwin, qbd) + bias
        m = jnp.max(st, axis=0, keepdims=True)
        p = jnp.exp(st - m)
        l = jnp.sum(p, axis=0, keepdims=True)
        vwin = vbt_s[lanes, r0:r0 + nrows]
        ot = _dot(vwin, p.astype(BF16)) * (1.0 / l)
        blk = jnp.concatenate([ot[0:HEAD_DIM, 0:GROUP], ot[HEAD_DIM:2 * HEAD_DIM, GROUP:2 * GROUP]], axis=0)
        ob_s[tok, lanes] = blk.T.astype(BF16)


def _mixer_kernel(x_ref, gpre_ref, gpost_ref, wka_ref, wkb_ref, wga_ref, wgb_ref, wqt_ref, wvat_ref, wvbt_ref,
                  woa_ref, wob_ref, wout_ref, sink_ref, bta_ref, btb_ref,
                  x1_ref, kat_ref, vat_ref, kbt_ref, vbt_ref,
                  h_s, qt_s, ka_s, vat_s, kb_s, vbt_s, oa_s, ob_s, mg_s, *, tile, n_tiles):
    i = pl.program_id(1)
    x = x_ref[...]
    h = _rms(x, gpre_ref[...]).astype(BF16)
    h_s[...] = h

    ka = _dot(h, wka_ref[...])
    kb = _dot(h, wkb_ref[...])
    ka_s[A_PREV:A_PREV + tile, :] = ka.astype(BF16)
    kb_s[B_PREV:B_PREV + tile, :] = kb.astype(BF16)
    qt_s[...] = (_dot_nt(wqt_ref[...], h) * QK_SCALE).astype(BF16)
    vat = _dot_nt(wvat_ref[...], h)
    vbt = _dot_nt(wvbt_ref[...], h)
    vat_s[:, A_PREV:A_PREV + tile] = vat.astype(BF16)
    vbt_s[:, B_PREV:B_PREV + tile] = vbt.astype(BF16)

    @pl.when(i == n_tiles - 1)
    def _():
        kat_ref[...] = ka
        kbt_ref[...] = kb
        vat_ref[...] = vat.T
        vbt_ref[...] = vbt.T

    sink_row = sink_ref[...]
    n_groups = tile // GROUP

    @pl.when(i == 0)
    def _():
        for g in range(n_groups):
            _attend_a(g, max(A_PREV - GROUP * g, 0), ka_s, vat_s, qt_s, bta_ref, sink_row, oa_s)
            _attend_b(g, max(B_PREV - GROUP * g, 0), kb_s, vbt_s, qt_s, btb_ref, ob_s)

    @pl.when(i > 0)
    def _():
        for g in range(n_groups):
            _attend_a(g, 0, ka_s, vat_s, qt_s, bta_ref, sink_row, oa_s)
            _attend_b(g, 0, kb_s, vbt_s, qt_s, btb_ref, ob_s)

    ka_s[0:A_PREV, :] = ka_s[tile:tile + A_PREV, :]
    vat_s[:, 0:A_PREV] = vat_s[:, tile:tile + A_PREV]
    kb_s[0:B_PREV, :] = kb_s[tile:tile + B_PREV, :]
    vbt_s[:, 0:B_PREV] = vbt_s[:, tile:tile + B_PREV]

    oa = oa_s[...]
    ob = ob_s[...]
    hh = h_s[...]
    cw = 256
    for c in range(D_MODEL // cw):
        cs = slice(cw * c, cw * (c + 1))
        ya = _dot(oa, woa_ref[:, cs])
        yb = _dot(ob, wob_ref[:, cs])
        ga = _dot(hh, wga_ref[:, cs])
        gb = _dot(hh, wgb_ref[:, cs])
        mg_s[:, cs] = (jax.nn.sigmoid(ga) * ya + jax.nn.sigmoid(gb) * yb).astype(BF16)
    z = _dot(mg_s[...], wout_ref[...])
    x1_ref[...] = x_ref[...] + _rms(z, gpost_ref[...])


def _const_spec(shape):
    nd = len(shape)
    return pl.BlockSpec(shape, lambda *_: (0,) * nd, pipeline_mode=pl.Buffered(1))


def _prompt_mixer(x, gpre, gpost, w, sink_row, bta, btb):
    bsz, seq, _ = x.shape
    tile = SEQ_TILE
    assert seq % tile == 0 and tile % GROUP == 0 and tile >= B_PREV
    n_tiles = seq // tile
    row_spec = pl.BlockSpec((None, tile, D_MODEL), lambda b, i: (b, i, 0))

    def tail_spec(width):
        return pl.BlockSpec((None, tile, width), lambda b, i: (b, 0, 0))

    consts = [gpre, gpost, w["ka"], w["kb"], w["ga"], w["gb"], w["qt"], w["vat"], w["vbt"],
              w["oa"], w["ob"], w["out"], sink_row, bta, btb]
    return pl.pallas_call(
        functools.partial(_mixer_kernel, tile=tile, n_tiles=n_tiles),
        grid=(bsz, n_tiles),
        in_specs=[row_spec] + [_const_spec(c.shape) for c in consts],
        out_specs=(row_spec, tail_spec(A_KVW), tail_spec(A_KVW), tail_spec(B_W), tail_spec(B_W)),
        out_shape=(
            jax.ShapeDtypeStruct((bsz, seq, D_MODEL), F32),
            jax.ShapeDtypeStruct((bsz, tile, A_KVW), F32),
            jax.ShapeDtypeStruct((bsz, tile, A_KVW), F32),
            jax.ShapeDtypeStruct((bsz, tile, B_W), F32),
            jax.ShapeDtypeStruct((bsz, tile, B_W), F32),
        ),
        scratch_shapes=[
            pltpu.VMEM((tile, D_MODEL), BF16),
            pltpu.VMEM((A_QW + B_W, tile), BF16),
            pltpu.VMEM((A_PREV + tile, A_KVW), BF16),
            pltpu.VMEM((A_KVW, A_PREV + tile), BF16),
            pltpu.VMEM((B_PREV + tile, B_W), BF16),
            pltpu.VMEM((B_W, B_PREV + tile), BF16),
            pltpu.VMEM((tile, A_QW), BF16),
            pltpu.VMEM((tile, B_W), BF16),
            pltpu.VMEM((tile, D_MODEL), BF16),
        ],
        compiler_params=pltpu.CompilerParams(
            dimension_semantics=("arbitrary", "arbitrary"), vmem_limit_bytes=VMEM_LIMIT),
        name="prompt_mixer",
    )(x, *consts)


def _conv_gelu_gate(u, gt, u1, u2, cw, cb):
    c = cb + cw[0:1, :] * u2
    c = c + cw[1:2, :] * u1
    c = c + cw[2:3, :] * u
    return jax.nn.gelu(c, approximate=True) * gt


def _ffn_kernel(x_ref, gpre_ref, gpost_ref, wup_ref, wgate_ref, cw_ref, cb_ref, wdown_ref,
                x2_ref, tail_ref, carry_s, *, tile):
    i = pl.program_id(1)

    @pl.when(i == 0)
    def _():
        carry_s[...] = jnp.zeros_like(carry_s)

    x1 = x_ref[...]
    xn = _rms(x1, gpre_ref[...]).astype(BF16)
    row = lax.broadcasted_iota(jnp.int32, (tile, FF_CHUNK), 0)
    acc = jnp.zeros((tile, D_MODEL), F32)
    for c in range(N_FF):
        u = _dot(xn, wup_ref[c])
        gt = _dot(xn, wgate_ref[c])
        prev = carry_s[c]
        u1 = jnp.where(row == 0, prev[7:8, :], pltpu.roll(u, 1, 0))
        u2 = jnp.where(row == 0, prev[6:7, :], jnp.where(row == 1, prev[7:8, :], pltpu.roll(u, 2, 0)))
        last = u[tile - 8:tile, :]
        carry_s[c] = last
        tail_ref[:, FF_CHUNK * c:FF_CHUNK * (c + 1)] = last
        act = _conv_gelu_gate(u, gt, u1, u2, cw_ref[c], cb_ref[c])
        acc = acc + _dot(act.astype(BF16), wdown_ref[c])
    x2_ref[...] = x1 + _rms(acc, gpost_ref[...])


def _prompt_ffn(x1, gpre, gpost, w):
    bsz, seq, _ = x1.shape
    tile = SEQ_TILE
    assert seq % tile == 0
    row_spec = pl.BlockSpec((None, tile, D_MODEL), lambda b, i: (b, i, 0))
    consts = [gpre, gpost, w["up"], w["gate"], w["cw"], w["cb"], w["down"]]
    return pl.pallas_call(
        functools.partial(_ffn_kernel, tile=tile),
        grid=(bsz, seq // tile),
        in_specs=[row_spec] + [_const_spec(c.shape) for c in consts],
        out_specs=(row_spec, pl.BlockSpec((None, 8, D_FF), lambda b, i: (b, 0, 0))),
        out_shape=(jax.ShapeDtypeStruct((bsz, seq, D_MODEL), F32),
                   jax.ShapeDtypeStruct((bsz, 8, D_FF), F32)),
        scratch_shapes=[pltpu.VMEM((N_FF, 8, FF_CHUNK), F32)],
        compiler_params=pltpu.CompilerParams(
            dimension_semantics=("arbitrary", "arbitrary"), vmem_limit_bytes=VMEM_LIMIT),
        name="prompt_ffn",
    )(x1, *consts)


def _sample_proj_kernel(x_ref, g_ref, w_ref, p_ref):
    h = _rms(x_ref[...], g_ref[...]).astype(BF16)
    p_ref[...] = _dot(h, w_ref[...])


def _sample_proj(xs, gpre, w_in_b):
    rows = xs.shape[0]
    nblk = 2
    width = IN_WIDTH // nblk
    assert width % LANES == 0
    return pl.pallas_call(
        _sample_proj_kernel,
        grid=(nblk,),
        in_specs=[pl.BlockSpec((rows, D_MODEL), lambda j: (0, 0)),
                  pl.BlockSpec((1, D_MODEL), lambda j: (0, 0)),
                  pl.BlockSpec((D_MODEL, width), lambda j: (0, j))],
        out_specs=pl.BlockSpec((rows, width), lambda j: (0, j)),
        out_shape=jax.ShapeDtypeStruct((rows, IN_WIDTH), F32),
        compiler_params=pltpu.CompilerParams(dimension_semantics=("arbitrary",)),
        name="sample_proj",
    )(xs, gpre, w_in_b)


def _sample_attn_kernel(sink_ref, p_ref, cak_ref, cav_ref, cbk_ref, cbv_ref, bsa_ref, bsb_ref,
                        oa_ref, ob_ref, ka_s, va_s, kb_s, vb_s, *, s, la, lb):
    p = p_ref[...]
    ka_pad = ka_s.shape[0]
    kb_pad = kb_s.shape[0]

    def fill(dst, cache_ref, new, n_cache, n_pad):
        dst[0:n_cache, :] = cache_ref[...].astype(BF16)
        dst[n_cache:n_cache + s, :] = new.astype(BF16)
        dst[n_cache + s:n_pad, :] = jnp.zeros((n_pad - n_cache - s, dst.shape[1]), BF16)

    fill(ka_s, cak_ref, p[:, _C_KA:_C_KA + A_KVW], la, ka_pad)
    fill(va_s, cav_ref, p[:, _C_VA:_C_VA + A_KVW], la, ka_pad)
    fill(kb_s, cbk_ref, p[:, _C_KB:_C_KB + B_W], lb, kb_pad)
    fill(vb_s, cbv_ref, p[:, _C_VB:_C_VB + B_W], lb, kb_pad)

    lo = lax.broadcasted_iota(jnp.int32, (s, LANES), 1) < HEAD_DIM
    top = lax.broadcasted_iota(jnp.int32, (2 * s, 1), 0) < s

    def softmax_pv(qs, k, v, bias, sink):
        st = _dot_nt(qs.astype(BF16), k) + bias
        m = jnp.max(st, axis=-1, keepdims=True)
        if sink is not None:
            m = jnp.maximum(m, sink)
        e = jnp.exp(st - m)
        l = jnp.sum(e, axis=-1, keepdims=True)
        if sink is not None:
            l = l + jnp.exp(sink - m)
        return _dot(e.astype(BF16), v) * (1.0 / l)

    ka = ka_s[...]
    va = va_s[...]
    for j in range(A_HEADS // 2):
        q = p[:, _C_QA + LANES * j:_C_QA + LANES * (j + 1)] * QK_SCALE
        qr = pltpu.roll(q, HEAD_DIM, 1)
        if 2 * j < A_GROUP:
            qs = jnp.concatenate([jnp.where(lo, q, 0.0), jnp.where(lo, qr, 0.0)], axis=0)
        else:
            qs = jnp.concatenate([jnp.where(lo, 0.0, qr), jnp.where(lo, 0.0, q)], axis=0)
        bias = jnp.concatenate([bsa_ref[2 * j], bsa_ref[2 * j + 1]], axis=0)
        sink = jnp.where(top, sink_ref[2 * j], sink_ref[2 * j + 1])
        o = softmax_pv(qs, ka, va, bias, sink)
        oe, oo = o[0:s], o[s:2 * s]
        if 2 * j < A_GROUP:
            out = jnp.where(lo, oe, pltpu.roll(oo, HEAD_DIM, 1))
        else:
            out = jnp.where(lo, pltpu.roll(oe, HEAD_DIM, 1), oo)
        oa_ref[:, LANES * j:LANES * (j + 1)] = out

    for j in range(B_HEADS // 2):
        lanes = slice(LANES * j, LANES * (j + 1))
        q = p[:, _C_QB + LANES * j:_C_QB + LANES * (j + 1)] * QK_SCALE
        qs = jnp.concatenate([jnp.where(lo, q, 0.0), jnp.where(lo, 0.0, q)], axis=0)
        bias = jnp.concatenate([bsb_ref[2 * j], bsb_ref[2 * j + 1]], axis=0)
        o = softmax_pv(qs, kb_s[:, lanes], vb_s[:, lanes], bias, None)
        ob_ref[:, lanes] = jnp.where(lo, o[0:s], o[s:2 * s])


def _sample_attn(sink, p, cak, cav, cbk, cbv, bsa, bsb, s):
    nb, la, _ = cak.shape
    lb = cbk.shape[1]
    ka_pad, kb_pad = bsa.shape[-1], bsb.shape[-1]
    assert s % 16 == 0 and la % 16 == 0 and lb % 16 == 0
    row = lambda w: pl.BlockSpec((s, w), lambda b: (b, 0))
    cache = lambda n, w: pl.BlockSpec((None, n, w), lambda b: (b, 0, 0))
    full = lambda a: pl.BlockSpec(a.shape, lambda b: (0,) * a.ndim)
    return pl.pallas_call(
        functools.partial(_sample_attn_kernel, s=s, la=la, lb=lb),
        grid=(nb,),
        in_specs=[pl.BlockSpec(memory_space=pltpu.SMEM), row(IN_WIDTH),
                  cache(la, A_KVW), cache(la, A_KVW), cache(lb, B_W), cache(lb, B_W), full(bsa), full(bsb)],
        out_specs=(row(A_QW), row(B_W)),
        out_shape=(jax.ShapeDtypeStruct((nb * s, A_QW), F32), jax.ShapeDtypeStruct((nb * s, B_W), F32)),
        scratch_shapes=[pltpu.VMEM((ka_pad, A_KVW), BF16), pltpu.VMEM((ka_pad, A_KVW), BF16),
                        pltpu.VMEM((kb_pad, B_W), BF16), pltpu.VMEM((kb_pad, B_W), BF16)],
        compiler_params=pltpu.CompilerParams(dimension_semantics=("arbitrary",)),
        name="sample_attn",
    )(sink, p, cak, cav, cbk, cbv, bsa, bsb)


def _sample_tail_kernel(x_ref, oa_ref, ob_ref, ga_ref, gb_ref, woa_ref, wob_ref, wout_ref,
                        gpost_ref, gpre2_ref, gpost2_ref, init1_ref, init2_ref,
                        wup_ref, wgate_ref, cw_ref, cb_ref, wdown_ref,
                        x2_ref, u_ref, x1_s, xn_s, acc_s, *, s):
    c = pl.program_id(0)

    @pl.when(c == 0)
    def _():
        ya = _dot(oa_ref[...].astype(BF16), woa_ref[...])
        yb = _dot(ob_ref[...].astype(BF16), wob_ref[...])
        merged = jax.nn.sigmoid(ga_ref[...]) * ya + jax.nn.sigmoid(gb_ref[...]) * yb
        z = _dot(merged.astype(BF16), wout_ref[...])
        x1 = x_ref[...] + _rms(z, gpost_ref[...])
        x1_s[...] = x1
        xn_s[...] = _rms(x1, gpre2_ref[...]).astype(BF16)
        acc_s[...] = jnp.zeros_like(acc_s)

    xn = xn_s[...]
    u = _dot(xn, wup_ref[...])
    gt = _dot(xn, wgate_ref[...])
    u_ref[...] = u
    pos = lax.broadcasted_iota(jnp.int32, u.shape, 0) % s
    u1 = jnp.where(pos == 0, init1_ref[...], pltpu.roll(u, 1, 0))
    u2 = jnp.where(pos < 2, init2_ref[...], pltpu.roll(u, 2, 0))
    act = _conv_gelu_gate(u, gt, u1, u2, cw_ref[...], cb_ref[...])
    acc_s[...] += _dot(act.astype(BF16), wdown_ref[...])

    @pl.when(c == pl.num_programs(0) - 1)
    def _():
        x2_ref[...] = x1_s[...] + _rms(acc_s[...], gpost2_ref[...])


def _sample_tail(xs, oa, ob, ga, gb, w, gpost, gpre2, gpost2, init1, init2, s):
    rows = xs.shape[0]
    full = lambda a: pl.BlockSpec(a.shape, lambda c: (0,) * a.ndim)
    chunk_w = lambda a: pl.BlockSpec((None,) + a.shape[1:], lambda c: (c, 0, 0))
    cols = pl.BlockSpec((rows, FF_CHUNK), lambda c: (0, c))
    args = [xs, oa, ob, ga, gb, w["oa"], w["ob"], w["out"], gpost, gpre2, gpost2]
    return pl.pallas_call(
        functools.partial(_sample_tail_kernel, s=s),
        grid=(N_FF,),
        in_specs=[full(a) for a in args] + [cols, cols] + [chunk_w(w[k]) for k in ("up", "gate", "cw", "cb", "down")],
        out_specs=(full(xs), cols),
        out_shape=(jax.ShapeDtypeStruct((rows, D_MODEL), F32), jax.ShapeDtypeStruct((rows, D_FF), F32)),
        scratch_shapes=[pltpu.VMEM((rows, D_MODEL), F32), pltpu.VMEM((rows, D_MODEL), BF16),
                        pltpu.VMEM((rows, D_MODEL), F32)],
        compiler_params=pltpu.CompilerParams(dimension_semantics=("arbitrary",)),
        name="sample_tail",
    )(*args, init1, init2, w["up"], w["gate"], w["cw"], w["cb"], w["down"])


def _prep_weights(w_in, w_oa, w_ob, w_out, w_upg, conv_w, conv_b, w_down):
    wb = w_in.astype(BF16)
    col = lambda a, n: wb[:, a:a + n]
    up = w_upg[:, :D_FF].astype(BF16).reshape(D_MODEL, N_FF, FF_CHUNK).transpose(1, 0, 2)
    gate = w_upg[:, D_FF:].astype(BF16).reshape(D_MODEL, N_FF, FF_CHUNK).transpose(1, 0, 2)
    return {
        "in": wb,
        "ka": col(_C_KA, A_KVW), "kb": col(_C_KB, B_W), "ga": col(_C_GA, D_MODEL), "gb": col(_C_GB, D_MODEL),
        "qt": jnp.concatenate([col(_C_QA, A_QW), col(_C_QB, B_W)], axis=1).T,
        "vat": col(_C_VA, A_KVW).T, "vbt": col(_C_VB, B_W).T,
        "oa": w_oa.astype(BF16), "ob": w_ob.astype(BF16), "out": w_out.astype(BF16),
        "up": up, "gate": gate,
        "cw": conv_w.reshape(3, N_FF, FF_CHUNK).transpose(1, 0, 2),
        "cb": conv_b.reshape(N_FF, 1, FF_CHUNK),
        "down": w_down.astype(BF16).reshape(N_FF, FF_CHUNK, D_MODEL),
    }


def _round_up(n, m):
    return (n + m - 1) // m * m


def kernel(x_prompt, x_sample, cache_a_k, cache_a_v, cache_b_k, cache_b_v, state_conv,
           w_in, w_oa, w_ob, w_out, sink_a, t5_table, rel_table_b,
           g_pre_mix, g_post_mix, g_pre_ffn, g_post_ffn, w_upg, conv_w, conv_b, w_down):
    depth = w_in.shape[0]
    bsz, seq, _ = x_prompt.shape
    nb, s, _ = x_sample.shape
    la, lb = cache_a_k.shape[2], cache_b_k.shape[2]
    ka_pad = _round_up(la + s, LANES)
    kb_pad = _round_up(lb + s, LANES)

    xp = x_prompt
    xs = x_sample.reshape(nb * s, D_MODEL)
    prompt_states = [[] for _ in range(5)]
    sample_states = [[] for _ in range(5)]
    for l in range(depth):
        w = _prep_weights(w_in[l], w_oa[l], w_ob[l], w_out[l], w_upg[l], conv_w[l], conv_b[l], w_down[l])
        row = lambda g: g[l].reshape(1, D_MODEL)
        bta, btb, bsa, bsb = _build_bias(t5_table, rel_table_b[l], s, la, lb, ka_pad, kb_pad)
        sink_row = jnp.repeat(sink_a[l], GROUP).reshape(1, A_HEADS * GROUP)

        x1, ka_t, va_t, kb_t, vb_t = _prompt_mixer(xp, row(g_pre_mix), row(g_post_mix), w, sink_row, bta, btb)
        xp, conv_t = _prompt_ffn(x1, row(g_pre_ffn), row(g_post_ffn), w)
        prompt_states[0].append(ka_t[:, -A_PREV:].reshape(bsz, A_PREV, A_KV_HEADS, HEAD_DIM))
        prompt_states[1].append(va_t[:, -A_PREV:].reshape(bsz, A_PREV, A_KV_HEADS, HEAD_DIM))
        prompt_states[2].append(kb_t[:, -B_PREV:].reshape(bsz, B_PREV, B_HEADS, HEAD_DIM))
        prompt_states[3].append(vb_t[:, -B_PREV:].reshape(bsz, B_PREV, B_HEADS, HEAD_DIM))
        prompt_states[4].append(conv_t[:, -2:])

        p = _sample_proj(xs, row(g_pre_mix), w["in"])
        oa, ob = _sample_attn(sink_a[l], p,
                              cache_a_k[l].reshape(nb, la, A_KVW), cache_a_v[l].reshape(nb, la, A_KVW),
                              cache_b_k[l].reshape(nb, lb, B_W), cache_b_v[l].reshape(nb, lb, B_W), bsa, bsb, s)
        st = state_conv[l]
        zeros = jnp.zeros((nb, s, D_FF), F32)
        init1 = zeros.at[:, 0].set(st[:, 1]).reshape(nb * s, D_FF)
        init2 = zeros.at[:, 0].set(st[:, 0]).at[:, 1].set(st[:, 1]).reshape(nb * s, D_FF)
        xs, u = _sample_tail(xs, oa, ob, p[:, _C_GA:_C_GA + D_MODEL], p[:, _C_GB:_C_GB + D_MODEL], w,
                             row(g_post_mix), row(g_pre_ffn), row(g_post_ffn), init1, init2, s)
        p3 = p.reshape(nb, s, IN_WIDTH)
        sample_states[0].append(p3[:, :, _C_KA:_C_KA + A_KVW].reshape(nb, s, A_KV_HEADS, HEAD_DIM))
        sample_states[1].append(p3[:, :, _C_VA:_C_VA + A_KVW].reshape(nb, s, A_KV_HEADS, HEAD_DIM))
        sample_states[2].append(p3[:, :, _C_KB:_C_KB + B_W].reshape(nb, s, B_HEADS, HEAD_DIM))
        sample_states[3].append(p3[:, :, _C_VB:_C_VB + B_W].reshape(nb, s, B_HEADS, HEAD_DIM))
        sample_states[4].append(u.reshape(nb, s, D_FF)[:, -2:])

    ps = [jnp.stack(a, axis=0) for a in prompt_states]
    ss = [jnp.stack(a, axis=0) for a in sample_states]
    return (xp, xs.reshape(nb, s, D_MODEL), ps[0], ps[1], ps[2], ps[3], ps[4],
            ss[0], ss[1], ss[2], ss[3], ss[4])
```

```python
import functools
import math

import numpy as np
import jax
import jax.numpy as jnp
from jax import lax
from jax.experimental import pallas as pl
from jax.experimental.pallas import tpu as pltpu

F32 = jnp.float32
BF16 = jnp.bfloat16

D_MODEL = 1024
CHUNK = 64
HEAD_DIM = 64
EPS = 1e-6
NEG_INF = -1e30
QK_SCALE = HEAD_DIM ** -0.5

A_HEADS = 8
A_KV_HEADS = 2
A_GROUP = A_HEADS // A_KV_HEADS
A_PREV = 128
T5_BUCKETS = 32
T5_MAX_DIST = 128

B_HEADS = 8
B_PREV = 512
B_REL_CLIP = 128

A_QW = A_HEADS * HEAD_DIM
A_KVW = A_KV_HEADS * HEAD_DIM
B_W = B_HEADS * HEAD_DIM
D_FF = 3072

_C_QA = 0
_C_KA = _C_QA + A_QW
_C_VA = _C_KA + A_KVW
_C_QB = _C_VA + A_KVW
_C_KB = _C_QB + B_W
_C_VB = _C_KB + B_W
_C_GA = _C_VB + B_W
_C_GB = _C_GA + D_MODEL
IN_WIDTH = _C_GB + D_MODEL

LANES = 128
GROUP = 2 * CHUNK
A_WIN = A_PREV + GROUP
B_WIN = B_PREV + GROUP
SEQ_TILE = 512
FF_CHUNK = 512
N_FF = D_FF // FF_CHUNK
VMEM_LIMIT = 56 * 1024 * 1024


def _rms(x, g):
    y = x * lax.rsqrt(jnp.mean(x * x, axis=-1, keepdims=True) + EPS)
    return y * g


def _dot(a, b):
    return jnp.dot(a, b, preferred_element_type=F32)


def _dot_nt(a, b):
    return lax.dot_general(a, b, (((1,), (1,)), ((), ())), preferred_element_type=F32)


def _t5_bucket(n):
    half = T5_BUCKETS // 2
    max_exact = half // 2
    ret = jnp.where(n < 0, half, 0)
    a = jnp.abs(n)
    af = jnp.maximum(a, 1).astype(jnp.float32)
    large = max_exact + (jnp.log(af / max_exact) / math.log(T5_MAX_DIST / max_exact)
                         * (half - max_exact)).astype(jnp.int32)
    large = jnp.minimum(large, half - 1)
    return ret + jnp.where(a < max_exact, a, large)


def _prompt_rel(n_prev, win):
    j = np.arange(win)[:, None]
    c = np.arange(GROUP)[None, :]
    rel = c - (j - n_prev)
    ci = c // CHUNK
    valid = (j >= CHUNK * ci) & (j < CHUNK * ci + n_prev + CHUNK)
    return rel, valid


def _bias_kernel(t5_ref, rel_ref, ia_ref, ib_ref, isa_ref, isb_ref,
                 bta_ref, btb_ref, bsa_ref, bsb_ref, *, b_ranges, sb_range):
    def lookup(idx, tab_ref, h, lo, hi):
        init = jnp.where(idx < 0, NEG_INF, 0.0).astype(F32)

        def body(b, acc):
            return jnp.where(idx == b, tab_ref[h, b], acc)

        return lax.fori_loop(lo, hi, body, init)

    ia = ia_ref[...]
    for h in range(A_HEADS):
        bta_ref[:, LANES * h:LANES * (h + 1)] = lookup(ia, t5_ref, h, 0, T5_BUCKETS)
    for r, (lo, hi) in enumerate(b_ranges):
        ib = ib_ref[LANES * r:LANES * (r + 1), :]
        for h in range(B_HEADS):
            btb_ref[h, LANES * r:LANES * (r + 1), :] = lookup(ib, rel_ref, h, lo, hi)
    isa = isa_ref[...]
    for h in range(A_HEADS):
        bsa_ref[h] = lookup(isa, t5_ref, h, 0, T5_BUCKETS)
    isb = isb_ref[...]
    for h in range(B_HEADS):
        bsb_ref[h] = lookup(isb, rel_ref, h, sb_range[0], sb_range[1])


def _index_range(idx, valid):
    if not valid.any():
        return (0, 0)
    return (int(idx[valid].min()), int(idx[valid].max()) + 1)


def _build_bias(t5_table, rel_table, dec_seq, la, lb, ka_pad, kb_pad):
    rel_a, valid_a = _prompt_rel(A_PREV, A_WIN)
    ia = jnp.where(jnp.asarray(valid_a), _t5_bucket(jnp.asarray(rel_a)), -1).astype(jnp.int32)
    rel_b, valid_b = _prompt_rel(B_PREV, B_WIN)
    ib_np = np.where(valid_b, np.clip(rel_b, -B_REL_CLIP, B_REL_CLIP) + B_REL_CLIP, -1).astype(np.int32)
    b_ranges = tuple(
        _index_range(ib_np[LANES * r:LANES * (r + 1)], valid_b[LANES * r:LANES * (r + 1)])
        for r in range(B_WIN // LANES))

    q = np.arange(dec_seq)[:, None]
    ka = np.arange(ka_pad)[None, :]
    valid_sa = np.broadcast_to(ka < la + dec_seq, (dec_seq, ka_pad))
    isa = jnp.where(jnp.asarray(valid_sa), _t5_bucket(jnp.asarray(q - ka + la)), -1).astype(jnp.int32)
    kb = np.arange(kb_pad)[None, :]
    valid_sb = np.broadcast_to(kb < lb + dec_seq, (dec_seq, kb_pad))
    isb_np = np.where(valid_sb, np.clip(q - kb + lb, -B_REL_CLIP, B_REL_CLIP) + B_REL_CLIP, -1).astype(np.int32)
    sb_range = _index_range(isb_np, valid_sb)

    smem = pl.BlockSpec(memory_space=pltpu.SMEM)
    vmem = pl.BlockSpec(memory_space=pltpu.VMEM)
    return pl.pallas_call(
        functools.partial(_bias_kernel, b_ranges=b_ranges, sb_range=sb_range),
        out_shape=(
            jax.ShapeDtypeStruct((A_WIN, A_HEADS * GROUP), F32),
            jax.ShapeDtypeStruct((B_HEADS, B_WIN, GROUP), F32),
            jax.ShapeDtypeStruct((A_HEADS, dec_seq, ka_pad), F32),
            jax.ShapeDtypeStruct((B_HEADS, dec_seq, kb_pad), F32),
        ),
        in_specs=[smem, smem, vmem, vmem, vmem, vmem],
        out_specs=(vmem, vmem, vmem, vmem),
        name="bias_tables",
    )(t5_table, rel_table, ia, jnp.asarray(ib_np), isa, jnp.asarray(isb_np))


def _ordering_zero(x, never):
    return jnp.where(never, x, 0.0)


def _add_to_corner(x, z):
    head = jnp.concatenate([x[0:16, 0:LANES] + z, x[0:16, LANES:]], axis=1)
    return jnp.concatenate([head, x[16:, :]], axis=0)


def _scores_a(g, skip, ka_s, qt_s):
    tok = slice(GROUP * g, GROUP * (g + 1))
    r0 = GROUP * g + skip
    kwin = ka_s[r0:r0 + A_WIN - skip, :]
    q = qt_s[0:A_QW, tok]
    zero = jnp.zeros((HEAD_DIM, GROUP), BF16)
    cols = []
    for i in range(A_HEADS):
        qi = q[HEAD_DIM * i:HEAD_DIM * (i + 1), :]
        cols.append(jnp.concatenate([qi, zero] if i < A_GROUP else [zero, qi], axis=0))
    return _dot(kwin, jnp.concatenate(cols, axis=1))


def _scores_b(g, p2, skip, kb_s, qt_s):
    tok = slice(GROUP * g, GROUP * (g + 1))
    r0 = GROUP * g + skip
    kwin = kb_s[r0:r0 + B_WIN - skip, LANES * p2:LANES * (p2 + 1)]
    qrow = A_QW + LANES * p2
    qe = qt_s[qrow:qrow + HEAD_DIM, tok]
    qo = qt_s[qrow + HEAD_DIM:qrow + 2 * HEAD_DIM, tok]
    zero = jnp.zeros((HEAD_DIM, GROUP), BF16)
    qbd = jnp.concatenate([jnp.concatenate([qe, zero], axis=0),
                           jnp.concatenate([zero, qo], axis=0)], axis=1)
    return _dot(kwin, qbd)


def _softmax_keys(st, bias, sink_row, order_after):
    st = st + bias
    m = jnp.max(st, axis=0, keepdims=True)
    if sink_row is not None:
        m = jnp.maximum(m, sink_row)
    p = jnp.exp(st - m)
    l = jnp.sum(p, axis=0, keepdims=True)
    if sink_row is not None:
        l = l + jnp.exp(sink_row - m)
    if order_after is not None:
        p = _add_to_corner(p, order_after)
    return p.astype(BF16), 1.0 / l


def _values_a(g, skip, vat_s, p, rl, oa_s):
    tok = slice(GROUP * g, GROUP * (g + 1))
    r0 = GROUP * g + skip
    vwin = vat_s[:, r0:r0 + A_WIN - skip]
    ot = _dot(vwin, p) * rl
    for j in range(A_HEADS // 2):
        parts = []
        for i in (2 * j, 2 * j + 1):
            rows = slice(0, HEAD_DIM) if i < A_GROUP else slice(HEAD_DIM, 2 * HEAD_DIM)
            parts.append(ot[rows, GROUP * i:GROUP * (i + 1)])
        blk = jnp.concatenate(parts, axis=0)
        oa_s[tok, LANES * j:LANES * (j + 1)] = blk.T.astype(BF16)


def _values_b(g, p2, skip, vbt_s, p, rl, ob_s):
    tok = slice(GROUP * g, GROUP * (g + 1))
    r0 = GROUP * g + skip
    lanes = slice(LANES * p2, LANES * (p2 + 1))
    vwin = vbt_s[lanes, r0:r0 + B_WIN - skip]
    ot = _dot(vwin, p) * rl
    blk = jnp.concatenate([ot[0:HEAD_DIM, 0:GROUP], ot[HEAD_DIM:2 * HEAD_DIM, GROUP:2 * GROUP]], axis=0)
    ob_s[tok, lanes] = blk.T.astype(BF16)


def _attend_tile(n_groups, first, never, ka_s, vat_s, kb_s, vbt_s, qt_s, bta_ref, btb_ref, sink_row, oa_s, ob_s):
    steps = []
    for g in range(n_groups):
        skip_a = max(A_PREV - GROUP * g, 0) if first else 0
        skip_b = max(B_PREV - GROUP * g, 0) if first else 0
        steps.append(("a", g, 0, skip_a))
        steps.extend(("b", g, p2, skip_b) for p2 in range(B_HEADS // 2))

    def scores(step):
        kind, g, p2, skip = step
        return _scores_a(g, skip, ka_s, qt_s) if kind == "a" else _scores_b(g, p2, skip, kb_s, qt_s)

    st_next = scores(steps[0])
    for k, (kind, g, p2, skip) in enumerate(steps):
        st, order_after = st_next, None
        if k + 1 < len(steps):
            st_next = scores(steps[k + 1])
            order_after = _ordering_zero(st_next[0:16, 0:LANES], never)
        if kind == "a":
            p, rl = _softmax_keys(st, bta_ref[skip:A_WIN, :], sink_row, order_after)
            _values_a(g, skip, vat_s, p, rl, oa_s)
        else:
            bias = jnp.concatenate([btb_ref[2 * p2, skip:B_WIN, :], btb_ref[2 * p2 + 1, skip:B_WIN, :]], axis=1)
            p, rl = _softmax_keys(st, bias, None, order_after)
            _values_b(g, p2, skip, vbt_s, p, rl, ob_s)


def _mixer_kernel(x_ref, gpre_ref, gpost_ref, wka_ref, wkb_ref, wga_ref, wgb_ref, wqt_ref, wvat_ref, wvbt_ref,
                  woa_ref, wob_ref, wout_ref, sink_ref, bta_ref, btb_ref, never_ref,
                  x1_ref, kat_ref, vat_ref, kbt_ref, vbt_ref,
                  h_s, qt_s, ka_s, vat_s, kb_s, vbt_s, oa_s, ob_s, mg_s, *, tile, n_tiles):
    i = pl.program_id(1)
    x = x_ref[...]
    h = _rms(x, gpre_ref[...]).astype(BF16)
    h_s[...] = h

    ka = _dot(h, wka_ref[...])
    kb = _dot(h, wkb_ref[...])
    ka_s[A_PREV:A_PREV + tile, :] = ka.astype(BF16)
    kb_s[B_PREV:B_PREV + tile, :] = kb.astype(BF16)
    qt_s[...] = (_dot_nt(wqt_ref[...], h) * QK_SCALE).astype(BF16)
    vat = _dot_nt(wvat_ref[...], h)
    vbt = _dot_nt(wvbt_ref[...], h)
    vat_s[:, A_PREV:A_PREV + tile] = vat.astype(BF16)
    vbt_s[:, B_PREV:B_PREV + tile] = vbt.astype(BF16)

    @pl.when(i == n_tiles - 1)
    def _():
        kat_ref[...] = ka
        kbt_ref[...] = kb
        vat_ref[...] = vat.T
        vbt_ref[...] = vbt.T

    sink_row = sink_ref[...]
    n_groups = tile // GROUP

    never = never_ref[...] != 0
    attend = functools.partial(_attend_tile, n_groups, never=never, ka_s=ka_s, vat_s=vat_s, kb_s=kb_s, vbt_s=vbt_s,
                               qt_s=qt_s, bta_ref=bta_ref, btb_ref=btb_ref, sink_row=sink_row, oa_s=oa_s, ob_s=ob_s)
    pl.when(i == 0)(functools.partial(attend, True))
    pl.when(i > 0)(functools.partial(attend, False))

    ka_s[0:A_PREV, :] = ka_s[tile:tile + A_PREV, :]
    vat_s[:, 0:A_PREV] = vat_s[:, tile:tile + A_PREV]
    kb_s[0:B_PREV, :] = kb_s[tile:tile + B_PREV, :]
    vbt_s[:, 0:B_PREV] = vbt_s[:, tile:tile + B_PREV]

    oa = oa_s[...]
    ob = ob_s[...]
    hh = h_s[...]
    cw = 256
    for c in range(D_MODEL // cw):
        cs = slice(cw * c, cw * (c + 1))
        ya = _dot(oa, woa_ref[:, cs])
        yb = _dot(ob, wob_ref[:, cs])
        ga = _dot(hh, wga_ref[:, cs])
        gb = _dot(hh, wgb_ref[:, cs])
        mg_s[:, cs] = (jax.nn.sigmoid(ga) * ya + jax.nn.sigmoid(gb) * yb).astype(BF16)
    z = _dot(mg_s[...], wout_ref[...])
    x1_ref[...] = x_ref[...] + _rms(z, gpost_ref[...])


def _const_spec(shape):
    nd = len(shape)
    return pl.BlockSpec(shape, lambda *_: (0,) * nd, pipeline_mode=pl.Buffered(1))


def _prompt_mixer(x, gpre, gpost, w, sink_row, bta, btb):
    bsz, seq, _ = x.shape
    tile = SEQ_TILE
    assert seq % tile == 0 and tile % GROUP == 0 and tile >= B_PREV
    n_tiles = seq // tile
    row_spec = pl.BlockSpec((None, tile, D_MODEL), lambda b, i: (b, i, 0))

    def tail_spec(width):
        return pl.BlockSpec((None, tile, width), lambda b, i: (b, 0, 0))

    consts = [gpre, gpost, w["ka"], w["kb"], w["ga"], w["gb"], w["qt"], w["vat"], w["vbt"],
              w["oa"], w["ob"], w["out"], sink_row, bta, btb, jnp.zeros((16, LANES), jnp.int32)]
    return pl.pallas_call(
        functools.partial(_mixer_kernel, tile=tile, n_tiles=n_tiles),
        grid=(bsz, n_tiles),
        in_specs=[row_spec] + [_const_spec(c.shape) for c in consts],
        out_specs=(row_spec, tail_spec(A_KVW), tail_spec(A_KVW), tail_spec(B_W), tail_spec(B_W)),
        out_shape=(
            jax.ShapeDtypeStruct((bsz, seq, D_MODEL), F32),
            jax.ShapeDtypeStruct((bsz, tile, A_KVW), F32),
            jax.ShapeDtypeStruct((bsz, tile, A_KVW), F32),
            jax.ShapeDtypeStruct((bsz, tile, B_W), F32),
            jax.ShapeDtypeStruct((bsz, tile, B_W), F32),
        ),
        scratch_shapes=[
            pltpu.VMEM((tile, D_MODEL), BF16),
            pltpu.VMEM((A_QW + B_W, tile), BF16),
            pltpu.VMEM((A_PREV + tile, A_KVW), BF16),
            pltpu.VMEM((A_KVW, A_PREV + tile), BF16),
            pltpu.VMEM((B_PREV + tile, B_W), BF16),
            pltpu.VMEM((B_W, B_PREV + tile), BF16),
            pltpu.VMEM((tile, A_QW), BF16),
            pltpu.VMEM((tile, B_W), BF16),
            pltpu.VMEM((tile, D_MODEL), BF16),
        ],
        compiler_params=pltpu.CompilerParams(
            dimension_semantics=("arbitrary", "arbitrary"), vmem_limit_bytes=VMEM_LIMIT),
        name="prompt_mixer",
    )(x, *consts)


def _conv_gelu_gate(u, gt, u1, u2, cw, cb):
    c = cb + cw[0:1, :] * u2
    c = c + cw[1:2, :] * u1
    c = c + cw[2:3, :] * u
    return jax.nn.gelu(c, approximate=True) * gt


def _ffn_kernel(x_ref, gpre_ref, gpost_ref, wup_ref, wgate_ref, cw_ref, cb_ref, wdown_ref, never_ref,
                x2_ref, tail_ref, carry_s, *, tile):
    i = pl.program_id(1)

    @pl.when(i == 0)
    def _():
        carry_s[...] = jnp.zeros_like(carry_s)

    x1 = x_ref[...]
    xn = _rms(x1, gpre_ref[...]).astype(BF16)
    never = never_ref[...] != 0
    row = lax.broadcasted_iota(jnp.int32, (tile, FF_CHUNK), 0)
    acc = jnp.zeros((tile, D_MODEL), F32)
    up_next = (_dot(xn, wup_ref[0]), _dot(xn, wgate_ref[0]))
    for c in range(N_FF):
        u, gt = up_next
        prev = carry_s[c]
        u1 = jnp.where(row == 0, prev[7:8, :], pltpu.roll(u, 1, 0))
        u2 = jnp.where(row == 0, prev[6:7, :], jnp.where(row == 1, prev[7:8, :], pltpu.roll(u, 2, 0)))
        last = u[tile - 8:tile, :]
        carry_s[c] = last
        tail_ref[:, FF_CHUNK * c:FF_CHUNK * (c + 1)] = last
        act = _conv_gelu_gate(u, gt, u1, u2, cw_ref[c], cb_ref[c])
        if c + 1 < N_FF:
            up_next = (_dot(xn, wup_ref[c + 1]), _dot(xn, wgate_ref[c + 1]))
            act = _add_to_corner(act, _ordering_zero(up_next[0][0:16, 0:LANES], never))
        acc = acc + _dot(act.astype(BF16), wdown_ref[c])
    x2_ref[...] = x1 + _rms(acc, gpost_ref[...])


def _prompt_ffn(x1, gpre, gpost, w):
    bsz, seq, _ = x1.shape
    tile = SEQ_TILE
    assert seq % tile == 0
    row_spec = pl.BlockSpec((None, tile, D_MODEL), lambda b, i: (b, i, 0))
    consts = [gpre, gpost, w["up"], w["gate"], w["cw"], w["cb"], w["down"], jnp.zeros((16, LANES), jnp.int32)]
    return pl.pallas_call(
        functools.partial(_ffn_kernel, tile=tile),
        grid=(bsz, seq // tile),
        in_specs=[row_spec] + [_const_spec(c.shape) for c in consts],
        out_specs=(row_spec, pl.BlockSpec((None, 8, D_FF), lambda b, i: (b, 0, 0))),
        out_shape=(jax.ShapeDtypeStruct((bsz, seq, D_MODEL), F32),
                   jax.ShapeDtypeStruct((bsz, 8, D_FF), F32)),
        scratch_shapes=[pltpu.VMEM((N_FF, 8, FF_CHUNK), F32)],
        compiler_params=pltpu.CompilerParams(
            dimension_semantics=("arbitrary", "arbitrary"), vmem_limit_bytes=VMEM_LIMIT),
        name="prompt_ffn",
    )(x1, *consts)


def _sample_proj_kernel(x_ref, g_ref, w_ref, p_ref):
    h = _rms(x_ref[...], g_ref[...]).astype(BF16)
    p_ref[...] = _dot(h, w_ref[...])


def _sample_proj(xs, gpre, w_in_b):
    rows = xs.shape[0]
    nblk = 2
    width = IN_WIDTH // nblk
    assert width % LANES == 0
    return pl.pallas_call(
        _sample_proj_kernel,
        grid=(nblk,),
        in_specs=[pl.BlockSpec((rows, D_MODEL), lambda j: (0, 0)),
                  pl.BlockSpec((1, D_MODEL), lambda j: (0, 0)),
                  pl.BlockSpec((D_MODEL, width), lambda j: (0, j))],
        out_specs=pl.BlockSpec((rows, width), lambda j: (0, j)),
        out_shape=jax.ShapeDtypeStruct((rows, IN_WIDTH), F32),
        compiler_params=pltpu.CompilerParams(dimension_semantics=("arbitrary",)),
        name="sample_proj",
    )(xs, gpre, w_in_b)


def _sample_attn_kernel(sink_ref, p_ref, cak_ref, cav_ref, cbk_ref, cbv_ref, bsa_ref, bsb_ref,
                        oa_ref, ob_ref, ka_s, va_s, kb_s, vb_s, *, s, la, lb):
    p = p_ref[...]
    ka_pad = ka_s.shape[0]
    kb_pad = kb_s.shape[0]

    def fill(dst, cache_ref, new, n_cache, n_pad):
        dst[0:n_cache, :] = cache_ref[...].astype(BF16)
        dst[n_cache:n_cache + s, :] = new.astype(BF16)
        dst[n_cache + s:n_pad, :] = jnp.zeros((n_pad - n_cache - s, dst.shape[1]), BF16)

    fill(ka_s, cak_ref, p[:, _C_KA:_C_KA + A_KVW], la, ka_pad)
    fill(va_s, cav_ref, p[:, _C_VA:_C_VA + A_KVW], la, ka_pad)
    fill(kb_s, cbk_ref, p[:, _C_KB:_C_KB + B_W], lb, kb_pad)
    fill(vb_s, cbv_ref, p[:, _C_VB:_C_VB + B_W], lb, kb_pad)

    lo = lax.broadcasted_iota(jnp.int32, (s, LANES), 1) < HEAD_DIM
    top = lax.broadcasted_iota(jnp.int32, (2 * s, 1), 0) < s

    def softmax_pv(qs, k, v, bias, sink):
        st = _dot_nt(qs.astype(BF16), k) + bias
        m = jnp.max(st, axis=-1, keepdims=True)
        if sink is not None:
            m = jnp.maximum(m, sink)
        e = jnp.exp(st - m)
        l = jnp.sum(e, axis=-1, keepdims=True)
        if sink is not None:
            l = l + jnp.exp(sink - m)
        return _dot(e.astype(BF16), v) * (1.0 / l)

    ka = ka_s[...]
    va = va_s[...]
    for j in range(A_HEADS // 2):
        q = p[:, _C_QA + LANES * j:_C_QA + LANES * (j + 1)] * QK_SCALE
        qr = pltpu.roll(q, HEAD_DIM, 1)
        if 2 * j < A_GROUP:
            qs = jnp.concatenate([jnp.where(lo, q, 0.0), jnp.where(lo, qr, 0.0)], axis=0)
        else:
            qs = jnp.concatenate([jnp.where(lo, 0.0, qr), jnp.where(lo, 0.0, q)], axis=0)
        bias = jnp.concatenate([bsa_ref[2 * j], bsa_ref[2 * j + 1]], axis=0)
        sink = jnp.where(top, sink_ref[2 * j], sink_ref[2 * j + 1])
        o = softmax_pv(qs, ka, va, bias, sink)
        oe, oo = o[0:s], o[s:2 * s]
        if 2 * j < A_GROUP:
            out = jnp.where(lo, oe, pltpu.roll(oo, HEAD_DIM, 1))
        else:
            out = jnp.where(lo, pltpu.roll(oe, HEAD_DIM, 1), oo)
        oa_ref[:, LANES * j:LANES * (j + 1)] = out

    for j in range(B_HEADS // 2):
        lanes = slice(LANES * j, LANES * (j + 1))
        q = p[:, _C_QB + LANES * j:_C_QB + LANES * (j + 1)] * QK_SCALE
        qs = jnp.concatenate([jnp.where(lo, q, 0.0), jnp.where(lo, 0.0, q)], axis=0)
        bias = jnp.concatenate([bsb_ref[2 * j], bsb_ref[2 * j + 1]], axis=0)
        o = softmax_pv(qs, kb_s[:, lanes], vb_s[:, lanes], bias, None)
        ob_ref[:, lanes] = jnp.where(lo, o[0:s], o[s:2 * s])


def _sample_attn(sink, p, cak, cav, cbk, cbv, bsa, bsb, s):
    nb, la, _ = cak.shape
    lb = cbk.shape[1]
    ka_pad, kb_pad = bsa.shape[-1], bsb.shape[-1]
    assert s % 16 == 0 and la % 16 == 0 and lb % 16 == 0
    row = lambda w: pl.BlockSpec((s, w), lambda b: (b, 0))
    cache = lambda n, w: pl.BlockSpec((None, n, w), lambda b: (b, 0, 0))
    full = lambda a: pl.BlockSpec(a.shape, lambda b: (0,) * a.ndim)
    return pl.pallas_call(
        functools.partial(_sample_attn_kernel, s=s, la=la, lb=lb),
        grid=(nb,),
        in_specs=[pl.BlockSpec(memory_space=pltpu.SMEM), row(IN_WIDTH),
                  cache(la, A_KVW), cache(la, A_KVW), cache(lb, B_W), cache(lb, B_W), full(bsa), full(bsb)],
        out_specs=(row(A_QW), row(B_W)),
        out_shape=(jax.ShapeDtypeStruct((nb * s, A_QW), F32), jax.ShapeDtypeStruct((nb * s, B_W), F32)),
        scratch_shapes=[pltpu.VMEM((ka_pad, A_KVW), BF16), pltpu.VMEM((ka_pad, A_KVW), BF16),
                        pltpu.VMEM((kb_pad, B_W), BF16), pltpu.VMEM((kb_pad, B_W), BF16)],
        compiler_params=pltpu.CompilerParams(dimension_semantics=("arbitrary",)),
        name="sample_attn",
    )(sink, p, cak, cav, cbk, cbv, bsa, bsb)


def _sample_tail_kernel(x_ref, oa_ref, ob_ref, ga_ref, gb_ref, woa_ref, wob_ref, wout_ref,
                        gpost_ref, gpre2_ref, gpost2_ref, init1_ref, init2_ref,
                        wup_ref, wgate_ref, cw_ref, cb_ref, wdown_ref,
                        x2_ref, u_ref, x1_s, xn_s, acc_s, *, s):
    c = pl.program_id(0)

    @pl.when(c == 0)
    def _():
        ya = _dot(oa_ref[...].astype(BF16), woa_ref[...])
        yb = _dot(ob_ref[...].astype(BF16), wob_ref[...])
        merged = jax.nn.sigmoid(ga_ref[...]) * ya + jax.nn.sigmoid(gb_ref[...]) * yb
        z = _dot(merged.astype(BF16), wout_ref[...])
        x1 = x_ref[...] + _rms(z, gpost_ref[...])
        x1_s[...] = x1
        xn_s[...] = _rms(x1, gpre2_ref[...]).astype(BF16)
        acc_s[...] = jnp.zeros_like(acc_s)

    xn = xn_s[...]
    u = _dot(xn, wup_ref[...])
    gt = _dot(xn, wgate_ref[...])
    u_ref[...] = u
    pos = lax.broadcasted_iota(jnp.int32, u.shape, 0) % s
    u1 = jnp.where(pos == 0, init1_ref[...], pltpu.roll(u, 1, 0))
    u2 = jnp.where(pos < 2, init2_ref[...], pltpu.roll(u, 2, 0))
    act = _conv_gelu_gate(u, gt, u1, u2, cw_ref[...], cb_ref[...])
    acc_s[...] += _dot(act.astype(BF16), wdown_ref[...])

    @pl.when(c == pl.num_programs(0) - 1)
    def _():
        x2_ref[...] = x1_s[...] + _rms(acc_s[...], gpost2_ref[...])


def _sample_tail(xs, oa, ob, ga, gb, w, gpost, gpre2, gpost2, init1, init2, s):
    rows = xs.shape[0]
    full = lambda a: pl.BlockSpec(a.shape, lambda c: (0,) * a.ndim)
    chunk_w = lambda a: pl.BlockSpec((None,) + a.shape[1:], lambda c: (c, 0, 0))
    cols = pl.BlockSpec((rows, FF_CHUNK), lambda c: (0, c))
    args = [xs, oa, ob, ga, gb, w["oa"], w["ob"], w["out"], gpost, gpre2, gpost2]
    return pl.pallas_call(
        functools.partial(_sample_tail_kernel, s=s),
        grid=(N_FF,),
        in_specs=[full(a) for a in args] + [cols, cols] + [chunk_w(w[k]) for k in ("up", "gate", "cw", "cb", "down")],
        out_specs=(full(xs), cols),
        out_shape=(jax.ShapeDtypeStruct((rows, D_MODEL), F32), jax.ShapeDtypeStruct((rows, D_FF), F32)),
        scratch_shapes=[pltpu.VMEM((rows, D_MODEL), F32), pltpu.VMEM((rows, D_MODEL), BF16),
                        pltpu.VMEM((rows, D_MODEL), F32)],
        compiler_params=pltpu.CompilerParams(dimension_semantics=("arbitrary",)),
        name="sample_tail",
    )(*args, init1, init2, w["up"], w["gate"], w["cw"], w["cb"], w["down"])


def _prep_weights(w_in, w_oa, w_ob, w_out, w_upg, conv_w, conv_b, w_down):
    wb = w_in.astype(BF16)
    col = lambda a, n: wb[:, a:a + n]
    up = w_upg[:, :D_FF].astype(BF16).reshape(D_MODEL, N_FF, FF_CHUNK).transpose(1, 0, 2)
    gate = w_upg[:, D_FF:].astype(BF16).reshape(D_MODEL, N_FF, FF_CHUNK).transpose(1, 0, 2)
    return {
        "in": wb,
        "ka": col(_C_KA, A_KVW), "kb": col(_C_KB, B_W), "ga": col(_C_GA, D_MODEL), "gb": col(_C_GB, D_MODEL),
        "qt": jnp.concatenate([col(_C_QA, A_QW), col(_C_QB, B_W)], axis=1).T,
        "vat": col(_C_VA, A_KVW).T, "vbt": col(_C_VB, B_W).T,
        "oa": w_oa.astype(BF16), "ob": w_ob.astype(BF16), "out": w_out.astype(BF16),
        "up": up, "gate": gate,
        "cw": conv_w.reshape(3, N_FF, FF_CHUNK).transpose(1, 0, 2),
        "cb": conv_b.reshape(N_FF, 1, FF_CHUNK),
        "down": w_down.astype(BF16).reshape(N_FF, FF_CHUNK, D_MODEL),
    }


def _round_up(n, m):
    return (n + m - 1) // m * m


def kernel(x_prompt, x_sample, cache_a_k, cache_a_v, cache_b_k, cache_b_v, state_conv,
           w_in, w_oa, w_ob, w_out, sink_a, t5_table, rel_table_b,
           g_pre_mix, g_post_mix, g_pre_ffn, g_post_ffn, w_upg, conv_w, conv_b, w_down):
    depth = w_in.shape[0]
    bsz, seq, _ = x_prompt.shape
    nb, s, _ = x_sample.shape
    la, lb = cache_a_k.shape[2], cache_b_k.shape[2]
    ka_pad = _round_up(la + s, LANES)
    kb_pad = _round_up(lb + s, LANES)

    xp = x_prompt
    xs = x_sample.reshape(nb * s, D_MODEL)
    prompt_states = [[] for _ in range(5)]
    sample_states = [[] for _ in range(5)]
    for l in range(depth):
        w = _prep_weights(w_in[l], w_oa[l], w_ob[l], w_out[l], w_upg[l], conv_w[l], conv_b[l], w_down[l])
        row = lambda g: g[l].reshape(1, D_MODEL)
        bta, btb, bsa, bsb = _build_bias(t5_table, rel_table_b[l], s, la, lb, ka_pad, kb_pad)
        sink_row = jnp.repeat(sink_a[l], GROUP).reshape(1, A_HEADS * GROUP)

        x1, ka_t, va_t, kb_t, vb_t = _prompt_mixer(xp, row(g_pre_mix), row(g_post_mix), w, sink_row, bta, btb)
        xp, conv_t = _prompt_ffn(x1, row(g_pre_ffn), row(g_post_ffn), w)
        prompt_states[0].append(ka_t[:, -A_PREV:].reshape(bsz, A_PREV, A_KV_HEADS, HEAD_DIM))
        prompt_states[1].append(va_t[:, -A_PREV:].reshape(bsz, A_PREV, A_KV_HEADS, HEAD_DIM))
        prompt_states[2].append(kb_t[:, -B_PREV:].reshape(bsz, B_PREV, B_HEADS, HEAD_DIM))
        prompt_states[3].append(vb_t[:, -B_PREV:].reshape(bsz, B_PREV, B_HEADS, HEAD_DIM))
        prompt_states[4].append(conv_t[:, -2:])

        p = _sample_proj(xs, row(g_pre_mix), w["in"])
        oa, ob = _sample_attn(sink_a[l], p,
                              cache_a_k[l].reshape(nb, la, A_KVW), cache_a_v[l].reshape(nb, la, A_KVW),
                              cache_b_k[l].reshape(nb, lb, B_W), cache_b_v[l].reshape(nb, lb, B_W), bsa, bsb, s)
        st = state_conv[l]
        zeros = jnp.zeros((nb, s, D_FF), F32)
        init1 = zeros.at[:, 0].set(st[:, 1]).reshape(nb * s, D_FF)
        init2 = zeros.at[:, 0].set(st[:, 0]).at[:, 1].set(st[:, 1]).reshape(nb * s, D_FF)
        xs, u = _sample_tail(xs, oa, ob, p[:, _C_GA:_C_GA + D_MODEL], p[:, _C_GB:_C_GB + D_MODEL], w,
                             row(g_post_mix), row(g_pre_ffn), row(g_post_ffn), init1, init2, s)
        p3 = p.reshape(nb, s, IN_WIDTH)
        sample_states[0].append(p3[:, :, _C_KA:_C_KA + A_KVW].reshape(nb, s, A_KV_HEADS, HEAD_DIM))
        sample_states[1].append(p3[:, :, _C_VA:_C_VA + A_KVW].reshape(nb, s, A_KV_HEADS, HEAD_DIM))
        sample_states[2].append(p3[:, :, _C_KB:_C_KB + B_W].reshape(nb, s, B_HEADS, HEAD_DIM))
        sample_states[3].append(p3[:, :, _C_VB:_C_VB + B_W].reshape(nb, s, B_HEADS, HEAD_DIM))
        sample_states[4].append(u.reshape(nb, s, D_FF)[:, -2:])

    ps = [jnp.stack(a, axis=0) for a in prompt_states]
    ss = [jnp.stack(a, axis=0) for a in sample_states]
    return (xp, xs.reshape(nb, s, D_MODEL), ps[0], ps[1], ps[2], ps[3], ps[4],
            ss[0], ss[1], ss[2], ss[3], ss[4])
```

```python
import functools
import math

import numpy as np
import jax
import jax.numpy as jnp
from jax import lax
from jax.experimental import pallas as pl
from jax.experimental.pallas import tpu as pltpu

F32 = jnp.float32
BF16 = jnp.bfloat16

D_MODEL = 1024
CHUNK = 64
HEAD_DIM = 64
EPS = 1e-6
NEG_INF = -1e30
QK_SCALE = HEAD_DIM ** -0.5

A_HEADS = 8
A_KV_HEADS = 2
A_GROUP = A_HEADS // A_KV_HEADS
A_PREV = 128
T5_BUCKETS = 32
T5_MAX_DIST = 128

B_HEADS = 8
B_PREV = 512
B_REL_CLIP = 128

A_QW = A_HEADS * HEAD_DIM
A_KVW = A_KV_HEADS * HEAD_DIM
B_W = B_HEADS * HEAD_DIM
D_FF = 3072

_C_QA = 0
_C_KA = _C_QA + A_QW
_C_VA = _C_KA + A_KVW
_C_QB = _C_VA + A_KVW
_C_KB = _C_QB + B_W
_C_VB = _C_KB + B_W
_C_GA = _C_VB + B_W
_C_GB = _C_GA + D_MODEL
IN_WIDTH = _C_GB + D_MODEL

LANES = 128
GROUP = 2 * CHUNK
A_WIN = A_PREV + GROUP
B_WIN = B_PREV + GROUP
SEQ_TILE = 512
FF_CHUNK = 512
N_FF = D_FF // FF_CHUNK
VMEM_LIMIT = 56 * 1024 * 1024


def _rms(x, g):
    y = x * lax.rsqrt(jnp.mean(x * x, axis=-1, keepdims=True) + EPS)
    return y * g


def _dot(a, b):
    return jnp.dot(a, b, preferred_element_type=F32)


def _dot_nt(a, b):
    return lax.dot_general(a, b, (((1,), (1,)), ((), ())), preferred_element_type=F32)


def _t5_bucket(n):
    half = T5_BUCKETS // 2
    max_exact = half // 2
    ret = jnp.where(n < 0, half, 0)
    a = jnp.abs(n)
    af = jnp.maximum(a, 1).astype(jnp.float32)
    large = max_exact + (jnp.log(af / max_exact) / math.log(T5_MAX_DIST / max_exact)
                         * (half - max_exact)).astype(jnp.int32)
    large = jnp.minimum(large, half - 1)
    return ret + jnp.where(a < max_exact, a, large)


def _window_valid(n_prev, win):
    j = np.arange(win)[:, None]
    ci = np.arange(GROUP)[None, :] // CHUNK
    return ((j >= CHUNK * ci) & (j < CHUNK * ci + n_prev + CHUNK)).astype(np.int32)


def _lookup(idx, table, lo, hi):
    acc = jnp.zeros((table.shape[0], idx.shape[1]), F32)
    for b in range(lo, hi):
        acc = jnp.where(idx == b, table[:, b:b + 1], acc)
    return acc


def _toeplitz(vec, rows, first):
    w = vec.shape[1]
    return pltpu.roll(jnp.broadcast_to(vec, (rows, w)), (w - first) % w, 1, stride=1, stride_axis=0)


def _bias_kernel(t5_ref, rel_ref, ja_ref, jb_ref, jsa_ref, jsb_ref, va_ref, vb_ref, vsa_ref, vsb_ref,
                 bta_ref, btb_ref, bsa_ref, bsb_ref, *, s, jb_range, jsb_range):
    t5 = t5_ref[...]
    rel = rel_ref[...]

    fa = _lookup(ja_ref[...], t5, 0, T5_BUCKETS)
    va = va_ref[...] != 0
    for h in range(A_HEADS):
        t = _toeplitz(fa[h:h + 1, :], A_WIN, A_WIN - 1)[:, 0:GROUP]
        bta_ref[:, GROUP * h:GROUP * (h + 1)] = jnp.where(va, t, NEG_INF)

    fb = _lookup(jb_ref[...], rel, *jb_range)
    for r in range(B_WIN // LANES):
        base = B_WIN - LANES * r
        vb = vb_ref[LANES * r:LANES * (r + 1), :] != 0
        for h in range(B_HEADS):
            seg = fb[h:h + 1, base - LANES:base + LANES]
            t = _toeplitz(seg, LANES, LANES)[:, 0:GROUP]
            btb_ref[h, LANES * r:LANES * (r + 1), :] = jnp.where(vb, t, NEG_INF)

    fsa = _lookup(jsa_ref[...], t5, 0, T5_BUCKETS)
    fsb = _lookup(jsb_ref[...], rel, *jsb_range)
    vsa = vsa_ref[...] != 0
    vsb = vsb_ref[...] != 0
    for h in range(A_HEADS):
        bsa_ref[h] = jnp.where(vsa, _toeplitz(fsa[h:h + 1, :], s, s - 1), NEG_INF)
    for h in range(B_HEADS):
        bsb_ref[h] = jnp.where(vsb, _toeplitz(fsb[h:h + 1, :], s, s - 1), NEG_INF)


def _build_bias(t5_table, rel_table, s, la, lb, ka_pad, kb_pad):
    wa = 3 * LANES
    assert A_WIN - 1 + GROUP <= wa and la + 2 * s - 2 < ka_pad and lb + 2 * s - 2 < kb_pad
    ja = _t5_bucket(jnp.arange(wa) - (A_WIN - 1) + A_PREV).astype(jnp.int32).reshape(1, wa)
    jb = np.clip(np.arange(B_WIN + GROUP) - B_WIN + B_PREV, -B_REL_CLIP, B_REL_CLIP) + B_REL_CLIP
    jsa = _t5_bucket(la + s - 1 - jnp.arange(ka_pad)).astype(jnp.int32).reshape(1, ka_pad)
    jsb = np.clip(lb + s - 1 - np.arange(kb_pad), -B_REL_CLIP, B_REL_CLIP) + B_REL_CLIP
    key = lambda n: np.broadcast_to(np.arange(n)[None, :], (s, n))
    vsa = (key(ka_pad) < la + s).astype(np.int32)
    vsb = (key(kb_pad) < lb + s).astype(np.int32)
    as_row = lambda v: jnp.asarray(v.astype(np.int32).reshape(1, -1))
    rng = lambda v: (int(v.min()), int(v.max()) + 1)

    vmem = pl.BlockSpec(memory_space=pltpu.VMEM)
    return pl.pallas_call(
        functools.partial(_bias_kernel, s=s, jb_range=rng(jb), jsb_range=rng(jsb)),
        out_shape=(
            jax.ShapeDtypeStruct((A_WIN, A_HEADS * GROUP), F32),
            jax.ShapeDtypeStruct((B_HEADS, B_WIN, GROUP), F32),
            jax.ShapeDtypeStruct((A_HEADS, s, ka_pad), F32),
            jax.ShapeDtypeStruct((B_HEADS, s, kb_pad), F32),
        ),
        in_specs=[vmem] * 10,
        out_specs=(vmem,) * 4,
        name="bias_tables",
    )(t5_table, rel_table, ja, as_row(jb), jsa, as_row(jsb),
      jnp.asarray(_window_valid(A_PREV, A_WIN)), jnp.asarray(_window_valid(B_PREV, B_WIN)),
      jnp.asarray(vsa), jnp.asarray(vsb))


def _ordering_zero(x, never):
    return jnp.where(never, x, 0.0)


def _add_to_corner(x, z):
    head = jnp.concatenate([x[0:16, 0:LANES] + z, x[0:16, LANES:]], axis=1)
    return jnp.concatenate([head, x[16:, :]], axis=0)


def _scores_a(g, skip, ka_s, qt_s):
    tok = slice(GROUP * g, GROUP * (g + 1))
    r0 = GROUP * g + skip
    kwin = ka_s[r0:r0 + A_WIN - skip, :]
    q = qt_s[0:A_QW, tok]
    zero = jnp.zeros((HEAD_DIM, GROUP), BF16)
    cols = []
    for i in range(A_HEADS):
        qi = q[HEAD_DIM * i:HEAD_DIM * (i + 1), :]
        cols.append(jnp.concatenate([qi, zero] if i < A_GROUP else [zero, qi], axis=0))
    return _dot(kwin, jnp.concatenate(cols, axis=1))


def _scores_b(g, p2, skip, kb_s, qt_s):
    tok = slice(GROUP * g, GROUP * (g + 1))
    r0 = GROUP * g + skip
    kwin = kb_s[r0:r0 + B_WIN - skip, LANES * p2:LANES * (p2 + 1)]
    qrow = A_QW + LANES * p2
    qe = qt_s[qrow:qrow + HEAD_DIM, tok]
    qo = qt_s[qrow + HEAD_DIM:qrow + 2 * HEAD_DIM, tok]
    zero = jnp.zeros((HEAD_DIM, GROUP), BF16)
    qbd = jnp.concatenate([jnp.concatenate([qe, zero], axis=0),
                           jnp.concatenate([zero, qo], axis=0)], axis=1)
    return _dot(kwin, qbd)


def _softmax_keys(st, bias, sink_row, order_after):
    st = st + bias
    m = jnp.max(st, axis=0, keepdims=True)
    if sink_row is not None:
        m = jnp.maximum(m, sink_row)
    p = jnp.exp(st - m)
    l = jnp.sum(p, axis=0, keepdims=True)
    if sink_row is not None:
        l = l + jnp.exp(sink_row - m)
    if order_after is not None:
        p = _add_to_corner(p, order_after)
    return p.astype(BF16), 1.0 / l


def _values_a(g, skip, vat_s, p, rl, oa_s):
    tok = slice(GROUP * g, GROUP * (g + 1))
    r0 = GROUP * g + skip
    vwin = vat_s[:, r0:r0 + A_WIN - skip]
    ot = _dot(vwin, p) * rl
    for j in range(A_HEADS // 2):
        parts = []
        for i in (2 * j, 2 * j + 1):
            rows = slice(0, HEAD_DIM) if i < A_GROUP else slice(HEAD_DIM, 2 * HEAD_DIM)
            parts.append(ot[rows, GROUP * i:GROUP * (i + 1)])
        blk = jnp.concatenate(parts, axis=0)
        oa_s[tok, LANES * j:LANES * (j + 1)] = blk.T.astype(BF16)


def _values_b(g, p2, skip, vbt_s, p, rl, ob_s):
    tok = slice(GROUP * g, GROUP * (g + 1))
    r0 = GROUP * g + skip
    lanes = slice(LANES * p2, LANES * (p2 + 1))
    vwin = vbt_s[lanes, r0:r0 + B_WIN - skip]
    ot = _dot(vwin, p) * rl
    blk = jnp.concatenate([ot[0:HEAD_DIM, 0:GROUP], ot[HEAD_DIM:2 * HEAD_DIM, GROUP:2 * GROUP]], axis=0)
    ob_s[tok, lanes] = blk.T.astype(BF16)


def _attend_tile(n_groups, first, never, ka_s, vat_s, kb_s, vbt_s, qt_s, bta_ref, btb_ref, sink_row, oa_s, ob_s):
    steps = []
    for g in range(n_groups):
        skip_a = max(A_PREV - GROUP * g, 0) if first else 0
        skip_b = max(B_PREV - GROUP * g, 0) if first else 0
        steps.append(("a", g, 0, skip_a))
        steps.extend(("b", g, p2, skip_b) for p2 in range(B_HEADS // 2))

    def scores(step):
        kind, g, p2, skip = step
        return _scores_a(g, skip, ka_s, qt_s) if kind == "a" else _scores_b(g, p2, skip, kb_s, qt_s)

    st_next = scores(steps[0])
    for k, (kind, g, p2, skip) in enumerate(steps):
        st, order_after = st_next, None
        if k + 1 < len(steps):
            st_next = scores(steps[k + 1])
            order_after = _ordering_zero(st_next[0:16, 0:LANES], never)
        if kind == "a":
            p, rl = _softmax_keys(st, bta_ref[skip:A_WIN, :], sink_row, order_after)
            _values_a(g, skip, vat_s, p, rl, oa_s)
        else:
            bias = jnp.concatenate([btb_ref[2 * p2, skip:B_WIN, :], btb_ref[2 * p2 + 1, skip:B_WIN, :]], axis=1)
            p, rl = _softmax_keys(st, bias, None, order_after)
            _values_b(g, p2, skip, vbt_s, p, rl, ob_s)


def _mixer_kernel(x_ref, gpre_ref, gpost_ref, win_ref, wt_ref,
                  woa_ref, wob_ref, wout_ref, sink_ref, bta_ref, btb_ref, never_ref,
                  x1_ref, kat_ref, vat_ref, kbt_ref, vbt_ref,
                  h_s, qt_s, ka_s, vat_s, kb_s, vbt_s, oa_s, ob_s, mg_s, *, tile, n_tiles):
    i = pl.program_id(1)
    x = x_ref[...]
    h = _rms(x, gpre_ref[...]).astype(BF16)
    h_s[...] = h

    ka = _dot(h, win_ref[:, _C_KA:_C_KA + A_KVW])
    kb = _dot(h, win_ref[:, _C_KB:_C_KB + B_W])
    ka_s[A_PREV:A_PREV + tile, :] = ka.astype(BF16)
    kb_s[B_PREV:B_PREV + tile, :] = kb.astype(BF16)
    nq = A_QW + B_W
    qt_s[...] = (_dot_nt(wt_ref[0:nq, :], h) * QK_SCALE).astype(BF16)
    vat = _dot_nt(wt_ref[nq:nq + A_KVW, :], h)
    vbt = _dot_nt(wt_ref[nq + A_KVW:nq + A_KVW + B_W, :], h)
    vat_s[:, A_PREV:A_PREV + tile] = vat.astype(BF16)
    vbt_s[:, B_PREV:B_PREV + tile] = vbt.astype(BF16)

    @pl.when(i == n_tiles - 1)
    def _():
        kat_ref[...] = ka[tile - A_PREV:, :]
        kbt_ref[...] = kb[tile - B_PREV:, :]
        vat_ref[...] = vat[:, tile - A_PREV:].T
        vbt_ref[...] = vbt[:, tile - B_PREV:].T

    sink_row = sink_ref[...]
    n_groups = tile // GROUP

    never = never_ref[...] != 0
    attend = functools.partial(_attend_tile, n_groups, never=never, ka_s=ka_s, vat_s=vat_s, kb_s=kb_s, vbt_s=vbt_s,
                               qt_s=qt_s, bta_ref=bta_ref, btb_ref=btb_ref, sink_row=sink_row, oa_s=oa_s, ob_s=ob_s)
    pl.when(i == 0)(functools.partial(attend, True))
    pl.when(i > 0)(functools.partial(attend, False))

    ka_s[0:A_PREV, :] = ka_s[tile:tile + A_PREV, :]
    vat_s[:, 0:A_PREV] = vat_s[:, tile:tile + A_PREV]
    kb_s[0:B_PREV, :] = kb_s[tile:tile + B_PREV, :]
    vbt_s[:, 0:B_PREV] = vbt_s[:, tile:tile + B_PREV]

    oa = oa_s[...]
    ob = ob_s[...]
    hh = h_s[...]
    cw = 256
    for c in range(D_MODEL // cw):
        cs = slice(cw * c, cw * (c + 1))
        ya = _dot(oa, woa_ref[:, cs])
        yb = _dot(ob, wob_ref[:, cs])
        ga = _dot(hh, win_ref[:, _C_GA + cw * c:_C_GA + cw * (c + 1)])
        gb = _dot(hh, win_ref[:, _C_GB + cw * c:_C_GB + cw * (c + 1)])
        mg_s[:, cs] = (jax.nn.sigmoid(ga) * ya + jax.nn.sigmoid(gb) * yb).astype(BF16)
    z = _dot(mg_s[...], wout_ref[...])
    x1_ref[...] = x_ref[...] + _rms(z, gpost_ref[...])


def _const_spec(shape):
    nd = len(shape)
    return pl.BlockSpec(shape, lambda *_: (0,) * nd, pipeline_mode=pl.Buffered(1))


def _prompt_mixer(x, gpre, gpost, w, sink_row, bta, btb):
    bsz, seq, _ = x.shape
    tile = SEQ_TILE
    assert seq % tile == 0 and tile % GROUP == 0 and tile >= B_PREV
    n_tiles = seq // tile
    row_spec = pl.BlockSpec((None, tile, D_MODEL), lambda b, i: (b, i, 0))

    def tail_spec(rows, width):
        return pl.BlockSpec((None, rows, width), lambda b, i: (b, 0, 0))

    consts = [gpre, gpost, w["in"], w["qvt"],
              w["oa"], w["ob"], w["out"], sink_row, bta, btb, jnp.zeros((16, LANES), jnp.int32)]
    return pl.pallas_call(
        functools.partial(_mixer_kernel, tile=tile, n_tiles=n_tiles),
        grid=(bsz, n_tiles),
        in_specs=[row_spec] + [_const_spec(c.shape) for c in consts],
        out_specs=(row_spec, tail_spec(A_PREV, A_KVW), tail_spec(A_PREV, A_KVW),
                   tail_spec(B_PREV, B_W), tail_spec(B_PREV, B_W)),
        out_shape=(
            jax.ShapeDtypeStruct((bsz, seq, D_MODEL), F32),
            jax.ShapeDtypeStruct((bsz, A_PREV, A_KVW), F32),
            jax.ShapeDtypeStruct((bsz, A_PREV, A_KVW), F32),
            jax.ShapeDtypeStruct((bsz, B_PREV, B_W), F32),
            jax.ShapeDtypeStruct((bsz, B_PREV, B_W), F32),
        ),
        scratch_shapes=[
            pltpu.VMEM((tile, D_MODEL), BF16),
            pltpu.VMEM((A_QW + B_W, tile), BF16),
            pltpu.VMEM((A_PREV + tile, A_KVW), BF16),
            pltpu.VMEM((A_KVW, A_PREV + tile), BF16),
            pltpu.VMEM((B_PREV + tile, B_W), BF16),
            pltpu.VMEM((B_W, B_PREV + tile), BF16),
            pltpu.VMEM((tile, A_QW), BF16),
            pltpu.VMEM((tile, B_W), BF16),
            pltpu.VMEM((tile, D_MODEL), BF16),
        ],
        compiler_params=pltpu.CompilerParams(
            dimension_semantics=("arbitrary", "arbitrary"), vmem_limit_bytes=VMEM_LIMIT),
        name="prompt_mixer",
    )(x, *consts)


def _conv_gelu_gate(u, gt, u1, u2, cw, cb):
    c = cb + cw[0:1, :] * u2
    c = c + cw[1:2, :] * u1
    c = c + cw[2:3, :] * u
    return jax.nn.gelu(c, approximate=True) * gt


def _ffn_kernel(x_ref, gpre_ref, gpost_ref, wupg_ref, cw_ref, cb_ref, wdown_ref, never_ref,
                x2_ref, tail_ref, carry_s, *, tile):
    i = pl.program_id(1)

    @pl.when(i == 0)
    def _():
        carry_s[...] = jnp.zeros_like(carry_s)

    x1 = x_ref[...]
    xn = _rms(x1, gpre_ref[...]).astype(BF16)
    never = never_ref[...] != 0
    row = lax.broadcasted_iota(jnp.int32, (tile, FF_CHUNK), 0)
    acc = jnp.zeros((tile, D_MODEL), F32)

    def up_gate(c):
        cols = slice(FF_CHUNK * c, FF_CHUNK * (c + 1))
        gcols = slice(D_FF + FF_CHUNK * c, D_FF + FF_CHUNK * (c + 1))
        return _dot(xn, wupg_ref[:, cols]), _dot(xn, wupg_ref[:, gcols])

    up_next = up_gate(0)
    for c in range(N_FF):
        u, gt = up_next
        cols = slice(FF_CHUNK * c, FF_CHUNK * (c + 1))
        prev = carry_s[c]
        u1 = jnp.where(row == 0, prev[7:8, :], pltpu.roll(u, 1, 0))
        u2 = jnp.where(row == 0, prev[6:7, :], jnp.where(row == 1, prev[7:8, :], pltpu.roll(u, 2, 0)))
        carry_s[c] = u[tile - 8:tile, :]
        tail_ref[:, cols] = u[tile - 2:tile, :]
        act = _conv_gelu_gate(u, gt, u1, u2, cw_ref[:, cols], cb_ref[:, cols])
        if c + 1 < N_FF:
            up_next = up_gate(c + 1)
            act = _add_to_corner(act, _ordering_zero(up_next[0][0:16, 0:LANES], never))
        acc = acc + _dot(act.astype(BF16), wdown_ref[cols, :])
    x2_ref[...] = x1 + _rms(acc, gpost_ref[...])


def _prompt_ffn(x1, gpre, gpost, w):
    bsz, seq, _ = x1.shape
    tile = SEQ_TILE
    assert seq % tile == 0
    row_spec = pl.BlockSpec((None, tile, D_MODEL), lambda b, i: (b, i, 0))
    consts = [gpre, gpost, w["upg"], w["cw"], w["cb"], w["down"], jnp.zeros((16, LANES), jnp.int32)]
    return pl.pallas_call(
        functools.partial(_ffn_kernel, tile=tile),
        grid=(bsz, seq // tile),
        in_specs=[row_spec] + [_const_spec(c.shape) for c in consts],
        out_specs=(row_spec, pl.BlockSpec((None, 2, D_FF), lambda b, i: (b, 0, 0))),
        out_shape=(jax.ShapeDtypeStruct((bsz, seq, D_MODEL), F32),
                   jax.ShapeDtypeStruct((bsz, 2, D_FF), F32)),
        scratch_shapes=[pltpu.VMEM((N_FF, 8, FF_CHUNK), F32)],
        compiler_params=pltpu.CompilerParams(
            dimension_semantics=("arbitrary", "arbitrary"), vmem_limit_bytes=VMEM_LIMIT),
        name="prompt_ffn",
    )(x1, *consts)


def _sample_proj_kernel(x_ref, g_ref, w_ref, p_ref):
    h = _rms(x_ref[...], g_ref[...]).astype(BF16)
    p_ref[...] = _dot(h, w_ref[...])


def _sample_proj(xs, gpre, w_in_b):
    rows = xs.shape[0]
    nblk = 2
    width = IN_WIDTH // nblk
    assert width % LANES == 0
    return pl.pallas_call(
        _sample_proj_kernel,
        grid=(nblk,),
        in_specs=[pl.BlockSpec((rows, D_MODEL), lambda j: (0, 0)),
                  pl.BlockSpec((1, D_MODEL), lambda j: (0, 0)),
                  pl.BlockSpec((D_MODEL, width), lambda j: (0, j))],
        out_specs=pl.BlockSpec((rows, width), lambda j: (0, j)),
        out_shape=jax.ShapeDtypeStruct((rows, IN_WIDTH), F32),
        compiler_params=pltpu.CompilerParams(dimension_semantics=("arbitrary",)),
        name="sample_proj",
    )(xs, gpre, w_in_b)


def _sample_attn_kernel(sink_ref, p_ref, cak_ref, cav_ref, cbk_ref, cbv_ref, bsa_ref, bsb_ref,
                        oa_ref, ob_ref, kan_ref, van_ref, kbn_ref, vbn_ref, ka_s, va_s, kb_s, vb_s, *, s, la, lb):
    p = p_ref[...]
    kan_ref[...] = p[:, _C_KA:_C_KA + A_KVW]
    van_ref[...] = p[:, _C_VA:_C_VA + A_KVW]
    kbn_ref[...] = p[:, _C_KB:_C_KB + B_W]
    vbn_ref[...] = p[:, _C_VB:_C_VB + B_W]
    ka_pad = ka_s.shape[0]
    kb_pad = kb_s.shape[0]

    def fill(dst, cache_ref, new, n_cache, n_pad):
        dst[0:n_cache, :] = cache_ref[...].astype(BF16)
        dst[n_cache:n_cache + s, :] = new.astype(BF16)
        dst[n_cache + s:n_pad, :] = jnp.zeros((n_pad - n_cache - s, dst.shape[1]), BF16)

    fill(ka_s, cak_ref, p[:, _C_KA:_C_KA + A_KVW], la, ka_pad)
    fill(va_s, cav_ref, p[:, _C_VA:_C_VA + A_KVW], la, ka_pad)
    fill(kb_s, cbk_ref, p[:, _C_KB:_C_KB + B_W], lb, kb_pad)
    fill(vb_s, cbv_ref, p[:, _C_VB:_C_VB + B_W], lb, kb_pad)

    lo = lax.broadcasted_iota(jnp.int32, (s, LANES), 1) < HEAD_DIM
    top = lax.broadcasted_iota(jnp.int32, (2 * s, 1), 0) < s

    def softmax_pv(qs, k, v, bias, sink):
        st = _dot_nt(qs.astype(BF16), k) + bias
        m = jnp.max(st, axis=-1, keepdims=True)
        if sink is not None:
            m = jnp.maximum(m, sink)
        e = jnp.exp(st - m)
        l = jnp.sum(e, axis=-1, keepdims=True)
        if sink is not None:
            l = l + jnp.exp(sink - m)
        return _dot(e.astype(BF16), v) * (1.0 / l)

    ka = ka_s[...]
    va = va_s[...]
    for j in range(A_HEADS // 2):
        q = p[:, _C_QA + LANES * j:_C_QA + LANES * (j + 1)] * QK_SCALE
        qr = pltpu.roll(q, HEAD_DIM, 1)
        if 2 * j < A_GROUP:
            qs = jnp.concatenate([jnp.where(lo, q, 0.0), jnp.where(lo, qr, 0.0)], axis=0)
        else:
            qs = jnp.concatenate([jnp.where(lo, 0.0, qr), jnp.where(lo, 0.0, q)], axis=0)
        bias = jnp.concatenate([bsa_ref[2 * j], bsa_ref[2 * j + 1]], axis=0)
        sink = jnp.where(top, sink_ref[2 * j], sink_ref[2 * j + 1])
        o = softmax_pv(qs, ka, va, bias, sink)
        oe, oo = o[0:s], o[s:2 * s]
        if 2 * j < A_GROUP:
            out = jnp.where(lo, oe, pltpu.roll(oo, HEAD_DIM, 1))
        else:
            out = jnp.where(lo, pltpu.roll(oe, HEAD_DIM, 1), oo)
        oa_ref[:, LANES * j:LANES * (j + 1)] = out

    for j in range(B_HEADS // 2):
        lanes = slice(LANES * j, LANES * (j + 1))
        q = p[:, _C_QB + LANES * j:_C_QB + LANES * (j + 1)] * QK_SCALE
        qs = jnp.concatenate([jnp.where(lo, q, 0.0), jnp.where(lo, 0.0, q)], axis=0)
        bias = jnp.concatenate([bsb_ref[2 * j], bsb_ref[2 * j + 1]], axis=0)
        o = softmax_pv(qs, kb_s[:, lanes], vb_s[:, lanes], bias, None)
        ob_ref[:, lanes] = jnp.where(lo, o[0:s], o[s:2 * s])


def _sample_attn(sink, p, cak, cav, cbk, cbv, bsa, bsb, s):
    nb, la, _ = cak.shape
    lb = cbk.shape[1]
    ka_pad, kb_pad = bsa.shape[-1], bsb.shape[-1]
    assert s % 16 == 0 and la % 16 == 0 and lb % 16 == 0
    row = lambda w: pl.BlockSpec((s, w), lambda b: (b, 0))
    cache = lambda n, w: pl.BlockSpec((None, n, w), lambda b: (b, 0, 0))
    full = lambda a: pl.BlockSpec(a.shape, lambda b: (0,) * a.ndim)
    widths = (A_QW, B_W, A_KVW, A_KVW, B_W, B_W)
    return pl.pallas_call(
        functools.partial(_sample_attn_kernel, s=s, la=la, lb=lb),
        grid=(nb,),
        in_specs=[pl.BlockSpec(memory_space=pltpu.SMEM), row(IN_WIDTH),
                  cache(la, A_KVW), cache(la, A_KVW), cache(lb, B_W), cache(lb, B_W), full(bsa), full(bsb)],
        out_specs=tuple(row(w) for w in widths),
        out_shape=tuple(jax.ShapeDtypeStruct((nb * s, w), F32) for w in widths),
        scratch_shapes=[pltpu.VMEM((ka_pad, A_KVW), BF16), pltpu.VMEM((ka_pad, A_KVW), BF16),
                        pltpu.VMEM((kb_pad, B_W), BF16), pltpu.VMEM((kb_pad, B_W), BF16)],
        compiler_params=pltpu.CompilerParams(dimension_semantics=("arbitrary",)),
        name="sample_attn",
    )(sink, p, cak, cav, cbk, cbv, bsa, bsb)


def _sample_tail_kernel(x_ref, oa_ref, ob_ref, p_ref, woa_ref, wob_ref, wout_ref,
                        gpost_ref, gpre2_ref, gpost2_ref, init_ref,
                        wup_ref, wgate_ref, cw_ref, cb_ref, wdown_ref,
                        x2_ref, u_ref, x1_s, xn_s, acc_s, *, s):
    c = pl.program_id(0)

    @pl.when(c == 0)
    def _():
        ya = _dot(oa_ref[...].astype(BF16), woa_ref[...])
        yb = _dot(ob_ref[...].astype(BF16), wob_ref[...])
        ga = p_ref[:, _C_GA:_C_GA + D_MODEL]
        gb = p_ref[:, _C_GB:_C_GB + D_MODEL]
        merged = jax.nn.sigmoid(ga) * ya + jax.nn.sigmoid(gb) * yb
        z = _dot(merged.astype(BF16), wout_ref[...])
        x1 = x_ref[...] + _rms(z, gpost_ref[...])
        x1_s[...] = x1
        xn_s[...] = _rms(x1, gpre2_ref[...]).astype(BF16)
        acc_s[...] = jnp.zeros_like(acc_s)

    xn = xn_s[...]
    u = _dot(xn, wup_ref[...])
    gt = _dot(xn, wgate_ref[...])
    u_ref[...] = u
    pos = lax.broadcasted_iota(jnp.int32, u.shape, 0) % s
    init = init_ref[...]
    u1 = jnp.where(pos == 0, pltpu.roll(init, u.shape[0] - 1, 0), pltpu.roll(u, 1, 0))
    u2 = jnp.where(pos < 2, init, pltpu.roll(u, 2, 0))
    act = _conv_gelu_gate(u, gt, u1, u2, cw_ref[...], cb_ref[...])
    acc_s[...] += _dot(act.astype(BF16), wdown_ref[...])

    @pl.when(c == pl.num_programs(0) - 1)
    def _():
        x2_ref[...] = x1_s[...] + _rms(acc_s[...], gpost2_ref[...])


def _sample_tail(xs, oa, ob, p, w, gpost, gpre2, gpost2, init, s):
    rows = xs.shape[0]
    full = lambda a: pl.BlockSpec(a.shape, lambda c: (0,) * a.ndim)
    cols = lambda r: pl.BlockSpec((r, FF_CHUNK), lambda c: (0, c))
    gate_cols = pl.BlockSpec((D_MODEL, FF_CHUNK), lambda c: (0, N_FF + c))
    down_rows = pl.BlockSpec((FF_CHUNK, D_MODEL), lambda c: (c, 0))
    args = [xs, oa, ob, p, w["oa"], w["ob"], w["out"], gpost, gpre2, gpost2]
    return pl.pallas_call(
        functools.partial(_sample_tail_kernel, s=s),
        grid=(N_FF,),
        in_specs=[full(a) for a in args] + [cols(rows), cols(D_MODEL), gate_cols, cols(3), cols(1), down_rows],
        out_specs=(full(xs), cols(rows)),
        out_shape=(jax.ShapeDtypeStruct((rows, D_MODEL), F32), jax.ShapeDtypeStruct((rows, D_FF), F32)),
        scratch_shapes=[pltpu.VMEM((rows, D_MODEL), F32), pltpu.VMEM((rows, D_MODEL), BF16),
                        pltpu.VMEM((rows, D_MODEL), F32)],
        compiler_params=pltpu.CompilerParams(dimension_semantics=("arbitrary",)),
        name="sample_tail",
    )(*args, init, w["upg"], w["upg"], w["cw"], w["cb"], w["down"])


def _prep_weights(w_in, w_oa, w_ob, w_out, w_upg, conv_w, conv_b, w_down):
    wb = w_in.astype(BF16)
    qv = jnp.concatenate([wb[:, _C_QA:_C_QA + A_QW], wb[:, _C_QB:_C_QB + B_W],
                          wb[:, _C_VA:_C_VA + A_KVW], wb[:, _C_VB:_C_VB + B_W]], axis=1)
    return {
        "in": wb, "qvt": qv.T,
        "oa": w_oa.astype(BF16), "ob": w_ob.astype(BF16), "out": w_out.astype(BF16),
        "upg": w_upg.astype(BF16), "cw": conv_w, "cb": conv_b.reshape(1, D_FF), "down": w_down.astype(BF16),
    }


def _round_up(n, m):
    return (n + m - 1) // m * m


def kernel(x_prompt, x_sample, cache_a_k, cache_a_v, cache_b_k, cache_b_v, state_conv,
           w_in, w_oa, w_ob, w_out, sink_a, t5_table, rel_table_b,
           g_pre_mix, g_post_mix, g_pre_ffn, g_post_ffn, w_upg, conv_w, conv_b, w_down):
    depth = w_in.shape[0]
    bsz, seq, _ = x_prompt.shape
    nb, s, _ = x_sample.shape
    la, lb = cache_a_k.shape[2], cache_b_k.shape[2]
    ka_pad = _round_up(la + s, LANES)
    kb_pad = _round_up(lb + s, LANES)

    xp = x_prompt
    xs = x_sample.reshape(nb * s, D_MODEL)
    prompt_states = [[] for _ in range(5)]
    sample_states = [[] for _ in range(5)]
    for l in range(depth):
        w = _prep_weights(w_in[l], w_oa[l], w_ob[l], w_out[l], w_upg[l], conv_w[l], conv_b[l], w_down[l])
        row = lambda g: g[l].reshape(1, D_MODEL)
        bta, btb, bsa, bsb = _build_bias(t5_table, rel_table_b[l], s, la, lb, ka_pad, kb_pad)
        sink_row = jnp.repeat(sink_a[l], GROUP).reshape(1, A_HEADS * GROUP)

        x1, ka_t, va_t, kb_t, vb_t = _prompt_mixer(xp, row(g_pre_mix), row(g_post_mix), w, sink_row, bta, btb)
        xp, conv_t = _prompt_ffn(x1, row(g_pre_ffn), row(g_post_ffn), w)
        prompt_states[0].append(ka_t.reshape(bsz, A_PREV, A_KV_HEADS, HEAD_DIM))
        prompt_states[1].append(va_t.reshape(bsz, A_PREV, A_KV_HEADS, HEAD_DIM))
        prompt_states[2].append(kb_t.reshape(bsz, B_PREV, B_HEADS, HEAD_DIM))
        prompt_states[3].append(vb_t.reshape(bsz, B_PREV, B_HEADS, HEAD_DIM))
        prompt_states[4].append(conv_t)

        p = _sample_proj(xs, row(g_pre_mix), w["in"])
        oa, ob, ka_n, va_n, kb_n, vb_n = _sample_attn(
            sink_a[l], p, cache_a_k[l].reshape(nb, la, A_KVW), cache_a_v[l].reshape(nb, la, A_KVW),
            cache_b_k[l].reshape(nb, lb, B_W), cache_b_v[l].reshape(nb, lb, B_W), bsa, bsb, s)
        init = jnp.pad(state_conv[l], ((0, 0), (0, s - 2), (0, 0))).reshape(nb * s, D_FF)
        xs, u = _sample_tail(xs, oa, ob, p, w, row(g_post_mix), row(g_pre_ffn), row(g_post_ffn), init, s)
        sample_states[0].append(ka_n.reshape(nb, s, A_KV_HEADS, HEAD_DIM))
        sample_states[1].append(va_n.reshape(nb, s, A_KV_HEADS, HEAD_DIM))
        sample_states[2].append(kb_n.reshape(nb, s, B_HEADS, HEAD_DIM))
        sample_states[3].append(vb_n.reshape(nb, s, B_HEADS, HEAD_DIM))
        sample_states[4].append(u.reshape(nb, s, D_FF)[:, -2:])

    ps = [jnp.stack(a, axis=0) for a in prompt_states]
    ss = [jnp.stack(a, axis=0) for a in sample_states]
    return (xp, xs.reshape(nb, s, D_MODEL), ps[0], ps[1], ps[2], ps[3], ps[4],
            ss[0], ss[1], ss[2], ss[3], ss[4])
```

```python
import functools
import math

import numpy as np
import jax
import jax.numpy as jnp
from jax import lax
from jax.experimental import pallas as pl
from jax.experimental.pallas import tpu as pltpu

F32 = jnp.float32
BF16 = jnp.bfloat16

D_MODEL = 1024
CHUNK = 64
HEAD_DIM = 64
EPS = 1e-6
NEG_INF = -1e30
QK_SCALE = HEAD_DIM ** -0.5

A_HEADS = 8
A_KV_HEADS = 2
A_GROUP = A_HEADS // A_KV_HEADS
A_PREV = 128
T5_BUCKETS = 32
T5_MAX_DIST = 128

B_HEADS = 8
B_PREV = 512
B_REL_CLIP = 128

A_QW = A_HEADS * HEAD_DIM
A_KVW = A_KV_HEADS * HEAD_DIM
B_W = B_HEADS * HEAD_DIM
D_FF = 3072

_C_QA = 0
_C_KA = _C_QA + A_QW
_C_VA = _C_KA + A_KVW
_C_QB = _C_VA + A_KVW
_C_KB = _C_QB + B_W
_C_VB = _C_KB + B_W
_C_GA = _C_VB + B_W
_C_GB = _C_GA + D_MODEL
IN_WIDTH = _C_GB + D_MODEL

LANES = 128
GROUP = 2 * CHUNK
A_WIN = A_PREV + GROUP
B_WIN = B_PREV + GROUP
SEQ_TILE = 512
FF_CHUNK = 512
N_FF = D_FF // FF_CHUNK
VMEM_LIMIT = 56 * 1024 * 1024
LOOKAHEAD = 1
GATE_FIRST_STEP = 1
GATE_STEP_STRIDE = 3


def _rms(x, g):
    y = x * lax.rsqrt(jnp.mean(x * x, axis=-1, keepdims=True) + EPS)
    return y * g


def _dot(a, b):
    return jnp.dot(a, b, preferred_element_type=F32)


def _dot_nt(a, b):
    return lax.dot_general(a, b, (((1,), (1,)), ((), ())), preferred_element_type=F32)


def _t5_bucket(n):
    half = T5_BUCKETS // 2
    max_exact = half // 2
    ret = jnp.where(n < 0, half, 0)
    a = jnp.abs(n)
    af = jnp.maximum(a, 1).astype(jnp.float32)
    large = max_exact + (jnp.log(af / max_exact) / math.log(T5_MAX_DIST / max_exact)
                         * (half - max_exact)).astype(jnp.int32)
    large = jnp.minimum(large, half - 1)
    return ret + jnp.where(a < max_exact, a, large)


def _window_valid(n_prev, win):
    j = np.arange(win)[:, None]
    ci = np.arange(GROUP)[None, :] // CHUNK
    return ((j >= CHUNK * ci) & (j < CHUNK * ci + n_prev + CHUNK)).astype(np.int32)


def _lookup(idx, table, lo, hi):
    acc = jnp.zeros((table.shape[0], idx.shape[1]), F32)
    for b in range(lo, hi):
        acc = jnp.where(idx == b, table[:, b:b + 1], acc)
    return acc


def _toeplitz(vec, rows, first):
    w = vec.shape[1]
    return pltpu.roll(jnp.broadcast_to(vec, (rows, w)), (w - first) % w, 1, stride=1, stride_axis=0)


def _bias_kernel(t5_ref, rel_ref, ja_ref, jb_ref, jsa_ref, jsb_ref, va_ref, vb_ref, vsa_ref, vsb_ref,
                 bta_ref, btb_ref, bsa_ref, bsb_ref, *, s, jb_range, jsb_range):
    t5 = t5_ref[...]
    rel = rel_ref[...]

    fa = _lookup(ja_ref[...], t5, 0, T5_BUCKETS)
    va = va_ref[...] != 0
    for h in range(A_HEADS):
        t = _toeplitz(fa[h:h + 1, :], A_WIN, A_WIN - 1)[:, 0:GROUP]
        bta_ref[:, GROUP * h:GROUP * (h + 1)] = jnp.where(va, t, NEG_INF)

    fb = _lookup(jb_ref[...], rel, *jb_range)
    for r in range(B_WIN // LANES):
        base = B_WIN - LANES * r
        vb = vb_ref[LANES * r:LANES * (r + 1), :] != 0
        for h in range(B_HEADS):
            seg = fb[h:h + 1, base - LANES:base + LANES]
            t = _toeplitz(seg, LANES, LANES)[:, 0:GROUP]
            btb_ref[h, LANES * r:LANES * (r + 1), :] = jnp.where(vb, t, NEG_INF)

    fsa = _lookup(jsa_ref[...], t5, 0, T5_BUCKETS)
    fsb = _lookup(jsb_ref[...], rel, *jsb_range)
    vsa = vsa_ref[...] != 0
    vsb = vsb_ref[...] != 0
    for h in range(A_HEADS):
        bsa_ref[h] = jnp.where(vsa, _toeplitz(fsa[h:h + 1, :], s, s - 1), NEG_INF)
    for h in range(B_HEADS):
        bsb_ref[h] = jnp.where(vsb, _toeplitz(fsb[h:h + 1, :], s, s - 1), NEG_INF)


def _build_bias(t5_table, rel_table, s, la, lb, ka_pad, kb_pad):
    wa = 3 * LANES
    assert A_WIN - 1 + GROUP <= wa and la + 2 * s - 2 < ka_pad and lb + 2 * s - 2 < kb_pad
    ja = _t5_bucket(jnp.arange(wa) - (A_WIN - 1) + A_PREV).astype(jnp.int32).reshape(1, wa)
    jb = np.clip(np.arange(B_WIN + GROUP) - B_WIN + B_PREV, -B_REL_CLIP, B_REL_CLIP) + B_REL_CLIP
    jsa = _t5_bucket(la + s - 1 - jnp.arange(ka_pad)).astype(jnp.int32).reshape(1, ka_pad)
    jsb = np.clip(lb + s - 1 - np.arange(kb_pad), -B_REL_CLIP, B_REL_CLIP) + B_REL_CLIP
    key = lambda n: np.broadcast_to(np.arange(n)[None, :], (s, n))
    vsa = (key(ka_pad) < la + s).astype(np.int32)
    vsb = (key(kb_pad) < lb + s).astype(np.int32)
    as_row = lambda v: jnp.asarray(v.astype(np.int32).reshape(1, -1))
    rng = lambda v: (int(v.min()), int(v.max()) + 1)

    vmem = pl.BlockSpec(memory_space=pltpu.VMEM)
    return pl.pallas_call(
        functools.partial(_bias_kernel, s=s, jb_range=rng(jb), jsb_range=rng(jsb)),
        out_shape=(
            jax.ShapeDtypeStruct((A_WIN, A_HEADS * GROUP), F32),
            jax.ShapeDtypeStruct((B_HEADS, B_WIN, GROUP), F32),
            jax.ShapeDtypeStruct((A_HEADS, s, ka_pad), F32),
            jax.ShapeDtypeStruct((B_HEADS, s, kb_pad), F32),
        ),
        in_specs=[vmem] * 10,
        out_specs=(vmem,) * 4,
        name="bias_tables",
    )(t5_table, rel_table, ja, as_row(jb), jsa, as_row(jsb),
      jnp.asarray(_window_valid(A_PREV, A_WIN)), jnp.asarray(_window_valid(B_PREV, B_WIN)),
      jnp.asarray(vsa), jnp.asarray(vsb))


def _ordering_zero(x, never):
    return jnp.where(never, x, 0.0)


def _add_to_corner(x, z, cols=(0,)):
    pieces, at = [], 0
    for c in cols:
        pieces += [x[0:16, at:c], x[0:16, c:c + LANES] + z]
        at = c + LANES
    pieces.append(x[0:16, at:])
    head = jnp.concatenate([q for q in pieces if q.shape[1]], axis=1)
    return jnp.concatenate([head, x[16:, :]], axis=0)


def _scores_a(g, ka_s, qt_s):
    tok = slice(GROUP * g, GROUP * (g + 1))
    kwin = ka_s[GROUP * g:GROUP * g + A_WIN, :]
    q = qt_s[0:A_QW, tok]
    zero = jnp.zeros((HEAD_DIM, GROUP), BF16)
    cols = []
    for i in range(A_HEADS):
        qi = q[HEAD_DIM * i:HEAD_DIM * (i + 1), :]
        cols.append(jnp.concatenate([qi, zero] if i < A_GROUP else [zero, qi], axis=0))
    return _dot(kwin, jnp.concatenate(cols, axis=1))


def _scores_b(g, pp, kb_s, qt_s):
    tok = slice(GROUP * g, GROUP * (g + 1))
    kwin = kb_s[GROUP * g:GROUP * g + B_WIN, 2 * LANES * pp:2 * LANES * (pp + 1)]
    qrow = A_QW + 2 * LANES * pp
    zero = jnp.zeros((HEAD_DIM, GROUP), BF16)
    cols = []
    for i in range(4):
        qi = qt_s[qrow + HEAD_DIM * i:qrow + HEAD_DIM * (i + 1), tok]
        cols.append(jnp.concatenate([zero] * i + [qi] + [zero] * (3 - i), axis=0))
    return _dot(kwin, jnp.concatenate(cols, axis=1))


def _softmax_keys(st, bias, sink_row, order_after, order_cols=(0,)):
    st = st + bias
    m = jnp.max(st, axis=0, keepdims=True)
    if sink_row is not None:
        m = jnp.maximum(m, sink_row)
    p = jnp.exp(st - m)
    if order_after is not None:
        p = _add_to_corner(p, order_after, order_cols)
    return p.astype(BF16), m


def _weighted_values(vwin, present, p, m, sink_row):
    ot = _dot(jnp.concatenate([vwin, present], axis=0), p)
    l = ot[2 * HEAD_DIM:2 * HEAD_DIM + 1, :]
    if sink_row is not None:
        l = l + jnp.exp(sink_row - m)
    return ot[0:2 * HEAD_DIM, :] * (1.0 / l)


def _values_a(g, vat_s, present_s, p, m, sink_row, oa_s):
    tok = slice(GROUP * g, GROUP * (g + 1))
    c0 = B_PREV - A_PREV + GROUP * g
    ot = _weighted_values(vat_s[:, GROUP * g:GROUP * g + A_WIN], present_s[:, c0:c0 + A_WIN], p, m, sink_row)
    for j in range(A_HEADS // 2):
        parts = []
        for i in (2 * j, 2 * j + 1):
            rows = slice(0, HEAD_DIM) if i < A_GROUP else slice(HEAD_DIM, 2 * HEAD_DIM)
            parts.append(ot[rows, GROUP * i:GROUP * (i + 1)])
        blk = jnp.concatenate(parts, axis=0)
        oa_s[tok, LANES * j:LANES * (j + 1)] = blk.T.astype(BF16)


def _values_b(g, p2, vbt_s, present_s, p, m, ob_s):
    tok = slice(GROUP * g, GROUP * (g + 1))
    keys = slice(GROUP * g, GROUP * g + B_WIN)
    lanes = slice(LANES * p2, LANES * (p2 + 1))
    ot = _weighted_values(vbt_s[lanes, keys], present_s[:, keys], p, m, None)
    blk = jnp.concatenate([ot[0:HEAD_DIM, 0:GROUP], ot[HEAD_DIM:2 * HEAD_DIM, GROUP:2 * GROUP]], axis=0)
    ob_s[tok, lanes] = blk.T.astype(BF16)


def _attend_tile(n_groups, never, ka_s, vat_s, kb_s, vbt_s, present_s, qt_s, bta_ref, btb_ref, sink_row, oa_s, ob_s,
                 fillers):
    steps = []
    for g in range(n_groups):
        steps.append(("a", g, 0))
        steps.extend(("b", g, pp) for pp in range(B_HEADS // 4))

    def scores(step):
        kind, g, pp = step
        return _scores_a(g, ka_s, qt_s) if kind == "a" else _scores_b(g, pp, kb_s, qt_s)

    ahead = [scores(step) for step in steps[:LOOKAHEAD]]
    for k, (kind, g, pp) in enumerate(steps):
        st, order_after = ahead.pop(0), None
        if k + LOOKAHEAD < len(steps):
            ahead.append(scores(steps[k + LOOKAHEAD]))
            order_after = _ordering_zero(ahead[-1][0:16, 0:LANES], never)
        if fillers.get(k) is not None:
            z = _ordering_zero(fillers[k](), never)
            order_after = z if order_after is None else order_after + z
        if kind == "a":
            p, m = _softmax_keys(st, bta_ref[...], sink_row, order_after)
            _values_a(g, vat_s, present_s, p, m, sink_row, oa_s)
        else:
            bias = jnp.concatenate([btb_ref[4 * pp + i] for i in range(4)], axis=1)
            p, m = _softmax_keys(st, bias, None, order_after, (0, 2 * GROUP))
            for j in range(2):
                cols = slice(2 * GROUP * j, 2 * GROUP * (j + 1))
                _values_b(g, 2 * pp + j, vbt_s, present_s, p[:, cols], m[:, cols], ob_s)


def _mixer_kernel(x_ref, gpre_ref, gpost_ref, win_ref, wt_ref,
                  woa_ref, wob_ref, wout_ref, sink_ref, bta_ref, btb_ref, never_ref,
                  x1_ref, kat_ref, vat_ref, kbt_ref, vbt_ref,
                  h_s, qt_s, ka_s, vat_s, kb_s, vbt_s, present_s, oa_s, ob_s, mg_s, sg_s,
                  ka_f, va_f, kb_f, vb_f, *, tile, n_tiles):
    i = pl.program_id(1)

    @pl.when(i == 0)
    def _():
        ka_s[0:A_PREV, :] = jnp.zeros((A_PREV, A_KVW), BF16)
        vat_s[:, 0:A_PREV] = jnp.zeros((A_KVW, A_PREV), BF16)
        kb_s[0:B_PREV, :] = jnp.zeros((B_PREV, B_W), BF16)
        vbt_s[:, 0:B_PREV] = jnp.zeros((B_W, B_PREV), BF16)
        present_s[:, 0:B_PREV] = jnp.zeros((16, B_PREV), BF16)

    @pl.when(i > 0)
    def _():
        ka_s[0:A_PREV, :] = ka_s[tile:tile + A_PREV, :]
        vat_s[:, 0:A_PREV] = vat_s[:, tile:tile + A_PREV]
        kb_s[0:B_PREV, :] = kb_s[tile:tile + B_PREV, :]
        vbt_s[:, 0:B_PREV] = vbt_s[:, tile:tile + B_PREV]
        present_s[:, 0:B_PREV] = present_s[:, tile:tile + B_PREV]

    x = x_ref[...]
    h = _rms(x, gpre_ref[...]).astype(BF16)
    h_s[...] = h

    ka = _dot(h, win_ref[:, _C_KA:_C_KA + A_KVW])
    kb = _dot(h, win_ref[:, _C_KB:_C_KB + B_W])
    ka_s[A_PREV:A_PREV + tile, :] = ka.astype(BF16)
    kb_s[B_PREV:B_PREV + tile, :] = kb.astype(BF16)
    nq = A_QW + B_W
    qt_s[...] = (_dot_nt(wt_ref[0:nq, :], h) * QK_SCALE).astype(BF16)
    vat = _dot_nt(wt_ref[nq:nq + A_KVW, :], h)
    vbt = _dot_nt(wt_ref[nq + A_KVW:nq + A_KVW + B_W, :], h)
    vat_s[:, A_PREV:A_PREV + tile] = vat.astype(BF16)
    vbt_s[:, B_PREV:B_PREV + tile] = vbt.astype(BF16)
    first_row = lax.broadcasted_iota(jnp.int32, (16, tile), 0) == 0
    present_s[:, B_PREV:B_PREV + tile] = jnp.where(first_row, 1.0, 0.0).astype(BF16)
    ka_f[...] = ka[tile - A_PREV:, :]
    kb_f[...] = kb[tile - B_PREV:, :]
    va_f[...] = vat[:, tile - A_PREV:]
    vb_f[...] = vbt[:, tile - B_PREV:]

    gate_w = 512
    n_steps = (tile // GROUP) * (1 + B_HEADS // 4)

    def gate_piece(j):
        def run():
            g = _dot(h_s[...], win_ref[:, _C_GA + gate_w * j:_C_GA + gate_w * (j + 1)])
            sg_s[:, gate_w * j:gate_w * (j + 1)] = jax.nn.sigmoid(g)
            return g[0:16, 0:LANES]
        return run

    n_pieces = 2 * D_MODEL // gate_w
    fillers = {GATE_FIRST_STEP + GATE_STEP_STRIDE * j: gate_piece(j) for j in range(n_pieces)}
    assert max(fillers) < n_steps
    _attend_tile(tile // GROUP, never_ref[...] != 0, ka_s, vat_s, kb_s, vbt_s, present_s, qt_s,
                 bta_ref, btb_ref, sink_ref[...], oa_s, ob_s, fillers)

    oa = oa_s[...]
    ob = ob_s[...]
    cw = 512
    for c in range(D_MODEL // cw):
        cs = slice(cw * c, cw * (c + 1))
        ya = _dot(oa, woa_ref[:, cs])
        yb = _dot(ob, wob_ref[:, cs])
        sgb = sg_s[:, D_MODEL + cw * c:D_MODEL + cw * (c + 1)]
        mg_s[:, cs] = (sg_s[:, cs] * ya + sgb * yb).astype(BF16)
    z = _dot(mg_s[...], wout_ref[...])
    x1_ref[...] = x_ref[...] + _rms(z, gpost_ref[...])

    @pl.when(i == n_tiles - 1)
    def _():
        kat_ref[...] = ka_f[...]
        kbt_ref[...] = kb_f[...]
        vat_ref[...] = va_f[...].T
        vbt_ref[...] = vb_f[...].T


def _const_spec(shape):
    nd = len(shape)
    return pl.BlockSpec(shape, lambda *_: (0,) * nd, pipeline_mode=pl.Buffered(1))


def _prompt_mixer(x, gpre, gpost, w, sink_row, bta, btb):
    bsz, seq, _ = x.shape
    tile = SEQ_TILE
    assert seq % tile == 0 and tile % GROUP == 0 and tile >= B_PREV
    n_tiles = seq // tile
    row_spec = pl.BlockSpec((None, tile, D_MODEL), lambda b, i: (b, i, 0))

    def tail_spec(rows, width):
        return pl.BlockSpec((None, rows, width), lambda b, i: (b, 0, 0))

    consts = [gpre, gpost, w["in"], w["qvt"],
              w["oa"], w["ob"], w["out"], sink_row, bta, btb, jnp.zeros((16, LANES), jnp.int32)]
    return pl.pallas_call(
        functools.partial(_mixer_kernel, tile=tile, n_tiles=n_tiles),
        grid=(bsz, n_tiles),
        in_specs=[row_spec] + [_const_spec(c.shape) for c in consts],
        out_specs=(row_spec, tail_spec(A_PREV, A_KVW), tail_spec(A_PREV, A_KVW),
                   tail_spec(B_PREV, B_W), tail_spec(B_PREV, B_W)),
        out_shape=(
            jax.ShapeDtypeStruct((bsz, seq, D_MODEL), F32),
            jax.ShapeDtypeStruct((bsz, A_PREV, A_KVW), F32),
            jax.ShapeDtypeStruct((bsz, A_PREV, A_KVW), F32),
            jax.ShapeDtypeStruct((bsz, B_PREV, B_W), F32),
            jax.ShapeDtypeStruct((bsz, B_PREV, B_W), F32),
        ),
        scratch_shapes=[
            pltpu.VMEM((tile, D_MODEL), BF16),
            pltpu.VMEM((A_QW + B_W, tile), BF16),
            pltpu.VMEM((A_PREV + tile, A_KVW), BF16),
            pltpu.VMEM((A_KVW, A_PREV + tile), BF16),
            pltpu.VMEM((B_PREV + tile, B_W), BF16),
            pltpu.VMEM((B_W, B_PREV + tile), BF16),
            pltpu.VMEM((16, B_PREV + tile), BF16),
            pltpu.VMEM((tile, A_QW), BF16),
            pltpu.VMEM((tile, B_W), BF16),
            pltpu.VMEM((tile, D_MODEL), BF16),
            pltpu.VMEM((tile, 2 * D_MODEL), F32),
            pltpu.VMEM((A_PREV, A_KVW), F32),
            pltpu.VMEM((A_KVW, A_PREV), F32),
            pltpu.VMEM((B_PREV, B_W), F32),
            pltpu.VMEM((B_W, B_PREV), F32),
        ],
        compiler_params=pltpu.CompilerParams(
            dimension_semantics=("arbitrary", "arbitrary"), vmem_limit_bytes=VMEM_LIMIT),
        name="prompt_mixer",
    )(x, *consts)


def _conv_gelu_gate(u, gt, u1, u2, cw, cb):
    c = cb + cw[0:1, :] * u2
    c = c + cw[1:2, :] * u1
    c = c + cw[2:3, :] * u
    return jax.nn.gelu(c, approximate=True) * gt


def _ffn_kernel(x_ref, gpre_ref, gpost_ref, wupg_ref, cw_ref, cb_ref, wdown_ref, never_ref,
                x2_ref, tail_ref, carry_s, next_s, *, tile):
    i = pl.program_id(1)

    @pl.when(i == 0)
    def _():
        carry_s[...] = jnp.zeros_like(carry_s)

    @pl.when(i > 0)
    def _():
        carry_s[...] = next_s[...]

    x1 = x_ref[...]
    xn = _rms(x1, gpre_ref[...]).astype(BF16)
    never = never_ref[...] != 0
    row = lax.broadcasted_iota(jnp.int32, (tile, FF_CHUNK), 0)
    acc = jnp.zeros((tile, D_MODEL), F32)

    def up_gate(c):
        cols = slice(FF_CHUNK * c, FF_CHUNK * (c + 1))
        gcols = slice(D_FF + FF_CHUNK * c, D_FF + FF_CHUNK * (c + 1))
        return _dot(xn, wupg_ref[:, cols]), _dot(xn, wupg_ref[:, gcols])

    up_next = up_gate(0)
    for c in range(N_FF):
        u, gt = up_next
        cols = slice(FF_CHUNK * c, FF_CHUNK * (c + 1))
        prev = carry_s[c]
        u1 = jnp.where(row == 0, prev[7:8, :], pltpu.roll(u, 1, 0))
        u2 = jnp.where(row == 0, prev[6:7, :], jnp.where(row == 1, prev[7:8, :], pltpu.roll(u, 2, 0)))
        next_s[c] = u[tile - 8:tile, :]
        tail_ref[:, cols] = u[tile - 2:tile, :]
        act = _conv_gelu_gate(u, gt, u1, u2, cw_ref[:, cols], cb_ref[:, cols])
        if c + 1 < N_FF:
            up_next = up_gate(c + 1)
            act = _add_to_corner(act, _ordering_zero(up_next[0][0:16, 0:LANES], never))
        acc = acc + _dot(act.astype(BF16), wdown_ref[cols, :])
    x2_ref[...] = x1 + _rms(acc, gpost_ref[...])


def _prompt_ffn(x1, gpre, gpost, w):
    bsz, seq, _ = x1.shape
    tile = SEQ_TILE
    assert seq % tile == 0
    row_spec = pl.BlockSpec((None, tile, D_MODEL), lambda b, i: (b, i, 0))
    consts = [gpre, gpost, w["upg"], w["cw"], w["cb"], w["down"], jnp.zeros((16, LANES), jnp.int32)]
    return pl.pallas_call(
        functools.partial(_ffn_kernel, tile=tile),
        grid=(bsz, seq // tile),
        in_specs=[row_spec] + [_const_spec(c.shape) for c in consts],
        out_specs=(row_spec, pl.BlockSpec((None, 2, D_FF), lambda b, i: (b, 0, 0))),
        out_shape=(jax.ShapeDtypeStruct((bsz, seq, D_MODEL), F32),
                   jax.ShapeDtypeStruct((bsz, 2, D_FF), F32)),
        scratch_shapes=[pltpu.VMEM((N_FF, 8, FF_CHUNK), F32), pltpu.VMEM((N_FF, 8, FF_CHUNK), F32)],
        compiler_params=pltpu.CompilerParams(
            dimension_semantics=("arbitrary", "arbitrary"), vmem_limit_bytes=VMEM_LIMIT),
        name="prompt_ffn",
    )(x1, *consts)


def _sample_proj_kernel(x_ref, g_ref, w_ref, p_ref):
    h = _rms(x_ref[...], g_ref[...]).astype(BF16)
    p_ref[...] = _dot(h, w_ref[...])


def _sample_proj(xs, gpre, w_in_b):
    rows = xs.shape[0]
    nblk = 2
    width = IN_WIDTH // nblk
    assert width % LANES == 0
    return pl.pallas_call(
        _sample_proj_kernel,
        grid=(nblk,),
        in_specs=[pl.BlockSpec((rows, D_MODEL), lambda j: (0, 0)),
                  pl.BlockSpec((1, D_MODEL), lambda j: (0, 0)),
                  pl.BlockSpec((D_MODEL, width), lambda j: (0, j))],
        out_specs=pl.BlockSpec((rows, width), lambda j: (0, j)),
        out_shape=jax.ShapeDtypeStruct((rows, IN_WIDTH), F32),
        compiler_params=pltpu.CompilerParams(dimension_semantics=("arbitrary",)),
        name="sample_proj",
    )(xs, gpre, w_in_b)


def _sample_attn_kernel(sink_ref, p_ref, cak_ref, cav_ref, cbk_ref, cbv_ref, bsa_ref, bsb_ref,
                        oa_ref, ob_ref, kan_ref, van_ref, kbn_ref, vbn_ref, ka_s, va_s, kb_s, vb_s, *, s, la, lb):
    p = p_ref[...]
    kan_ref[...] = p[:, _C_KA:_C_KA + A_KVW]
    van_ref[...] = p[:, _C_VA:_C_VA + A_KVW]
    kbn_ref[...] = p[:, _C_KB:_C_KB + B_W]
    vbn_ref[...] = p[:, _C_VB:_C_VB + B_W]
    ka_pad = ka_s.shape[0]
    kb_pad = kb_s.shape[0]

    def fill(dst, cache_ref, new, n_cache, n_pad):
        dst[0:n_cache, :] = cache_ref[...].astype(BF16)
        dst[n_cache:n_cache + s, :] = new.astype(BF16)
        dst[n_cache + s:n_pad, :] = jnp.zeros((n_pad - n_cache - s, dst.shape[1]), BF16)

    fill(ka_s, cak_ref, p[:, _C_KA:_C_KA + A_KVW], la, ka_pad)
    fill(va_s, cav_ref, p[:, _C_VA:_C_VA + A_KVW], la, ka_pad)
    fill(kb_s, cbk_ref, p[:, _C_KB:_C_KB + B_W], lb, kb_pad)
    fill(vb_s, cbv_ref, p[:, _C_VB:_C_VB + B_W], lb, kb_pad)

    lo = lax.broadcasted_iota(jnp.int32, (s, LANES), 1) < HEAD_DIM
    top = lax.broadcasted_iota(jnp.int32, (2 * s, 1), 0) < s

    def softmax_pv(qs, k, v, bias, sink):
        st = _dot_nt(qs.astype(BF16), k) + bias
        m = jnp.max(st, axis=-1, keepdims=True)
        if sink is not None:
            m = jnp.maximum(m, sink)
        e = jnp.exp(st - m)
        l = jnp.sum(e, axis=-1, keepdims=True)
        if sink is not None:
            l = l + jnp.exp(sink - m)
        return _dot(e.astype(BF16), v) * (1.0 / l)

    ka = ka_s[...]
    va = va_s[...]
    for j in range(A_HEADS // 2):
        q = p[:, _C_QA + LANES * j:_C_QA + LANES * (j + 1)] * QK_SCALE
        qr = pltpu.roll(q, HEAD_DIM, 1)
        if 2 * j < A_GROUP:
            qs = jnp.concatenate([jnp.where(lo, q, 0.0), jnp.where(lo, qr, 0.0)], axis=0)
        else:
            qs = jnp.concatenate([jnp.where(lo, 0.0, qr), jnp.where(lo, 0.0, q)], axis=0)
        bias = jnp.concatenate([bsa_ref[2 * j], bsa_ref[2 * j + 1]], axis=0)
        sink = jnp.where(top, sink_ref[2 * j], sink_ref[2 * j + 1])
        o = softmax_pv(qs, ka, va, bias, sink)
        oe, oo = o[0:s], o[s:2 * s]
        if 2 * j < A_GROUP:
            out = jnp.where(lo, oe, pltpu.roll(oo, HEAD_DIM, 1))
        else:
            out = jnp.where(lo, pltpu.roll(oe, HEAD_DIM, 1), oo)
        oa_ref[:, LANES * j:LANES * (j + 1)] = out

    for j in range(B_HEADS // 2):
        lanes = slice(LANES * j, LANES * (j + 1))
        q = p[:, _C_QB + LANES * j:_C_QB + LANES * (j + 1)] * QK_SCALE
        qs = jnp.concatenate([jnp.where(lo, q, 0.0), jnp.where(lo, 0.0, q)], axis=0)
        bias = jnp.concatenate([bsb_ref[2 * j], bsb_ref[2 * j + 1]], axis=0)
        o = softmax_pv(qs, kb_s[:, lanes], vb_s[:, lanes], bias, None)
        ob_ref[:, lanes] = jnp.where(lo, o[0:s], o[s:2 * s])


def _sample_attn(sink, p, cak, cav, cbk, cbv, bsa, bsb, s):
    nb, la, _ = cak.shape
    lb = cbk.shape[1]
    ka_pad, kb_pad = bsa.shape[-1], bsb.shape[-1]
    assert s % 16 == 0 and la % 16 == 0 and lb % 16 == 0
    row = lambda w: pl.BlockSpec((s, w), lambda b: (b, 0))
    cache = lambda n, w: pl.BlockSpec((None, n, w), lambda b: (b, 0, 0))
    full = lambda a: pl.BlockSpec(a.shape, lambda b: (0,) * a.ndim)
    widths = (A_QW, B_W, A_KVW, A_KVW, B_W, B_W)
    return pl.pallas_call(
        functools.partial(_sample_attn_kernel, s=s, la=la, lb=lb),
        grid=(nb,),
        in_specs=[pl.BlockSpec(memory_space=pltpu.SMEM), row(IN_WIDTH),
                  cache(la, A_KVW), cache(la, A_KVW), cache(lb, B_W), cache(lb, B_W), full(bsa), full(bsb)],
        out_specs=tuple(row(w) for w in widths),
        out_shape=tuple(jax.ShapeDtypeStruct((nb * s, w), F32) for w in widths),
        scratch_shapes=[pltpu.VMEM((ka_pad, A_KVW), BF16), pltpu.VMEM((ka_pad, A_KVW), BF16),
                        pltpu.VMEM((kb_pad, B_W), BF16), pltpu.VMEM((kb_pad, B_W), BF16)],
        compiler_params=pltpu.CompilerParams(dimension_semantics=("arbitrary",)),
        name="sample_attn",
    )(sink, p, cak, cav, cbk, cbv, bsa, bsb)


def _sample_tail_kernel(x_ref, oa_ref, ob_ref, p_ref, woa_ref, wob_ref, wout_ref,
                        gpost_ref, gpre2_ref, gpost2_ref, init_ref,
                        wup_ref, wgate_ref, cw_ref, cb_ref, wdown_ref,
                        x2_ref, u_ref, x1_s, xn_s, acc_s, *, s):
    c = pl.program_id(0)

    @pl.when(c == 0)
    def _():
        ya = _dot(oa_ref[...].astype(BF16), woa_ref[...])
        yb = _dot(ob_ref[...].astype(BF16), wob_ref[...])
        ga = p_ref[:, _C_GA:_C_GA + D_MODEL]
        gb = p_ref[:, _C_GB:_C_GB + D_MODEL]
        merged = jax.nn.sigmoid(ga) * ya + jax.nn.sigmoid(gb) * yb
        z = _dot(merged.astype(BF16), wout_ref[...])
        x1 = x_ref[...] + _rms(z, gpost_ref[...])
        x1_s[...] = x1
        xn_s[...] = _rms(x1, gpre2_ref[...]).astype(BF16)
        acc_s[...] = jnp.zeros_like(acc_s)

    xn = xn_s[...]
    u = _dot(xn, wup_ref[...])
    gt = _dot(xn, wgate_ref[...])
    u_ref[...] = u
    pos = lax.broadcasted_iota(jnp.int32, u.shape, 0) % s
    init = init_ref[...]
    u1 = jnp.where(pos == 0, pltpu.roll(init, u.shape[0] - 1, 0), pltpu.roll(u, 1, 0))
    u2 = jnp.where(pos < 2, init, pltpu.roll(u, 2, 0))
    act = _conv_gelu_gate(u, gt, u1, u2, cw_ref[...], cb_ref[...])
    acc_s[...] += _dot(act.astype(BF16), wdown_ref[...])

    @pl.when(c == pl.num_programs(0) - 1)
    def _():
        x2_ref[...] = x1_s[...] + _rms(acc_s[...], gpost2_ref[...])


def _sample_tail(xs, oa, ob, p, w, gpost, gpre2, gpost2, init, s):
    rows = xs.shape[0]
    full = lambda a: pl.BlockSpec(a.shape, lambda c: (0,) * a.ndim)
    cols = lambda r: pl.BlockSpec((r, FF_CHUNK), lambda c: (0, c))
    gate_cols = pl.BlockSpec((D_MODEL, FF_CHUNK), lambda c: (0, N_FF + c))
    down_rows = pl.BlockSpec((FF_CHUNK, D_MODEL), lambda c: (c, 0))
    args = [xs, oa, ob, p, w["oa"], w["ob"], w["out"], gpost, gpre2, gpost2]
    return pl.pallas_call(
        functools.partial(_sample_tail_kernel, s=s),
        grid=(N_FF,),
        in_specs=[full(a) for a in args] + [cols(rows), cols(D_MODEL), gate_cols, cols(3), cols(1), down_rows],
        out_specs=(full(xs), cols(rows)),
        out_shape=(jax.ShapeDtypeStruct((rows, D_MODEL), F32), jax.ShapeDtypeStruct((rows, D_FF), F32)),
        scratch_shapes=[pltpu.VMEM((rows, D_MODEL), F32), pltpu.VMEM((rows, D_MODEL), BF16),
                        pltpu.VMEM((rows, D_MODEL), F32)],
        compiler_params=pltpu.CompilerParams(dimension_semantics=("arbitrary",)),
        name="sample_tail",
    )(*args, init, w["upg"], w["upg"], w["cw"], w["cb"], w["down"])


def _prep_weights(w_in, w_oa, w_ob, w_out, w_upg, conv_w, conv_b, w_down):
    wb = w_in.astype(BF16)
    qv = jnp.concatenate([wb[:, _C_QA:_C_QA + A_QW], wb[:, _C_QB:_C_QB + B_W],
                          wb[:, _C_VA:_C_VA + A_KVW], wb[:, _C_VB:_C_VB + B_W]], axis=1)
    return {
        "in": wb, "qvt": qv.T,
        "oa": w_oa.astype(BF16), "ob": w_ob.astype(BF16), "out": w_out.astype(BF16),
        "upg": w_upg.astype(BF16), "cw": conv_w, "cb": conv_b.reshape(1, D_FF), "down": w_down.astype(BF16),
    }


def _round_up(n, m):
    return (n + m - 1) // m * m


def kernel(x_prompt, x_sample, cache_a_k, cache_a_v, cache_b_k, cache_b_v, state_conv,
           w_in, w_oa, w_ob, w_out, sink_a, t5_table, rel_table_b,
           g_pre_mix, g_post_mix, g_pre_ffn, g_post_ffn, w_upg, conv_w, conv_b, w_down):
    depth = w_in.shape[0]
    bsz, seq, _ = x_prompt.shape
    nb, s, _ = x_sample.shape
    la, lb = cache_a_k.shape[2], cache_b_k.shape[2]
    ka_pad = _round_up(la + s, LANES)
    kb_pad = _round_up(lb + s, LANES)

    xp = x_prompt
    xs = x_sample.reshape(nb * s, D_MODEL)
    prompt_states = [[] for _ in range(5)]
    sample_states = [[] for _ in range(5)]
    for l in range(depth):
        w = _prep_weights(w_in[l], w_oa[l], w_ob[l], w_out[l], w_upg[l], conv_w[l], conv_b[l], w_down[l])
        row = lambda g: g[l].reshape(1, D_MODEL)
        bta, btb, bsa, bsb = _build_bias(t5_table, rel_table_b[l], s, la, lb, ka_pad, kb_pad)
        sink_row = jnp.repeat(sink_a[l], GROUP).reshape(1, A_HEADS * GROUP)

        x1, ka_t, va_t, kb_t, vb_t = _prompt_mixer(xp, row(g_pre_mix), row(g_post_mix), w, sink_row, bta, btb)
        xp, conv_t = _prompt_ffn(x1, row(g_pre_ffn), row(g_post_ffn), w)
        prompt_states[0].append(ka_t.reshape(bsz, A_PREV, A_KV_HEADS, HEAD_DIM))
        prompt_states[1].append(va_t.reshape(bsz, A_PREV, A_KV_HEADS, HEAD_DIM))
        prompt_states[2].append(kb_t.reshape(bsz, B_PREV, B_HEADS, HEAD_DIM))
        prompt_states[3].append(vb_t.reshape(bsz, B_PREV, B_HEADS, HEAD_DIM))
        prompt_states[4].append(conv_t)

        p = _sample_proj(xs, row(g_pre_mix), w["in"])
        oa, ob, ka_n, va_n, kb_n, vb_n = _sample_attn(
            sink_a[l], p, cache_a_k[l].reshape(nb, la, A_KVW), cache_a_v[l].reshape(nb, la, A_KVW),
            cache_b_k[l].reshape(nb, lb, B_W), cache_b_v[l].reshape(nb, lb, B_W), bsa, bsb, s)
        init = jnp.pad(state_conv[l], ((0, 0), (0, s - 2), (0, 0))).reshape(nb * s, D_FF)
        xs, u = _sample_tail(xs, oa, ob, p, w, row(g_post_mix), row(g_pre_ffn), row(g_post_ffn), init, s)
        sample_states[0].append(ka_n.reshape(nb, s, A_KV_HEADS, HEAD_DIM))
        sample_states[1].append(va_n.reshape(nb, s, A_KV_HEADS, HEAD_DIM))
        sample_states[2].append(kb_n.reshape(nb, s, B_HEADS, HEAD_DIM))
        sample_states[3].append(vb_n.reshape(nb, s, B_HEADS, HEAD_DIM))
        sample_states[4].append(u.reshape(nb, s, D_FF)[:, -2:])

    ps = [jnp.stack(a, axis=0) for a in prompt_states]
    ss = [jnp.stack(a, axis=0) for a in sample_states]
    return (xp, xs.reshape(nb, s, D_MODEL), ps[0], ps[1], ps[2], ps[3], ps[4],
            ss[0], ss[1], ss[2], ss[3], ss[4])
```

```python
import functools
import math

import numpy as np
import jax
import jax.numpy as jnp
from jax import lax
from jax.experimental import pallas as pl
from jax.experimental.pallas import tpu as pltpu

F32 = jnp.float32
BF16 = jnp.bfloat16

D_MODEL = 1024
CHUNK = 64
HEAD_DIM = 64
EPS = 1e-6
NEG_INF = -1e30
QK_SCALE = HEAD_DIM ** -0.5

A_HEADS = 8
A_KV_HEADS = 2
A_GROUP = A_HEADS // A_KV_HEADS
A_PREV = 128
T5_BUCKETS = 32
T5_MAX_DIST = 128

B_HEADS = 8
B_PREV = 512
B_REL_CLIP = 128

A_QW = A_HEADS * HEAD_DIM
A_KVW = A_KV_HEADS * HEAD_DIM
B_W = B_HEADS * HEAD_DIM
D_FF = 3072

_C_QA = 0
_C_KA = _C_QA + A_QW
_C_VA = _C_KA + A_KVW
_C_QB = _C_VA + A_KVW
_C_KB = _C_QB + B_W
_C_VB = _C_KB + B_W
_C_GA = _C_VB + B_W
_C_GB = _C_GA + D_MODEL
IN_WIDTH = _C_GB + D_MODEL

LANES = 128
GROUP = 2 * CHUNK
A_WIN = A_PREV + GROUP
B_WIN = B_PREV + GROUP
SEQ_TILE = 512
FF_CHUNK = 512
N_FF = D_FF // FF_CHUNK
FF_LOOKAHEAD = 4
VMEM_LIMIT = 56 * 1024 * 1024
LOOKAHEAD = 1
GATE_FIRST_STEP = 1
GATE_STEP_STRIDE = 3


def _rms(x, g):
    y = x * lax.rsqrt(jnp.mean(x * x, axis=-1, keepdims=True) + EPS)
    return y * g


def _dot(a, b):
    return jnp.dot(a, b, preferred_element_type=F32)


def _dot_nt(a, b):
    return lax.dot_general(a, b, (((1,), (1,)), ((), ())), preferred_element_type=F32)


def _t5_bucket(n):
    half = T5_BUCKETS // 2
    max_exact = half // 2
    ret = jnp.where(n < 0, half, 0)
    a = jnp.abs(n)
    af = jnp.maximum(a, 1).astype(jnp.float32)
    large = max_exact + (jnp.log(af / max_exact) / math.log(T5_MAX_DIST / max_exact)
                         * (half - max_exact)).astype(jnp.int32)
    large = jnp.minimum(large, half - 1)
    return ret + jnp.where(a < max_exact, a, large)


def _window_valid(n_prev, win):
    j = np.arange(win)[:, None]
    ci = np.arange(GROUP)[None, :] // CHUNK
    return ((j >= CHUNK * ci) & (j < CHUNK * ci + n_prev + CHUNK)).astype(np.int32)


def _lookup(idx, table, lo, hi):
    acc = jnp.zeros((table.shape[0], idx.shape[1]), F32)
    for b in range(lo, hi):
        acc = jnp.where(idx == b, table[:, b:b + 1], acc)
    return acc


def _toeplitz(vec, rows, first):
    w = vec.shape[1]
    return pltpu.roll(jnp.broadcast_to(vec, (rows, w)), (w - first) % w, 1, stride=1, stride_axis=0)


def _bias_kernel(t5_ref, rel_ref, ja_ref, jb_ref, jsa_ref, jsb_ref, va_ref, vb_ref, vsa_ref, vsb_ref,
                 bta_ref, btb_ref, bsa_ref, bsb_ref, *, s, jb_range, jsb_range):
    t5 = t5_ref[...]
    rel = rel_ref[...]

    fa = _lookup(ja_ref[...], t5, 0, T5_BUCKETS)
    va = va_ref[...] != 0
    for h in range(A_HEADS):
        t = _toeplitz(fa[h:h + 1, :], A_WIN, A_WIN - 1)[:, 0:GROUP]
        bta_ref[:, GROUP * h:GROUP * (h + 1)] = jnp.where(va, t, NEG_INF)

    fb = _lookup(jb_ref[...], rel, *jb_range)
    for r in range(B_WIN // LANES):
        base = B_WIN - LANES * r
        vb = vb_ref[LANES * r:LANES * (r + 1), :] != 0
        for h in range(B_HEADS):
            seg = fb[h:h + 1, base - LANES:base + LANES]
            t = _toeplitz(seg, LANES, LANES)[:, 0:GROUP]
            btb_ref[h, LANES * r:LANES * (r + 1), :] = jnp.where(vb, t, NEG_INF)

    fsa = _lookup(jsa_ref[...], t5, 0, T5_BUCKETS)
    fsb = _lookup(jsb_ref[...], rel, *jsb_range)
    vsa = vsa_ref[...] != 0
    vsb = vsb_ref[...] != 0
    for h in range(A_HEADS):
        bsa_ref[h] = jnp.where(vsa, _toeplitz(fsa[h:h + 1, :], s, s - 1), NEG_INF)
    for h in range(B_HEADS):
        bsb_ref[h] = jnp.where(vsb, _toeplitz(fsb[h:h + 1, :], s, s - 1), NEG_INF)


def _build_bias(t5_table, rel_table, s, la, lb, ka_pad, kb_pad):
    wa = 3 * LANES
    assert A_WIN - 1 + GROUP <= wa and la + 2 * s - 2 < ka_pad and lb + 2 * s - 2 < kb_pad
    bucket = lambda n: (_t5_bucket(n).astype(jnp.int32) & (T5_BUCKETS - 1)).reshape(1, -1)
    ja = bucket(jnp.arange(wa) - (A_WIN - 1) + A_PREV)
    jb = np.clip(np.arange(B_WIN + GROUP) - B_WIN + B_PREV, -B_REL_CLIP, B_REL_CLIP) + B_REL_CLIP
    jsa = bucket(la + s - 1 - jnp.arange(ka_pad))
    jsb = np.clip(lb + s - 1 - np.arange(kb_pad), -B_REL_CLIP, B_REL_CLIP) + B_REL_CLIP
    key = lambda n: np.broadcast_to(np.arange(n)[None, :], (s, n))
    vsa = (key(ka_pad) < la + s).astype(np.int32)
    vsb = (key(kb_pad) < lb + s).astype(np.int32)
    as_row = lambda v: jnp.asarray(v.astype(np.int32).reshape(1, -1))
    rng = lambda v: (int(v.min()), int(v.max()) + 1)

    vmem = pl.BlockSpec(memory_space=pltpu.VMEM)
    return pl.pallas_call(
        functools.partial(_bias_kernel, s=s, jb_range=rng(jb), jsb_range=rng(jsb)),
        out_shape=(
            jax.ShapeDtypeStruct((A_WIN, A_HEADS * GROUP), F32),
            jax.ShapeDtypeStruct((B_HEADS, B_WIN, GROUP), F32),
            jax.ShapeDtypeStruct((A_HEADS, s, ka_pad), F32),
            jax.ShapeDtypeStruct((B_HEADS, s, kb_pad), F32),
        ),
        in_specs=[vmem] * 10,
        out_specs=(vmem,) * 4,
        name="bias_tables",
    )(t5_table, rel_table, ja, as_row(jb), jsa, as_row(jsb),
      jnp.asarray(_window_valid(A_PREV, A_WIN)), jnp.asarray(_window_valid(B_PREV, B_WIN)),
      jnp.asarray(vsa), jnp.asarray(vsb))


def _ordering_zero(x, never):
    return jnp.where(never, x, 0.0)


def _add_to_corner(x, z, cols=(0,)):
    pieces, at = [], 0
    for c in cols:
        pieces += [x[0:16, at:c], x[0:16, c:c + LANES] + z]
        at = c + LANES
    pieces.append(x[0:16, at:])
    head = jnp.concatenate([q for q in pieces if q.shape[1]], axis=1)
    return jnp.concatenate([head, x[16:, :]], axis=0)


def _scores_a(g, ka_s, qt_s):
    tok = slice(GROUP * g, GROUP * (g + 1))
    kwin = ka_s[GROUP * g:GROUP * g + A_WIN, :]
    q = qt_s[0:A_QW, tok]
    zero = jnp.zeros((HEAD_DIM, GROUP), BF16)
    cols = []
    for i in range(A_HEADS):
        qi = q[HEAD_DIM * i:HEAD_DIM * (i + 1), :]
        cols.append(jnp.concatenate([qi, zero] if i < A_GROUP else [zero, qi], axis=0))
    return _dot(kwin, jnp.concatenate(cols, axis=1))


def _scores_b(g, pp, kb_s, qt_s):
    tok = slice(GROUP * g, GROUP * (g + 1))
    kwin = kb_s[GROUP * g:GROUP * g + B_WIN, 2 * LANES * pp:2 * LANES * (pp + 1)]
    qrow = A_QW + 2 * LANES * pp
    zero = jnp.zeros((HEAD_DIM, GROUP), BF16)
    cols = []
    for i in range(4):
        qi = qt_s[qrow + HEAD_DIM * i:qrow + HEAD_DIM * (i + 1), tok]
        cols.append(jnp.concatenate([zero] * i + [qi] + [zero] * (3 - i), axis=0))
    return _dot(kwin, jnp.concatenate(cols, axis=1))


def _softmax_keys(st, bias, sink_row, order_after, order_cols=(0,)):
    st = st + bias
    m = jnp.max(st, axis=0, keepdims=True)
    if sink_row is not None:
        m = jnp.maximum(m, sink_row)
    p = jnp.exp(st - m)
    if order_after is not None:
        p = _add_to_corner(p, order_after, order_cols)
    return p.astype(BF16), m


def _weighted_values(vwin, present, p, m, sink_row):
    ot = _dot(jnp.concatenate([vwin, present], axis=0), p)
    l = ot[2 * HEAD_DIM:2 * HEAD_DIM + 1, :]
    if sink_row is not None:
        l = l + jnp.exp(sink_row - m)
    return ot[0:2 * HEAD_DIM, :] * (1.0 / l)


def _values_a(g, vat_s, present_s, p, m, sink_row, oa_s):
    tok = slice(GROUP * g, GROUP * (g + 1))
    c0 = B_PREV - A_PREV + GROUP * g
    ot = _weighted_values(vat_s[:, GROUP * g:GROUP * g + A_WIN], present_s[:, c0:c0 + A_WIN], p, m, sink_row)
    for j in range(A_HEADS // 2):
        parts = []
        for i in (2 * j, 2 * j + 1):
            rows = slice(0, HEAD_DIM) if i < A_GROUP else slice(HEAD_DIM, 2 * HEAD_DIM)
            parts.append(ot[rows, GROUP * i:GROUP * (i + 1)])
        blk = jnp.concatenate(parts, axis=0)
        oa_s[tok, LANES * j:LANES * (j + 1)] = blk.T.astype(BF16)


def _values_b(g, p2, vbt_s, present_s, p, m, ob_s):
    tok = slice(GROUP * g, GROUP * (g + 1))
    keys = slice(GROUP * g, GROUP * g + B_WIN)
    lanes = slice(LANES * p2, LANES * (p2 + 1))
    ot = _weighted_values(vbt_s[lanes, keys], present_s[:, keys], p, m, None)
    blk = jnp.concatenate([ot[0:HEAD_DIM, 0:GROUP], ot[HEAD_DIM:2 * HEAD_DIM, GROUP:2 * GROUP]], axis=0)
    ob_s[tok, lanes] = blk.T.astype(BF16)


def _attend_tile(n_groups, never, ka_s, vat_s, kb_s, vbt_s, present_s, qt_s, bta_ref, btb_ref, sink_row, oa_s, ob_s,
                 fillers):
    steps = []
    for g in range(n_groups):
        steps.append(("a", g, 0))
        steps.extend(("b", g, pp) for pp in range(B_HEADS // 4))

    def scores(step):
        kind, g, pp = step
        return _scores_a(g, ka_s, qt_s) if kind == "a" else _scores_b(g, pp, kb_s, qt_s)

    ahead = [scores(step) for step in steps[:LOOKAHEAD]]
    for k, (kind, g, pp) in enumerate(steps):
        st, order_after = ahead.pop(0), None
        if k + LOOKAHEAD < len(steps):
            ahead.append(scores(steps[k + LOOKAHEAD]))
            order_after = _ordering_zero(ahead[-1][0:16, 0:LANES], never)
        if fillers.get(k) is not None:
            z = _ordering_zero(fillers[k](), never)
            order_after = z if order_after is None else order_after + z
        if kind == "a":
            p, m = _softmax_keys(st, bta_ref[...], sink_row, order_after)
            _values_a(g, vat_s, present_s, p, m, sink_row, oa_s)
        else:
            bias = jnp.concatenate([btb_ref[4 * pp + i] for i in range(4)], axis=1)
            p, m = _softmax_keys(st, bias, None, order_after, (0, 2 * GROUP))
            for j in range(2):
                cols = slice(2 * GROUP * j, 2 * GROUP * (j + 1))
                _values_b(g, 2 * pp + j, vbt_s, present_s, p[:, cols], m[:, cols], ob_s)


def _mixer_kernel(x_ref, gpre_ref, gpost_ref, win_ref, wt_ref,
                  woa_ref, wob_ref, wout_ref, sink_ref, bta_ref, btb_ref, never_ref,
                  x1_ref, kat_ref, vat_ref, kbt_ref, vbt_ref,
                  h_s, qt_s, ka_s, vat_s, kb_s, vbt_s, present_s, oa_s, ob_s, mg_s, sg_s,
                  ka_f, va_f, kb_f, vb_f, *, tile, n_tiles):
    i = pl.program_id(1)

    @pl.when(i == 0)
    def _():
        ka_s[0:A_PREV, :] = jnp.zeros((A_PREV, A_KVW), BF16)
        vat_s[:, 0:A_PREV] = jnp.zeros((A_KVW, A_PREV), BF16)
        kb_s[0:B_PREV, :] = jnp.zeros((B_PREV, B_W), BF16)
        vbt_s[:, 0:B_PREV] = jnp.zeros((B_W, B_PREV), BF16)
        present_s[:, 0:B_PREV] = jnp.zeros((16, B_PREV), BF16)

    @pl.when(i > 0)
    def _():
        ka_s[0:A_PREV, :] = ka_s[tile:tile + A_PREV, :]
        vat_s[:, 0:A_PREV] = vat_s[:, tile:tile + A_PREV]
        kb_s[0:B_PREV, :] = kb_s[tile:tile + B_PREV, :]
        vbt_s[:, 0:B_PREV] = vbt_s[:, tile:tile + B_PREV]
        present_s[:, 0:B_PREV] = present_s[:, tile:tile + B_PREV]

    x = x_ref[...]
    h = _rms(x, gpre_ref[...]).astype(BF16)
    h_s[...] = h

    ka = _dot(h, win_ref[:, _C_KA:_C_KA + A_KVW])
    kb = _dot(h, win_ref[:, _C_KB:_C_KB + B_W])
    ka_s[A_PREV:A_PREV + tile, :] = ka.astype(BF16)
    kb_s[B_PREV:B_PREV + tile, :] = kb.astype(BF16)
    nq = A_QW + B_W
    qt_s[...] = (_dot_nt(wt_ref[0:nq, :], h) * QK_SCALE).astype(BF16)
    vat = _dot_nt(wt_ref[nq:nq + A_KVW, :], h)
    vbt = _dot_nt(wt_ref[nq + A_KVW:nq + A_KVW + B_W, :], h)
    vat_s[:, A_PREV:A_PREV + tile] = vat.astype(BF16)
    vbt_s[:, B_PREV:B_PREV + tile] = vbt.astype(BF16)
    first_row = lax.broadcasted_iota(jnp.int32, (16, tile), 0) == 0
    present_s[:, B_PREV:B_PREV + tile] = jnp.where(first_row, 1.0, 0.0).astype(BF16)
    ka_f[...] = ka[tile - A_PREV:, :]
    kb_f[...] = kb[tile - B_PREV:, :]
    va_f[...] = vat[:, tile - A_PREV:]
    vb_f[...] = vbt[:, tile - B_PREV:]

    gate_w = 512
    n_steps = (tile // GROUP) * (1 + B_HEADS // 4)

    def gate_piece(j):
        def run():
            g = _dot(h_s[...], win_ref[:, _C_GA + gate_w * j:_C_GA + gate_w * (j + 1)])
            sg_s[:, gate_w * j:gate_w * (j + 1)] = jax.nn.sigmoid(g)
            return g[0:16, 0:LANES]
        return run

    n_pieces = 2 * D_MODEL // gate_w
    fillers = {GATE_FIRST_STEP + GATE_STEP_STRIDE * j: gate_piece(j) for j in range(n_pieces)}
    assert max(fillers) < n_steps
    _attend_tile(tile // GROUP, never_ref[...] != 0, ka_s, vat_s, kb_s, vbt_s, present_s, qt_s,
                 bta_ref, btb_ref, sink_ref[...], oa_s, ob_s, fillers)

    oa = oa_s[...]
    ob = ob_s[...]
    cw = 512
    for c in range(D_MODEL // cw):
        cs = slice(cw * c, cw * (c + 1))
        ya = _dot(oa, woa_ref[:, cs])
        yb = _dot(ob, wob_ref[:, cs])
        sgb = sg_s[:, D_MODEL + cw * c:D_MODEL + cw * (c + 1)]
        mg_s[:, cs] = (sg_s[:, cs] * ya + sgb * yb).astype(BF16)
    z = _dot(mg_s[...], wout_ref[...])
    x1_ref[...] = x_ref[...] + _rms(z, gpost_ref[...])

    @pl.when(i == n_tiles - 1)
    def _():
        kat_ref[...] = ka_f[...]
        kbt_ref[...] = kb_f[...]
        vat_ref[...] = va_f[...].T
        vbt_ref[...] = vb_f[...].T


def _const_spec(shape):
    nd = len(shape)
    return pl.BlockSpec(shape, lambda *_: (0,) * nd, pipeline_mode=pl.Buffered(1))


def _prompt_mixer(x, gpre, gpost, w, sink_row, bta, btb):
    bsz, seq, _ = x.shape
    tile = SEQ_TILE
    assert seq % tile == 0 and tile % GROUP == 0 and tile >= B_PREV
    n_tiles = seq // tile
    row_spec = pl.BlockSpec((None, tile, D_MODEL), lambda b, i: (b, i, 0))

    def tail_spec(rows, width):
        return pl.BlockSpec((None, rows, width), lambda b, i: (b, 0, 0))

    consts = [gpre, gpost, w["in"], w["qvt"],
              w["oa"], w["ob"], w["out"], sink_row, bta, btb, jnp.zeros((16, LANES), jnp.int32)]
    return pl.pallas_call(
        functools.partial(_mixer_kernel, tile=tile, n_tiles=n_tiles),
        grid=(bsz, n_tiles),
        in_specs=[row_spec] + [_const_spec(c.shape) for c in consts],
        out_specs=(row_spec, tail_spec(A_PREV, A_KVW), tail_spec(A_PREV, A_KVW),
                   tail_spec(B_PREV, B_W), tail_spec(B_PREV, B_W)),
        out_shape=(
            jax.ShapeDtypeStruct((bsz, seq, D_MODEL), F32),
            jax.ShapeDtypeStruct((bsz, A_PREV, A_KVW), F32),
            jax.ShapeDtypeStruct((bsz, A_PREV, A_KVW), F32),
            jax.ShapeDtypeStruct((bsz, B_PREV, B_W), F32),
            jax.ShapeDtypeStruct((bsz, B_PREV, B_W), F32),
        ),
        scratch_shapes=[
            pltpu.VMEM((tile, D_MODEL), BF16),
            pltpu.VMEM((A_QW + B_W, tile), BF16),
            pltpu.VMEM((A_PREV + tile, A_KVW), BF16),
            pltpu.VMEM((A_KVW, A_PREV + tile), BF16),
            pltpu.VMEM((B_PREV + tile, B_W), BF16),
            pltpu.VMEM((B_W, B_PREV + tile), BF16),
            pltpu.VMEM((16, B_PREV + tile), BF16),
            pltpu.VMEM((tile, A_QW), BF16),
            pltpu.VMEM((tile, B_W), BF16),
            pltpu.VMEM((tile, D_MODEL), BF16),
            pltpu.VMEM((tile, 2 * D_MODEL), F32),
            pltpu.VMEM((A_PREV, A_KVW), F32),
            pltpu.VMEM((A_KVW, A_PREV), F32),
            pltpu.VMEM((B_PREV, B_W), F32),
            pltpu.VMEM((B_W, B_PREV), F32),
        ],
        compiler_params=pltpu.CompilerParams(
            dimension_semantics=("arbitrary", "arbitrary"), vmem_limit_bytes=VMEM_LIMIT),
        name="prompt_mixer",
    )(x, *consts)


def _conv_gelu_gate(u, gt, u1, u2, cw, cb):
    c = cb + cw[0:1, :] * u2
    c = c + cw[1:2, :] * u1
    c = c + cw[2:3, :] * u
    return jax.nn.gelu(c, approximate=True) * gt


def _ffn_kernel(x_ref, gpre_ref, gpost_ref, wupg_ref, cw_ref, cb_ref, wdown_ref, never_ref,
                x2_ref, tail_ref, carry_s, next_s, *, tile):
    i = pl.program_id(1)

    @pl.when(i == 0)
    def _():
        carry_s[...] = jnp.zeros_like(carry_s)

    @pl.when(i > 0)
    def _():
        carry_s[...] = next_s[...]

    x1 = x_ref[...]
    xn = _rms(x1, gpre_ref[...]).astype(BF16)
    never = never_ref[...] != 0
    row = lax.broadcasted_iota(jnp.int32, (tile, FF_CHUNK), 0)
    acc = jnp.zeros((tile, D_MODEL), F32)

    def up_gate(c):
        cols = slice(FF_CHUNK * c, FF_CHUNK * (c + 1))
        gcols = slice(D_FF + FF_CHUNK * c, D_FF + FF_CHUNK * (c + 1))
        return _dot(xn, wupg_ref[:, cols]), _dot(xn, wupg_ref[:, gcols])

    ahead = [up_gate(c) for c in range(min(FF_LOOKAHEAD, N_FF))]
    for c in range(N_FF):
        u, gt = ahead.pop(0)
        cols = slice(FF_CHUNK * c, FF_CHUNK * (c + 1))
        prev = carry_s[c]
        u1 = jnp.where(row == 0, prev[7:8, :], pltpu.roll(u, 1, 0))
        u2 = jnp.where(row == 0, prev[6:7, :], jnp.where(row == 1, prev[7:8, :], pltpu.roll(u, 2, 0)))
        next_s[c] = u[tile - 8:tile, :]
        tail_ref[:, cols] = u[tile - 2:tile, :]
        act = _conv_gelu_gate(u, gt, u1, u2, cw_ref[:, cols], cb_ref[:, cols])
        if c + FF_LOOKAHEAD < N_FF:
            ahead.append(up_gate(c + FF_LOOKAHEAD))
            act = _add_to_corner(act, _ordering_zero(ahead[-1][0][0:16, 0:LANES], never))
        acc = acc + _dot(act.astype(BF16), wdown_ref[cols, :])
    x2_ref[...] = x1 + _rms(acc, gpost_ref[...])


def _prompt_ffn(x1, gpre, gpost, w):
    bsz, seq, _ = x1.shape
    tile = SEQ_TILE
    assert seq % tile == 0
    row_spec = pl.BlockSpec((None, tile, D_MODEL), lambda b, i: (b, i, 0))
    consts = [gpre, gpost, w["upg"], w["cw"], w["cb"], w["down"], jnp.zeros((16, LANES), jnp.int32)]
    return pl.pallas_call(
        functools.partial(_ffn_kernel, tile=tile),
        grid=(bsz, seq // tile),
        in_specs=[row_spec] + [_const_spec(c.shape) for c in consts],
        out_specs=(row_spec, pl.BlockSpec((None, 2, D_FF), lambda b, i: (b, 0, 0))),
        out_shape=(jax.ShapeDtypeStruct((bsz, seq, D_MODEL), F32),
                   jax.ShapeDtypeStruct((bsz, 2, D_FF), F32)),
        scratch_shapes=[pltpu.VMEM((N_FF, 8, FF_CHUNK), F32), pltpu.VMEM((N_FF, 8, FF_CHUNK), F32)],
        compiler_params=pltpu.CompilerParams(
            dimension_semantics=("arbitrary", "arbitrary"), vmem_limit_bytes=VMEM_LIMIT),
        name="prompt_ffn",
    )(x1, *consts)


def _sample_proj_kernel(x_ref, g_ref, w_ref, p_ref):
    h = _rms(x_ref[...], g_ref[...]).astype(BF16)
    p_ref[...] = _dot(h, w_ref[...])


def _sample_proj(xs, gpre, w_in_b):
    rows = xs.shape[0]
    nblk = 2
    width = IN_WIDTH // nblk
    assert width % LANES == 0
    return pl.pallas_call(
        _sample_proj_kernel,
        grid=(nblk,),
        in_specs=[pl.BlockSpec((rows, D_MODEL), lambda j: (0, 0)),
                  pl.BlockSpec((1, D_MODEL), lambda j: (0, 0)),
                  pl.BlockSpec((D_MODEL, width), lambda j: (0, j))],
        out_specs=pl.BlockSpec((rows, width), lambda j: (0, j)),
        out_shape=jax.ShapeDtypeStruct((rows, IN_WIDTH), F32),
        compiler_params=pltpu.CompilerParams(dimension_semantics=("arbitrary",)),
        name="sample_proj",
    )(xs, gpre, w_in_b)


def _sample_attn_kernel(sink_ref, p_ref, cak_ref, cav_ref, cbk_ref, cbv_ref, bsa_ref, bsb_ref,
                        oa_ref, ob_ref, kan_ref, van_ref, kbn_ref, vbn_ref, ka_s, va_s, kb_s, vb_s, *, s, la, lb):
    p = p_ref[...]
    kan_ref[...] = p[:, _C_KA:_C_KA + A_KVW]
    van_ref[...] = p[:, _C_VA:_C_VA + A_KVW]
    kbn_ref[...] = p[:, _C_KB:_C_KB + B_W]
    vbn_ref[...] = p[:, _C_VB:_C_VB + B_W]
    ka_pad = ka_s.shape[0]
    kb_pad = kb_s.shape[0]

    def fill(dst, cache_ref, new, n_cache, n_pad):
        dst[0:n_cache, :] = cache_ref[...].astype(BF16)
        dst[n_cache:n_cache + s, :] = new.astype(BF16)
        dst[n_cache + s:n_pad, :] = jnp.zeros((n_pad - n_cache - s, dst.shape[1]), BF16)

    fill(ka_s, cak_ref, p[:, _C_KA:_C_KA + A_KVW], la, ka_pad)
    fill(va_s, cav_ref, p[:, _C_VA:_C_VA + A_KVW], la, ka_pad)
    fill(kb_s, cbk_ref, p[:, _C_KB:_C_KB + B_W], lb, kb_pad)
    fill(vb_s, cbv_ref, p[:, _C_VB:_C_VB + B_W], lb, kb_pad)

    lo = lax.broadcasted_iota(jnp.int32, (s, LANES), 1) < HEAD_DIM
    top = lax.broadcasted_iota(jnp.int32, (2 * s, 1), 0) < s

    def softmax_pv(qs, k, v, bias, sink):
        st = _dot_nt(qs.astype(BF16), k) + bias
        m = jnp.max(st, axis=-1, keepdims=True)
        if sink is not None:
            m = jnp.maximum(m, sink)
        e = jnp.exp(st - m)
        l = jnp.sum(e, axis=-1, keepdims=True)
        if sink is not None:
            l = l + jnp.exp(sink - m)
        return _dot(e.astype(BF16), v) * (1.0 / l)

    ka = ka_s[...]
    va = va_s[...]
    for j in range(A_HEADS // 2):
        q = p[:, _C_QA + LANES * j:_C_QA + LANES * (j + 1)] * QK_SCALE
        qr = pltpu.roll(q, HEAD_DIM, 1)
        if 2 * j < A_GROUP:
            qs = jnp.concatenate([jnp.where(lo, q, 0.0), jnp.where(lo, qr, 0.0)], axis=0)
        else:
            qs = jnp.concatenate([jnp.where(lo, 0.0, qr), jnp.where(lo, 0.0, q)], axis=0)
        bias = jnp.concatenate([bsa_ref[2 * j], bsa_ref[2 * j + 1]], axis=0)
        sink = jnp.where(top, sink_ref[2 * j], sink_ref[2 * j + 1])
        o = softmax_pv(qs, ka, va, bias, sink)
        oe, oo = o[0:s], o[s:2 * s]
        if 2 * j < A_GROUP:
            out = jnp.where(lo, oe, pltpu.roll(oo, HEAD_DIM, 1))
        else:
            out = jnp.where(lo, pltpu.roll(oe, HEAD_DIM, 1), oo)
        oa_ref[:, LANES * j:LANES * (j + 1)] = out

    for j in range(B_HEADS // 2):
        lanes = slice(LANES * j, LANES * (j + 1))
        q = p[:, _C_QB + LANES * j:_C_QB + LANES * (j + 1)] * QK_SCALE
        qs = jnp.concatenate([jnp.where(lo, q, 0.0), jnp.where(lo, 0.0, q)], axis=0)
        bias = jnp.concatenate([bsb_ref[2 * j], bsb_ref[2 * j + 1]], axis=0)
        o = softmax_pv(qs, kb_s[:, lanes], vb_s[:, lanes], bias, None)
        ob_ref[:, lanes] = jnp.where(lo, o[0:s], o[s:2 * s])


def _sample_attn(sink, p, cak, cav, cbk, cbv, bsa, bsb, s):
    nb, la, _ = cak.shape
    lb = cbk.shape[1]
    ka_pad, kb_pad = bsa.shape[-1], bsb.shape[-1]
    assert s % 16 == 0 and la % 16 == 0 and lb % 16 == 0
    row = lambda w: pl.BlockSpec((s, w), lambda b: (b, 0))
    cache = lambda n, w: pl.BlockSpec((None, n, w), lambda b: (b, 0, 0))
    full = lambda a: pl.BlockSpec(a.shape, lambda b: (0,) * a.ndim)
    widths = (A_QW, B_W, A_KVW, A_KVW, B_W, B_W)
    return pl.pallas_call(
        functools.partial(_sample_attn_kernel, s=s, la=la, lb=lb),
        grid=(nb,),
        in_specs=[pl.BlockSpec(memory_space=pltpu.SMEM), row(IN_WIDTH),
                  cache(la, A_KVW), cache(la, A_KVW), cache(lb, B_W), cache(lb, B_W), full(bsa), full(bsb)],
        out_specs=tuple(row(w) for w in widths),
        out_shape=tuple(jax.ShapeDtypeStruct((nb * s, w), F32) for w in widths),
        scratch_shapes=[pltpu.VMEM((ka_pad, A_KVW), BF16), pltpu.VMEM((ka_pad, A_KVW), BF16),
                        pltpu.VMEM((kb_pad, B_W), BF16), pltpu.VMEM((kb_pad, B_W), BF16)],
        compiler_params=pltpu.CompilerParams(dimension_semantics=("arbitrary",)),
        name="sample_attn",
    )(sink, p, cak, cav, cbk, cbv, bsa, bsb)


def _sample_tail_kernel(x_ref, oa_ref, ob_ref, p_ref, woa_ref, wob_ref, wout_ref,
                        gpost_ref, gpre2_ref, gpost2_ref, init_ref,
                        wup_ref, wgate_ref, cw_ref, cb_ref, wdown_ref,
                        x2_ref, u_ref, x1_s, xn_s, acc_s, *, s):
    c = pl.program_id(0)

    @pl.when(c == 0)
    def _():
        ya = _dot(oa_ref[...].astype(BF16), woa_ref[...])
        yb = _dot(ob_ref[...].astype(BF16), wob_ref[...])
        ga = p_ref[:, _C_GA:_C_GA + D_MODEL]
        gb = p_ref[:, _C_GB:_C_GB + D_MODEL]
        merged = jax.nn.sigmoid(ga) * ya + jax.nn.sigmoid(gb) * yb
        z = _dot(merged.astype(BF16), wout_ref[...])
        x1 = x_ref[...] + _rms(z, gpost_ref[...])
        x1_s[...] = x1
        xn_s[...] = _rms(x1, gpre2_ref[...]).astype(BF16)
        acc_s[...] = jnp.zeros_like(acc_s)

    xn = xn_s[...]
    u = _dot(xn, wup_ref[...])
    gt = _dot(xn, wgate_ref[...])
    u_ref[...] = u
    pos = lax.broadcasted_iota(jnp.int32, u.shape, 0) % s
    init = init_ref[...]
    u1 = jnp.where(pos == 0, pltpu.roll(init, u.shape[0] - 1, 0), pltpu.roll(u, 1, 0))
    u2 = jnp.where(pos < 2, init, pltpu.roll(u, 2, 0))
    act = _conv_gelu_gate(u, gt, u1, u2, cw_ref[...], cb_ref[...])
    acc_s[...] += _dot(act.astype(BF16), wdown_ref[...])

    @pl.when(c == pl.num_programs(0) - 1)
    def _():
        x2_ref[...] = x1_s[...] + _rms(acc_s[...], gpost2_ref[...])


def _sample_tail(xs, oa, ob, p, w, gpost, gpre2, gpost2, init, s):
    rows = xs.shape[0]
    full = lambda a: pl.BlockSpec(a.shape, lambda c: (0,) * a.ndim)
    cols = lambda r: pl.BlockSpec((r, FF_CHUNK), lambda c: (0, c))
    gate_cols = pl.BlockSpec((D_MODEL, FF_CHUNK), lambda c: (0, N_FF + c))
    down_rows = pl.BlockSpec((FF_CHUNK, D_MODEL), lambda c: (c, 0))
    args = [xs, oa, ob, p, w["oa"], w["ob"], w["out"], gpost, gpre2, gpost2]
    return pl.pallas_call(
        functools.partial(_sample_tail_kernel, s=s),
        grid=(N_FF,),
        in_specs=[full(a) for a in args] + [cols(rows), cols(D_MODEL), gate_cols, cols(3), cols(1), down_rows],
        out_specs=(full(xs), cols(rows)),
        out_shape=(jax.ShapeDtypeStruct((rows, D_MODEL), F32), jax.ShapeDtypeStruct((rows, D_FF), F32)),
        scratch_shapes=[pltpu.VMEM((rows, D_MODEL), F32), pltpu.VMEM((rows, D_MODEL), BF16),
                        pltpu.VMEM((rows, D_MODEL), F32)],
        compiler_params=pltpu.CompilerParams(dimension_semantics=("arbitrary",)),
        name="sample_tail",
    )(*args, init, w["upg"], w["upg"], w["cw"], w["cb"], w["down"])


def _prep_weights(w_in, w_oa, w_ob, w_out, w_upg, conv_w, conv_b, w_down):
    wb = w_in.astype(BF16)
    qv = jnp.concatenate([wb[:, _C_QA:_C_QA + A_QW], wb[:, _C_QB:_C_QB + B_W],
                          wb[:, _C_VA:_C_VA + A_KVW], wb[:, _C_VB:_C_VB + B_W]], axis=1)
    return {
        "in": wb, "qvt": qv.T,
        "oa": w_oa.astype(BF16), "ob": w_ob.astype(BF16), "out": w_out.astype(BF16),
        "upg": w_upg.astype(BF16), "cw": conv_w, "cb": conv_b.reshape(1, D_FF), "down": w_down.astype(BF16),
    }


def _round_up(n, m):
    return (n + m - 1) // m * m


def kernel(x_prompt, x_sample, cache_a_k, cache_a_v, cache_b_k, cache_b_v, state_conv,
           w_in, w_oa, w_ob, w_out, sink_a, t5_table, rel_table_b,
           g_pre_mix, g_post_mix, g_pre_ffn, g_post_ffn, w_upg, conv_w, conv_b, w_down):
    depth = w_in.shape[0]
    bsz, seq, _ = x_prompt.shape
    nb, s, _ = x_sample.shape
    la, lb = cache_a_k.shape[2], cache_b_k.shape[2]
    ka_pad = _round_up(la + s, LANES)
    kb_pad = _round_up(lb + s, LANES)

    xp = x_prompt
    xs = x_sample.reshape(nb * s, D_MODEL)
    prompt_states = [[] for _ in range(5)]
    sample_states = [[] for _ in range(5)]
    for l in range(depth):
        w = _prep_weights(w_in[l], w_oa[l], w_ob[l], w_out[l], w_upg[l], conv_w[l], conv_b[l], w_down[l])
        row = lambda g: g[l].reshape(1, D_MODEL)
        bta, btb, bsa, bsb = _build_bias(t5_table, rel_table_b[l], s, la, lb, ka_pad, kb_pad)
        sink_row = jnp.repeat(sink_a[l], GROUP).reshape(1, A_HEADS * GROUP)

        x1, ka_t, va_t, kb_t, vb_t = _prompt_mixer(xp, row(g_pre_mix), row(g_post_mix), w, sink_row, bta, btb)
        xp, conv_t = _prompt_ffn(x1, row(g_pre_ffn), row(g_post_ffn), w)
        prompt_states[0].append(ka_t.reshape(bsz, A_PREV, A_KV_HEADS, HEAD_DIM))
        prompt_states[1].append(va_t.reshape(bsz, A_PREV, A_KV_HEADS, HEAD_DIM))
        prompt_states[2].append(kb_t.reshape(bsz, B_PREV, B_HEADS, HEAD_DIM))
        prompt_states[3].append(vb_t.reshape(bsz, B_PREV, B_HEADS, HEAD_DIM))
        prompt_states[4].append(conv_t)

        p = _sample_proj(xs, row(g_pre_mix), w["in"])
        oa, ob, ka_n, va_n, kb_n, vb_n = _sample_attn(
            sink_a[l], p, cache_a_k[l].reshape(nb, la, A_KVW), cache_a_v[l].reshape(nb, la, A_KVW),
            cache_b_k[l].reshape(nb, lb, B_W), cache_b_v[l].reshape(nb, lb, B_W), bsa, bsb, s)
        init = jnp.pad(state_conv[l], ((0, 0), (0, s - 2), (0, 0))).reshape(nb * s, D_FF)
        xs, u = _sample_tail(xs, oa, ob, p, w, row(g_post_mix), row(g_pre_ffn), row(g_post_ffn), init, s)
        sample_states[0].append(ka_n.reshape(nb, s, A_KV_HEADS, HEAD_DIM))
        sample_states[1].append(va_n.reshape(nb, s, A_KV_HEADS, HEAD_DIM))
        sample_states[2].append(kb_n.reshape(nb, s, B_HEADS, HEAD_DIM))
        sample_states[3].append(vb_n.reshape(nb, s, B_HEADS, HEAD_DIM))
        sample_states[4].append(u.reshape(nb, s, D_FF)[:, -2:])

    ps = [jnp.stack(a, axis=0) for a in prompt_states]
    ss = [jnp.stack(a, axis=0) for a in sample_states]
    return (xp, xs.reshape(nb, s, D_MODEL), ps[0], ps[1], ps[2], ps[3], ps[4],
            ss[0], ss[1], ss[2], ss[3], ss[4])
```

```python
import functools
import math

import numpy as np
import jax
import jax.numpy as jnp
from jax import lax
from jax.experimental import pallas as pl
from jax.experimental.pallas import tpu as pltpu

F32 = jnp.float32
BF16 = jnp.bfloat16

D_MODEL = 1024
CHUNK = 64
HEAD_DIM = 64
EPS = 1e-6
NEG_INF = -1e30
QK_SCALE = HEAD_DIM ** -0.5

A_HEADS = 8
A_KV_HEADS = 2
A_GROUP = A_HEADS // A_KV_HEADS
A_PREV = 128
T5_BUCKETS = 32
T5_MAX_DIST = 128

B_HEADS = 8
B_PREV = 512
B_REL_CLIP = 128

A_QW = A_HEADS * HEAD_DIM
A_KVW = A_KV_HEADS * HEAD_DIM
B_W = B_HEADS * HEAD_DIM
D_FF = 3072

_C_QA = 0
_C_KA = _C_QA + A_QW
_C_VA = _C_KA + A_KVW
_C_QB = _C_VA + A_KVW
_C_KB = _C_QB + B_W
_C_VB = _C_KB + B_W
_C_GA = _C_VB + B_W
_C_GB = _C_GA + D_MODEL
IN_WIDTH = _C_GB + D_MODEL

LANES = 128
GROUP = 2 * CHUNK
A_WIN = A_PREV + GROUP
B_WIN = B_PREV + GROUP
SEQ_TILE = 512
FF_CHUNK = 512
N_FF = D_FF // FF_CHUNK
FF_LOOKAHEAD = 4
VMEM_LIMIT = 56 * 1024 * 1024
LOOKAHEAD = 1
GATE_FIRST_STEP = 1
GATE_STEP_STRIDE = 3


def _rms(x, g):
    y = x * lax.rsqrt(jnp.mean(x * x, axis=-1, keepdims=True) + EPS)
    return y * g


def _dot(a, b):
    return jnp.dot(a, b, preferred_element_type=F32)


def _dot_nt(a, b):
    return lax.dot_general(a, b, (((1,), (1,)), ((), ())), preferred_element_type=F32)


def _t5_bucket(n):
    half = T5_BUCKETS // 2
    max_exact = half // 2
    ret = jnp.where(n < 0, half, 0)
    a = jnp.abs(n)
    af = jnp.maximum(a, 1).astype(jnp.float32)
    large = max_exact + (jnp.log(af / max_exact) / math.log(T5_MAX_DIST / max_exact)
                         * (half - max_exact)).astype(jnp.int32)
    large = jnp.minimum(large, half - 1)
    return ret + jnp.where(a < max_exact, a, large)


def _window_valid(n_prev, win):
    j = np.arange(win)[:, None]
    ci = np.arange(GROUP)[None, :] // CHUNK
    return ((j >= CHUNK * ci) & (j < CHUNK * ci + n_prev + CHUNK)).astype(np.int32)


def _lookup(idx, table, lo, hi):
    acc = jnp.zeros((table.shape[0], idx.shape[1]), F32)
    for b in range(lo, hi):
        acc = jnp.where(idx == b, table[:, b:b + 1], acc)
    return acc


def _toeplitz(vec, rows, first):
    w = vec.shape[1]
    return pltpu.roll(jnp.broadcast_to(vec, (rows, w)), (w - first) % w, 1, stride=1, stride_axis=0)


def _bias_kernel(t5_ref, rel_ref, ja_ref, jb_ref, jsa_ref, jsb_ref, va_ref, vb_ref, vsa_ref, vsb_ref,
                 bta_ref, btb_ref, bsa_ref, bsb_ref, *, s, jb_range, jsb_range):
    t5 = t5_ref[...]
    rel = rel_ref[...]

    fa = _lookup(ja_ref[...], t5, 0, T5_BUCKETS)
    va = va_ref[...] != 0
    for h in range(A_HEADS):
        t = _toeplitz(fa[h:h + 1, :], A_WIN, A_WIN - 1)[:, 0:GROUP]
        bta_ref[:, GROUP * h:GROUP * (h + 1)] = jnp.where(va, t, NEG_INF)

    fb = _lookup(jb_ref[...], rel, *jb_range)
    for r in range(B_WIN // LANES):
        base = B_WIN - LANES * r
        vb = vb_ref[LANES * r:LANES * (r + 1), :] != 0
        for h in range(B_HEADS):
            seg = fb[h:h + 1, base - LANES:base + LANES]
            t = _toeplitz(seg, LANES, LANES)[:, 0:GROUP]
            btb_ref[h, LANES * r:LANES * (r + 1), :] = jnp.where(vb, t, NEG_INF)

    fsa = _lookup(jsa_ref[...], t5, 0, T5_BUCKETS)
    fsb = _lookup(jsb_ref[...], rel, *jsb_range)
    vsa = vsa_ref[...] != 0
    vsb = vsb_ref[...] != 0
    for h in range(A_HEADS):
        bsa_ref[h] = jnp.where(vsa, _toeplitz(fsa[h:h + 1, :], s, s - 1), NEG_INF)
    for h in range(B_HEADS):
        bsb_ref[h] = jnp.where(vsb, _toeplitz(fsb[h:h + 1, :], s, s - 1), NEG_INF)


def _build_bias(t5_table, rel_table, s, la, lb, ka_pad, kb_pad):
    wa = 3 * LANES
    assert A_WIN - 1 + GROUP <= wa and la + 2 * s - 2 < ka_pad and lb + 2 * s - 2 < kb_pad
    bucket = lambda n: (_t5_bucket(n).astype(jnp.int32) & (T5_BUCKETS - 1)).reshape(1, -1)
    ja = bucket(jnp.arange(wa) - (A_WIN - 1) + A_PREV)
    jb = np.clip(np.arange(B_WIN + GROUP) - B_WIN + B_PREV, -B_REL_CLIP, B_REL_CLIP) + B_REL_CLIP
    jsa = bucket(la + s - 1 - jnp.arange(ka_pad))
    jsb = np.clip(lb + s - 1 - np.arange(kb_pad), -B_REL_CLIP, B_REL_CLIP) + B_REL_CLIP
    key = lambda n: np.broadcast_to(np.arange(n)[None, :], (s, n))
    vsa = (key(ka_pad) < la + s).astype(np.int32)
    vsb = (key(kb_pad) < lb + s).astype(np.int32)
    as_row = lambda v: jnp.asarray(v.astype(np.int32).reshape(1, -1))
    rng = lambda v: (int(v.min()), int(v.max()) + 1)

    vmem = pl.BlockSpec(memory_space=pltpu.VMEM)
    return pl.pallas_call(
        functools.partial(_bias_kernel, s=s, jb_range=rng(jb), jsb_range=rng(jsb)),
        out_shape=(
            jax.ShapeDtypeStruct((A_WIN, A_HEADS * GROUP), F32),
            jax.ShapeDtypeStruct((B_HEADS, B_WIN, GROUP), F32),
            jax.ShapeDtypeStruct((A_HEADS, s, ka_pad), F32),
            jax.ShapeDtypeStruct((B_HEADS, s, kb_pad), F32),
        ),
        in_specs=[vmem] * 10,
        out_specs=(vmem,) * 4,
        name="bias_tables",
    )(t5_table, rel_table, ja, as_row(jb), jsa, as_row(jsb),
      jnp.asarray(_window_valid(A_PREV, A_WIN)), jnp.asarray(_window_valid(B_PREV, B_WIN)),
      jnp.asarray(vsa), jnp.asarray(vsb))


def _ordering_zero(x, never):
    return jnp.where(never, x, 0.0)


def _add_to_corner(x, z, cols=(0,)):
    pieces, at = [], 0
    for c in cols:
        pieces += [x[0:16, at:c], x[0:16, c:c + LANES] + z]
        at = c + LANES
    pieces.append(x[0:16, at:])
    head = jnp.concatenate([q for q in pieces if q.shape[1]], axis=1)
    return jnp.concatenate([head, x[16:, :]], axis=0)


def _scores_a(g, ka_s, qt_s):
    tok = slice(GROUP * g, GROUP * (g + 1))
    kwin = ka_s[GROUP * g:GROUP * g + A_WIN, :]
    q = qt_s[0:A_QW, tok]
    zero = jnp.zeros((HEAD_DIM, GROUP), BF16)
    cols = []
    for i in range(A_HEADS):
        qi = q[HEAD_DIM * i:HEAD_DIM * (i + 1), :]
        cols.append(jnp.concatenate([qi, zero] if i < A_GROUP else [zero, qi], axis=0))
    return _dot(kwin, jnp.concatenate(cols, axis=1))


def _scores_b(g, pp, kb_s, qt_s):
    tok = slice(GROUP * g, GROUP * (g + 1))
    kwin = kb_s[GROUP * g:GROUP * g + B_WIN, 2 * LANES * pp:2 * LANES * (pp + 1)]
    qrow = A_QW + 2 * LANES * pp
    zero = jnp.zeros((HEAD_DIM, GROUP), BF16)
    cols = []
    for i in range(4):
        qi = qt_s[qrow + HEAD_DIM * i:qrow + HEAD_DIM * (i + 1), tok]
        cols.append(jnp.concatenate([zero] * i + [qi] + [zero] * (3 - i), axis=0))
    return _dot(kwin, jnp.concatenate(cols, axis=1))


def _softmax_keys(st, bias, sink_row, order_after, order_cols=(0,)):
    st = st + bias
    m = jnp.max(st, axis=0, keepdims=True)
    if sink_row is not None:
        m = jnp.maximum(m, sink_row)
    p = jnp.exp(st - m)
    if order_after is not None:
        p = _add_to_corner(p, order_after, order_cols)
    return p.astype(BF16), m


def _weighted_values(vwin, present, p, m, sink_row):
    ot = _dot(jnp.concatenate([vwin, present], axis=0), p)
    l = ot[2 * HEAD_DIM:2 * HEAD_DIM + 1, :]
    if sink_row is not None:
        l = l + jnp.exp(sink_row - m)
    return ot[0:2 * HEAD_DIM, :] * (1.0 / l)


def _values_a(g, vat_s, present_s, p, m, sink_row, oa_s):
    tok = slice(GROUP * g, GROUP * (g + 1))
    c0 = B_PREV - A_PREV + GROUP * g
    ot = _weighted_values(vat_s[:, GROUP * g:GROUP * g + A_WIN], present_s[:, c0:c0 + A_WIN], p, m, sink_row)
    for j in range(A_HEADS // 2):
        parts = []
        for i in (2 * j, 2 * j + 1):
            rows = slice(0, HEAD_DIM) if i < A_GROUP else slice(HEAD_DIM, 2 * HEAD_DIM)
            parts.append(ot[rows, GROUP * i:GROUP * (i + 1)])
        blk = jnp.concatenate(parts, axis=0)
        oa_s[tok, LANES * j:LANES * (j + 1)] = blk.T.astype(BF16)


def _values_b(g, p2, vbt_s, present_s, p, m, ob_s):
    tok = slice(GROUP * g, GROUP * (g + 1))
    keys = slice(GROUP * g, GROUP * g + B_WIN)
    lanes = slice(LANES * p2, LANES * (p2 + 1))
    ot = _weighted_values(vbt_s[lanes, keys], present_s[:, keys], p, m, None)
    blk = jnp.concatenate([ot[0:HEAD_DIM, 0:GROUP], ot[HEAD_DIM:2 * HEAD_DIM, GROUP:2 * GROUP]], axis=0)
    ob_s[tok, lanes] = blk.T.astype(BF16)


def _attend_tile(n_groups, never, ka_s, vat_s, kb_s, vbt_s, present_s, qt_s, bta_ref, btb_ref, sink_row, oa_s, ob_s,
                 fillers):
    steps = []
    for g in range(n_groups):
        steps.append(("a", g, 0))
        steps.extend(("b", g, pp) for pp in range(B_HEADS // 4))

    def scores(step):
        kind, g, pp = step
        return _scores_a(g, ka_s, qt_s) if kind == "a" else _scores_b(g, pp, kb_s, qt_s)

    ahead = [scores(step) for step in steps[:LOOKAHEAD]]
    for k, (kind, g, pp) in enumerate(steps):
        st, order_after = ahead.pop(0), None
        if k + LOOKAHEAD < len(steps):
            ahead.append(scores(steps[k + LOOKAHEAD]))
            order_after = _ordering_zero(ahead[-1][0:16, 0:LANES], never)
        if fillers.get(k) is not None:
            z = _ordering_zero(fillers[k](), never)
            order_after = z if order_after is None else order_after + z
        if kind == "a":
            p, m = _softmax_keys(st, bta_ref[...], sink_row, order_after)
            _values_a(g, vat_s, present_s, p, m, sink_row, oa_s)
        else:
            bias = jnp.concatenate([btb_ref[4 * pp + i] for i in range(4)], axis=1)
            p, m = _softmax_keys(st, bias, None, order_after, (0, 2 * GROUP))
            for j in range(2):
                cols = slice(2 * GROUP * j, 2 * GROUP * (j + 1))
                _values_b(g, 2 * pp + j, vbt_s, present_s, p[:, cols], m[:, cols], ob_s)


def _mixer_kernel(x_ref, gpre_ref, gpost_ref, win_ref, wt_ref,
                  woa_ref, wob_ref, wout_ref, sink_ref, bta_ref, btb_ref, never_ref,
                  x1_ref, kat_ref, vat_ref, kbt_ref, vbt_ref,
                  h_s, qt_s, ka_s, vat_s, kb_s, vbt_s, present_s, oa_s, ob_s, mg_s, sg_s,
                  ka_f, va_f, kb_f, vb_f, *, tile, n_tiles):
    i = pl.program_id(1)

    @pl.when(i == 0)
    def _():
        ka_s[0:A_PREV, :] = jnp.zeros((A_PREV, A_KVW), BF16)
        vat_s[:, 0:A_PREV] = jnp.zeros((A_KVW, A_PREV), BF16)
        kb_s[0:B_PREV, :] = jnp.zeros((B_PREV, B_W), BF16)
        vbt_s[:, 0:B_PREV] = jnp.zeros((B_W, B_PREV), BF16)
        present_s[:, 0:B_PREV] = jnp.zeros((16, B_PREV), BF16)

    @pl.when(i > 0)
    def _():
        ka_s[0:A_PREV, :] = ka_s[tile:tile + A_PREV, :]
        vat_s[:, 0:A_PREV] = vat_s[:, tile:tile + A_PREV]
        kb_s[0:B_PREV, :] = kb_s[tile:tile + B_PREV, :]
        vbt_s[:, 0:B_PREV] = vbt_s[:, tile:tile + B_PREV]
        present_s[:, 0:B_PREV] = present_s[:, tile:tile + B_PREV]

    x = x_ref[...]
    h = _rms(x, gpre_ref[...]).astype(BF16)
    h_s[...] = h

    kb = _dot(h, win_ref[:, _C_KB:_C_KB + B_W])
    kb_s[B_PREV:B_PREV + tile, :] = kb.astype(BF16)
    nq = A_QW + B_W
    pt = _dot_nt(wt_ref[...], h)
    qt_s[...] = (pt[0:nq, :] * QK_SCALE).astype(BF16)
    vat = pt[nq:nq + A_KVW, :]
    vbt = pt[nq + A_KVW:nq + A_KVW + B_W, :]
    ka = pt[nq + A_KVW + B_W:, :].T
    ka_s[A_PREV:A_PREV + tile, :] = ka.astype(BF16)
    vat_s[:, A_PREV:A_PREV + tile] = vat.astype(BF16)
    vbt_s[:, B_PREV:B_PREV + tile] = vbt.astype(BF16)
    first_row = lax.broadcasted_iota(jnp.int32, (16, tile), 0) == 0
    present_s[:, B_PREV:B_PREV + tile] = jnp.where(first_row, 1.0, 0.0).astype(BF16)
    ka_f[...] = ka[tile - A_PREV:, :]
    kb_f[...] = kb[tile - B_PREV:, :]
    va_f[...] = vat[:, tile - A_PREV:]
    vb_f[...] = vbt[:, tile - B_PREV:]

    gate_w = 512
    n_steps = (tile // GROUP) * (1 + B_HEADS // 4)

    def gate_piece(j):
        def run():
            g = _dot(h_s[...], win_ref[:, _C_GA + gate_w * j:_C_GA + gate_w * (j + 1)])
            sg_s[:, gate_w * j:gate_w * (j + 1)] = jax.nn.sigmoid(g)
            return g[0:16, 0:LANES]
        return run

    n_pieces = 2 * D_MODEL // gate_w
    fillers = {GATE_FIRST_STEP + GATE_STEP_STRIDE * j: gate_piece(j) for j in range(n_pieces)}
    assert max(fillers) < n_steps
    _attend_tile(tile // GROUP, never_ref[...] != 0, ka_s, vat_s, kb_s, vbt_s, present_s, qt_s,
                 bta_ref, btb_ref, sink_ref[...], oa_s, ob_s, fillers)

    oa = oa_s[...]
    ob = ob_s[...]
    cw = 512
    for c in range(D_MODEL // cw):
        cs = slice(cw * c, cw * (c + 1))
        ya = _dot(oa, woa_ref[:, cs])
        yb = _dot(ob, wob_ref[:, cs])
        sgb = sg_s[:, D_MODEL + cw * c:D_MODEL + cw * (c + 1)]
        mg_s[:, cs] = (sg_s[:, cs] * ya + sgb * yb).astype(BF16)
    z = _dot(mg_s[...], wout_ref[...])
    x1_ref[...] = x_ref[...] + _rms(z, gpost_ref[...])

    @pl.when(i == n_tiles - 1)
    def _():
        kat_ref[...] = ka_f[...]
        kbt_ref[...] = kb_f[...]
        vat_ref[...] = va_f[...].T
        vbt_ref[...] = vb_f[...].T


def _const_spec(shape):
    nd = len(shape)
    return pl.BlockSpec(shape, lambda *_: (0,) * nd, pipeline_mode=pl.Buffered(1))


def _prompt_mixer(x, gpre, gpost, w, sink_row, bta, btb):
    bsz, seq, _ = x.shape
    tile = SEQ_TILE
    assert seq % tile == 0 and tile % GROUP == 0 and tile >= B_PREV
    n_tiles = seq // tile
    row_spec = pl.BlockSpec((None, tile, D_MODEL), lambda b, i: (b, i, 0))

    def tail_spec(rows, width):
        return pl.BlockSpec((None, rows, width), lambda b, i: (b, 0, 0))

    consts = [gpre, gpost, w["in"], w["qvt"],
              w["oa"], w["ob"], w["out"], sink_row, bta, btb, jnp.zeros((16, LANES), jnp.int32)]
    return pl.pallas_call(
        functools.partial(_mixer_kernel, tile=tile, n_tiles=n_tiles),
        grid=(bsz, n_tiles),
        in_specs=[row_spec] + [_const_spec(c.shape) for c in consts],
        out_specs=(row_spec, tail_spec(A_PREV, A_KVW), tail_spec(A_PREV, A_KVW),
                   tail_spec(B_PREV, B_W), tail_spec(B_PREV, B_W)),
        out_shape=(
            jax.ShapeDtypeStruct((bsz, seq, D_MODEL), F32),
            jax.ShapeDtypeStruct((bsz, A_PREV, A_KVW), F32),
            jax.ShapeDtypeStruct((bsz, A_PREV, A_KVW), F32),
            jax.ShapeDtypeStruct((bsz, B_PREV, B_W), F32),
            jax.ShapeDtypeStruct((bsz, B_PREV, B_W), F32),
        ),
        scratch_shapes=[
            pltpu.VMEM((tile, D_MODEL), BF16),
            pltpu.VMEM((A_QW + B_W, tile), BF16),
            pltpu.VMEM((A_PREV + tile, A_KVW), BF16),
            pltpu.VMEM((A_KVW, A_PREV + tile), BF16),
            pltpu.VMEM((B_PREV + tile, B_W), BF16),
            pltpu.VMEM((B_W, B_PREV + tile), BF16),
            pltpu.VMEM((16, B_PREV + tile), BF16),
            pltpu.VMEM((tile, A_QW), BF16),
            pltpu.VMEM((tile, B_W), BF16),
            pltpu.VMEM((tile, D_MODEL), BF16),
            pltpu.VMEM((tile, 2 * D_MODEL), F32),
            pltpu.VMEM((A_PREV, A_KVW), F32),
            pltpu.VMEM((A_KVW, A_PREV), F32),
            pltpu.VMEM((B_PREV, B_W), F32),
            pltpu.VMEM((B_W, B_PREV), F32),
        ],
        compiler_params=pltpu.CompilerParams(
            dimension_semantics=("arbitrary", "arbitrary"), vmem_limit_bytes=VMEM_LIMIT),
        name="prompt_mixer",
    )(x, *consts)


def _conv_gelu_gate(u, gt, u1, u2, cw, cb):
    c = cb + cw[0:1, :] * u2
    c = c + cw[1:2, :] * u1
    c = c + cw[2:3, :] * u
    return jax.nn.gelu(c, approximate=True) * gt


def _ffn_kernel(x_ref, gpre_ref, gpost_ref, wupg_ref, cw_ref, cb_ref, wdown_ref, never_ref,
                x2_ref, tail_ref, carry_s, next_s, *, tile):
    i = pl.program_id(1)

    @pl.when(i == 0)
    def _():
        carry_s[...] = jnp.zeros_like(carry_s)

    @pl.when(i > 0)
    def _():
        carry_s[...] = next_s[...]

    x1 = x_ref[...]
    xn = _rms(x1, gpre_ref[...]).astype(BF16)
    never = never_ref[...] != 0
    row = lax.broadcasted_iota(jnp.int32, (tile, FF_CHUNK), 0)
    acc = jnp.zeros((tile, D_MODEL), F32)

    def up_gate(c):
        cols = slice(FF_CHUNK * c, FF_CHUNK * (c + 1))
        gcols = slice(D_FF + FF_CHUNK * c, D_FF + FF_CHUNK * (c + 1))
        return _dot(xn, wupg_ref[:, cols]), _dot(xn, wupg_ref[:, gcols])

    ahead = [up_gate(c) for c in range(min(FF_LOOKAHEAD, N_FF))]
    for c in range(N_FF):
        u, gt = ahead.pop(0)
        cols = slice(FF_CHUNK * c, FF_CHUNK * (c + 1))
        prev = carry_s[c]
        u1 = jnp.where(row == 0, prev[7:8, :], pltpu.roll(u, 1, 0))
        u2 = jnp.where(row == 0, prev[6:7, :], jnp.where(row == 1, prev[7:8, :], pltpu.roll(u, 2, 0)))
        next_s[c] = u[tile - 8:tile, :]
        tail_ref[:, cols] = u[tile - 2:tile, :]
        act = _conv_gelu_gate(u, gt, u1, u2, cw_ref[:, cols], cb_ref[:, cols])
        if c + FF_LOOKAHEAD < N_FF:
            ahead.append(up_gate(c + FF_LOOKAHEAD))
            act = _add_to_corner(act, _ordering_zero(ahead[-1][0][0:16, 0:LANES], never))
        acc = acc + _dot(act.astype(BF16), wdown_ref[cols, :])
    x2_ref[...] = x1 + _rms(acc, gpost_ref[...])


def _prompt_ffn(x1, gpre, gpost, w):
    bsz, seq, _ = x1.shape
    tile = SEQ_TILE
    assert seq % tile == 0
    row_spec = pl.BlockSpec((None, tile, D_MODEL), lambda b, i: (b, i, 0))
    consts = [gpre, gpost, w["upg"], w["cw"], w["cb"], w["down"], jnp.zeros((16, LANES), jnp.int32)]
    return pl.pallas_call(
        functools.partial(_ffn_kernel, tile=tile),
        grid=(bsz, seq // tile),
        in_specs=[row_spec] + [_const_spec(c.shape) for c in consts],
        out_specs=(row_spec, pl.BlockSpec((None, 2, D_FF), lambda b, i: (b, 0, 0))),
        out_shape=(jax.ShapeDtypeStruct((bsz, seq, D_MODEL), F32),
                   jax.ShapeDtypeStruct((bsz, 2, D_FF), F32)),
        scratch_shapes=[pltpu.VMEM((N_FF, 8, FF_CHUNK), F32), pltpu.VMEM((N_FF, 8, FF_CHUNK), F32)],
        compiler_params=pltpu.CompilerParams(
            dimension_semantics=("arbitrary", "arbitrary"), vmem_limit_bytes=VMEM_LIMIT),
        name="prompt_ffn",
    )(x1, *consts)


def _sample_proj_kernel(x_ref, g_ref, w_ref, p_ref):
    h = _rms(x_ref[...], g_ref[...]).astype(BF16)
    p_ref[...] = _dot(h, w_ref[...])


def _sample_proj(xs, gpre, w_in_b):
    rows = xs.shape[0]
    nblk = 2
    width = IN_WIDTH // nblk
    assert width % LANES == 0
    return pl.pallas_call(
        _sample_proj_kernel,
        grid=(nblk,),
        in_specs=[pl.BlockSpec((rows, D_MODEL), lambda j: (0, 0)),
                  pl.BlockSpec((1, D_MODEL), lambda j: (0, 0)),
                  pl.BlockSpec((D_MODEL, width), lambda j: (0, j))],
        out_specs=pl.BlockSpec((rows, width), lambda j: (0, j)),
        out_shape=jax.ShapeDtypeStruct((rows, IN_WIDTH), F32),
        compiler_params=pltpu.CompilerParams(dimension_semantics=("arbitrary",)),
        name="sample_proj",
    )(xs, gpre, w_in_b)


def _sample_attn_kernel(sink_ref, p_ref, cak_ref, cav_ref, cbk_ref, cbv_ref, bsa_ref, bsb_ref, never_ref,
                        oa_ref, ob_ref, kan_ref, van_ref, kbn_ref, vbn_ref, ka_s, va_s, kb_s, vb_s, *, s, la, lb):
    p = p_ref[...]
    kan_ref[...] = p[:, _C_KA:_C_KA + A_KVW]
    van_ref[...] = p[:, _C_VA:_C_VA + A_KVW]
    kbn_ref[...] = p[:, _C_KB:_C_KB + B_W]
    vbn_ref[...] = p[:, _C_VB:_C_VB + B_W]
    ka_pad = ka_s.shape[0]
    kb_pad = kb_s.shape[0]

    def fill(dst, cache_ref, new, n_cache, n_pad):
        dst[0:n_cache, :] = cache_ref[...].astype(BF16)
        dst[n_cache:n_cache + s, :] = new.astype(BF16)
        dst[n_cache + s:n_pad, :] = jnp.zeros((n_pad - n_cache - s, dst.shape[1]), BF16)

    fill(ka_s, cak_ref, p[:, _C_KA:_C_KA + A_KVW], la, ka_pad)
    fill(va_s, cav_ref, p[:, _C_VA:_C_VA + A_KVW], la, ka_pad)
    fill(kb_s, cbk_ref, p[:, _C_KB:_C_KB + B_W], lb, kb_pad)
    fill(vb_s, cbv_ref, p[:, _C_VB:_C_VB + B_W], lb, kb_pad)

    lo = lax.broadcasted_iota(jnp.int32, (s, LANES), 1) < HEAD_DIM
    top = lax.broadcasted_iota(jnp.int32, (2 * s, 1), 0) < s
    never = never_ref[...] != 0
    ka = ka_s[...]
    va = va_s[...]

    def scores(step):
        kind, j = step
        if kind == "a":
            q = p[:, _C_QA + LANES * j:_C_QA + LANES * (j + 1)] * QK_SCALE
            qr = pltpu.roll(q, HEAD_DIM, 1)
            if 2 * j < A_GROUP:
                qs = jnp.concatenate([jnp.where(lo, q, 0.0), jnp.where(lo, qr, 0.0)], axis=0)
            else:
                qs = jnp.concatenate([jnp.where(lo, 0.0, qr), jnp.where(lo, 0.0, q)], axis=0)
            bias = jnp.concatenate([bsa_ref[2 * j], bsa_ref[2 * j + 1]], axis=0)
            return _dot_nt(qs.astype(BF16), ka) + bias
        q = p[:, _C_QB + LANES * j:_C_QB + LANES * (j + 1)] * QK_SCALE
        qs = jnp.concatenate([jnp.where(lo, q, 0.0), jnp.where(lo, 0.0, q)], axis=0)
        bias = jnp.concatenate([bsb_ref[2 * j], bsb_ref[2 * j + 1]], axis=0)
        return _dot_nt(qs.astype(BF16), kb_s[:, LANES * j:LANES * (j + 1)]) + bias

    def softmax_pv(st, v, sink, order_after):
        m = jnp.max(st, axis=-1, keepdims=True)
        if sink is not None:
            m = jnp.maximum(m, sink)
        e = jnp.exp(st - m)
        l = jnp.sum(e, axis=-1, keepdims=True)
        if sink is not None:
            l = l + jnp.exp(sink - m)
        if order_after is not None:
            e = _add_to_corner(e, order_after)
        return _dot(e.astype(BF16), v) * (1.0 / l)

    steps = [("a", j) for j in range(A_HEADS // 2)] + [("b", j) for j in range(B_HEADS // 2)]
    st_next = scores(steps[0])
    for k, (kind, j) in enumerate(steps):
        st, order_after = st_next, None
        if k + 1 < len(steps):
            st_next = scores(steps[k + 1])
            order_after = _ordering_zero(st_next[0:16, 0:LANES], never)
        lanes = slice(LANES * j, LANES * (j + 1))
        if kind == "a":
            sink = jnp.where(top, sink_ref[2 * j], sink_ref[2 * j + 1])
            o = softmax_pv(st, va, sink, order_after)
            oe, oo = o[0:s], o[s:2 * s]
            if 2 * j < A_GROUP:
                oa_ref[:, lanes] = jnp.where(lo, oe, pltpu.roll(oo, HEAD_DIM, 1))
            else:
                oa_ref[:, lanes] = jnp.where(lo, pltpu.roll(oe, HEAD_DIM, 1), oo)
        else:
            o = softmax_pv(st, vb_s[:, lanes], None, order_after)
            ob_ref[:, lanes] = jnp.where(lo, o[0:s], o[s:2 * s])


def _sample_attn(sink, p, cak, cav, cbk, cbv, bsa, bsb, s):
    nb, la, _ = cak.shape
    lb = cbk.shape[1]
    ka_pad, kb_pad = bsa.shape[-1], bsb.shape[-1]
    assert s % 16 == 0 and la % 16 == 0 and lb % 16 == 0
    row = lambda w: pl.BlockSpec((s, w), lambda b: (b, 0))
    cache = lambda n, w: pl.BlockSpec((None, n, w), lambda b: (b, 0, 0))
    full = lambda a: pl.BlockSpec(a.shape, lambda b: (0,) * a.ndim)
    widths = (A_QW, B_W, A_KVW, A_KVW, B_W, B_W)
    return pl.pallas_call(
        functools.partial(_sample_attn_kernel, s=s, la=la, lb=lb),
        grid=(nb,),
        in_specs=[pl.BlockSpec(memory_space=pltpu.SMEM), row(IN_WIDTH),
                  cache(la, A_KVW), cache(la, A_KVW), cache(lb, B_W), cache(lb, B_W), full(bsa), full(bsb),
                  pl.BlockSpec((16, LANES), lambda b: (0, 0))],
        out_specs=tuple(row(w) for w in widths),
        out_shape=tuple(jax.ShapeDtypeStruct((nb * s, w), F32) for w in widths),
        scratch_shapes=[pltpu.VMEM((ka_pad, A_KVW), BF16), pltpu.VMEM((ka_pad, A_KVW), BF16),
                        pltpu.VMEM((kb_pad, B_W), BF16), pltpu.VMEM((kb_pad, B_W), BF16)],
        compiler_params=pltpu.CompilerParams(dimension_semantics=("arbitrary",)),
        name="sample_attn",
    )(sink, p, cak, cav, cbk, cbv, bsa, bsb, jnp.zeros((16, LANES), jnp.int32))


def _sample_tail_kernel(x_ref, oa_ref, ob_ref, p_ref, woa_ref, wob_ref, wout_ref,
                        gpost_ref, gpre2_ref, gpost2_ref, init_ref,
                        wup_ref, wgate_ref, cw_ref, cb_ref, wdown_ref,
                        x2_ref, u_ref, x1_s, xn_s, acc_s, *, s):
    c = pl.program_id(0)

    @pl.when(c == 0)
    def _():
        ya = _dot(oa_ref[...].astype(BF16), woa_ref[...])
        yb = _dot(ob_ref[...].astype(BF16), wob_ref[...])
        ga = p_ref[:, _C_GA:_C_GA + D_MODEL]
        gb = p_ref[:, _C_GB:_C_GB + D_MODEL]
        merged = jax.nn.sigmoid(ga) * ya + jax.nn.sigmoid(gb) * yb
        z = _dot(merged.astype(BF16), wout_ref[...])
        x1 = x_ref[...] + _rms(z, gpost_ref[...])
        x1_s[...] = x1
        xn_s[...] = _rms(x1, gpre2_ref[...]).astype(BF16)
        acc_s[...] = jnp.zeros_like(acc_s)

    xn = xn_s[...]
    u = _dot(xn, wup_ref[...])
    gt = _dot(xn, wgate_ref[...])
    u_ref[...] = u
    pos = lax.broadcasted_iota(jnp.int32, u.shape, 0) % s
    init = init_ref[...]
    u1 = jnp.where(pos == 0, pltpu.roll(init, u.shape[0] - 1, 0), pltpu.roll(u, 1, 0))
    u2 = jnp.where(pos < 2, init, pltpu.roll(u, 2, 0))
    act = _conv_gelu_gate(u, gt, u1, u2, cw_ref[...], cb_ref[...])
    acc_s[...] += _dot(act.astype(BF16), wdown_ref[...])

    @pl.when(c == pl.num_programs(0) - 1)
    def _():
        x2_ref[...] = x1_s[...] + _rms(acc_s[...], gpost2_ref[...])


def _sample_tail(xs, oa, ob, p, w, gpost, gpre2, gpost2, init, s):
    rows = xs.shape[0]
    full = lambda a: pl.BlockSpec(a.shape, lambda c: (0,) * a.ndim)
    cols = lambda r: pl.BlockSpec((r, FF_CHUNK), lambda c: (0, c))
    gate_cols = pl.BlockSpec((D_MODEL, FF_CHUNK), lambda c: (0, N_FF + c))
    down_rows = pl.BlockSpec((FF_CHUNK, D_MODEL), lambda c: (c, 0))
    args = [xs, oa, ob, p, w["oa"], w["ob"], w["out"], gpost, gpre2, gpost2]
    return pl.pallas_call(
        functools.partial(_sample_tail_kernel, s=s),
        grid=(N_FF,),
        in_specs=[full(a) for a in args] + [cols(rows), cols(D_MODEL), gate_cols, cols(3), cols(1), down_rows],
        out_specs=(full(xs), cols(rows)),
        out_shape=(jax.ShapeDtypeStruct((rows, D_MODEL), F32), jax.ShapeDtypeStruct((rows, D_FF), F32)),
        scratch_shapes=[pltpu.VMEM((rows, D_MODEL), F32), pltpu.VMEM((rows, D_MODEL), BF16),
                        pltpu.VMEM((rows, D_MODEL), F32)],
        compiler_params=pltpu.CompilerParams(dimension_semantics=("arbitrary",)),
        name="sample_tail",
    )(*args, init, w["upg"], w["upg"], w["cw"], w["cb"], w["down"])


def _prep_weights(w_in, w_oa, w_ob, w_out, w_upg, conv_w, conv_b, w_down):
    wb = w_in.astype(BF16)
    qv = jnp.concatenate([wb[:, _C_QA:_C_QA + A_QW], wb[:, _C_QB:_C_QB + B_W],
                          wb[:, _C_VA:_C_VA + A_KVW], wb[:, _C_VB:_C_VB + B_W],
                          wb[:, _C_KA:_C_KA + A_KVW]], axis=1)
    return {
        "in": wb, "qvt": qv.T,
        "oa": w_oa.astype(BF16), "ob": w_ob.astype(BF16), "out": w_out.astype(BF16),
        "upg": w_upg.astype(BF16), "cw": conv_w, "cb": conv_b.reshape(1, D_FF), "down": w_down.astype(BF16),
    }


def _round_up(n, m):
    return (n + m - 1) // m * m


def kernel(x_prompt, x_sample, cache_a_k, cache_a_v, cache_b_k, cache_b_v, state_conv,
           w_in, w_oa, w_ob, w_out, sink_a, t5_table, rel_table_b,
           g_pre_mix, g_post_mix, g_pre_ffn, g_post_ffn, w_upg, conv_w, conv_b, w_down):
    depth = w_in.shape[0]
    bsz, seq, _ = x_prompt.shape
    nb, s, _ = x_sample.shape
    la, lb = cache_a_k.shape[2], cache_b_k.shape[2]
    ka_pad = _round_up(la + s, LANES)
    kb_pad = _round_up(lb + s, LANES)

    xp = x_prompt
    xs = x_sample.reshape(nb * s, D_MODEL)
    prompt_states = [[] for _ in range(5)]
    sample_states = [[] for _ in range(5)]
    for l in range(depth):
        w = _prep_weights(w_in[l], w_oa[l], w_ob[l], w_out[l], w_upg[l], conv_w[l], conv_b[l], w_down[l])
        row = lambda g: g[l].reshape(1, D_MODEL)
        bta, btb, bsa, bsb = _build_bias(t5_table, rel_table_b[l], s, la, lb, ka_pad, kb_pad)
        sink_row = jnp.repeat(sink_a[l], GROUP).reshape(1, A_HEADS * GROUP)

        x1, ka_t, va_t, kb_t, vb_t = _prompt_mixer(xp, row(g_pre_mix), row(g_post_mix), w, sink_row, bta, btb)
        xp, conv_t = _prompt_ffn(x1, row(g_pre_ffn), row(g_post_ffn), w)
        prompt_states[0].append(ka_t.reshape(bsz, A_PREV, A_KV_HEADS, HEAD_DIM))
        prompt_states[1].append(va_t.reshape(bsz, A_PREV, A_KV_HEADS, HEAD_DIM))
        prompt_states[2].append(kb_t.reshape(bsz, B_PREV, B_HEADS, HEAD_DIM))
        prompt_states[3].append(vb_t.reshape(bsz, B_PREV, B_HEADS, HEAD_DIM))
        prompt_states[4].append(conv_t)

        p = _sample_proj(xs, row(g_pre_mix), w["in"])
        oa, ob, ka_n, va_n, kb_n, vb_n = _sample_attn(
            sink_a[l], p, cache_a_k[l].reshape(nb, la, A_KVW), cache_a_v[l].reshape(nb, la, A_KVW),
            cache_b_k[l].reshape(nb, lb, B_W), cache_b_v[l].reshape(nb, lb, B_W), bsa, bsb, s)
        init = jnp.pad(state_conv[l], ((0, 0), (0, s - 2), (0, 0))).reshape(nb * s, D_FF)
        xs, u = _sample_tail(xs, oa, ob, p, w, row(g_post_mix), row(g_pre_ffn), row(g_post_ffn), init, s)
        sample_states[0].append(ka_n.reshape(nb, s, A_KV_HEADS, HEAD_DIM))
        sample_states[1].append(va_n.reshape(nb, s, A_KV_HEADS, HEAD_DIM))
        sample_states[2].append(kb_n.reshape(nb, s, B_HEADS, HEAD_DIM))
        sample_states[3].append(vb_n.reshape(nb, s, B_HEADS, HEAD_DIM))
        sample_states[4].append(u.reshape(nb, s, D_FF)[:, -2:])

    ps = [jnp.stack(a, axis=0) for a in prompt_states]
    ss = [jnp.stack(a, axis=0) for a in sample_states]
    return (xp, xs.reshape(nb, s, D_MODEL), ps[0], ps[1], ps[2], ps[3], ps[4],
            ss[0], ss[1], ss[2], ss[3], ss[4])
```

```python
import functools
import math

import numpy as np
import jax
import jax.numpy as jnp
from jax import lax
from jax.experimental import pallas as pl
from jax.experimental.pallas import tpu as pltpu

F32 = jnp.float32
BF16 = jnp.bfloat16

D_MODEL = 1024
CHUNK = 64
HEAD_DIM = 64
EPS = 1e-6
NEG_INF = -1e30
QK_SCALE = HEAD_DIM ** -0.5

A_HEADS = 8
A_KV_HEADS = 2
A_GROUP = A_HEADS // A_KV_HEADS
A_PREV = 128
T5_BUCKETS = 32
T5_MAX_DIST = 128

B_HEADS = 8
B_PREV = 512
B_REL_CLIP = 128

A_QW = A_HEADS * HEAD_DIM
A_KVW = A_KV_HEADS * HEAD_DIM
B_W = B_HEADS * HEAD_DIM
D_FF = 3072

_C_QA = 0
_C_KA = _C_QA + A_QW
_C_VA = _C_KA + A_KVW
_C_QB = _C_VA + A_KVW
_C_KB = _C_QB + B_W
_C_VB = _C_KB + B_W
_C_GA = _C_VB + B_W
_C_GB = _C_GA + D_MODEL
IN_WIDTH = _C_GB + D_MODEL

LANES = 128
GROUP = 2 * CHUNK
A_WIN = A_PREV + GROUP
B_WIN = B_PREV + GROUP
SEQ_TILE = 512
FF_CHUNK = 512
N_FF = D_FF // FF_CHUNK
FF_LOOKAHEAD = 4
VMEM_LIMIT = 56 * 1024 * 1024
LOOKAHEAD = 1
GATE_FIRST_STEP = 1
GATE_STEP_STRIDE = 3


def _rms(x, g):
    y = x * lax.rsqrt(jnp.mean(x * x, axis=-1, keepdims=True) + EPS)
    return y * g


def _dot(a, b):
    return jnp.dot(a, b, preferred_element_type=F32)


def _dot_nt(a, b):
    return lax.dot_general(a, b, (((1,), (1,)), ((), ())), preferred_element_type=F32)


def _t5_bucket(n):
    half = T5_BUCKETS // 2
    max_exact = half // 2
    ret = jnp.where(n < 0, half, 0)
    a = jnp.abs(n)
    af = jnp.maximum(a, 1).astype(jnp.float32)
    large = max_exact + (jnp.log(af / max_exact) / math.log(T5_MAX_DIST / max_exact)
                         * (half - max_exact)).astype(jnp.int32)
    large = jnp.minimum(large, half - 1)
    return ret + jnp.where(a < max_exact, a, large)


def _window_valid(n_prev, win):
    j = np.arange(win)[:, None]
    ci = np.arange(GROUP)[None, :] // CHUNK
    return ((j >= CHUNK * ci) & (j < CHUNK * ci + n_prev + CHUNK)).astype(np.int32)


def _lookup(idx, table, lo, hi):
    acc = jnp.zeros((table.shape[0], idx.shape[1]), F32)
    for b in range(lo, hi):
        acc = jnp.where(idx == b, table[:, b:b + 1], acc)
    return acc


def _toeplitz(vec, rows, first):
    w = vec.shape[1]
    return pltpu.roll(jnp.broadcast_to(vec, (rows, w)), (w - first) % w, 1, stride=1, stride_axis=0)


def _bias_kernel(t5_ref, rel_ref, ja_ref, jb_ref, jsa_ref, jsb_ref, va_ref, vb_ref, vsa_ref, vsb_ref,
                 bta_ref, btb_ref, bsa_ref, bsb_ref, *, s, jb_range, jsb_range):
    t5 = t5_ref[...]
    rel = rel_ref[...]

    fa = _lookup(ja_ref[...], t5, 0, T5_BUCKETS)
    va = va_ref[...] != 0
    for h in range(A_HEADS):
        t = _toeplitz(fa[h:h + 1, :], A_WIN, A_WIN - 1)[:, 0:GROUP]
        bta_ref[:, GROUP * h:GROUP * (h + 1)] = jnp.where(va, t, NEG_INF)

    fb = _lookup(jb_ref[...], rel, *jb_range)
    for r in range(B_WIN // LANES):
        base = B_WIN - LANES * r
        vb = vb_ref[LANES * r:LANES * (r + 1), :] != 0
        for h in range(B_HEADS):
            seg = fb[h:h + 1, base - LANES:base + LANES]
            t = _toeplitz(seg, LANES, LANES)[:, 0:GROUP]
            btb_ref[h, LANES * r:LANES * (r + 1), :] = jnp.where(vb, t, NEG_INF)

    fsa = _lookup(jsa_ref[...], t5, 0, T5_BUCKETS)
    fsb = _lookup(jsb_ref[...], rel, *jsb_range)
    vsa = vsa_ref[...] != 0
    vsb = vsb_ref[...] != 0
    for h in range(A_HEADS):
        bsa_ref[h] = jnp.where(vsa, _toeplitz(fsa[h:h + 1, :], s, s - 1), NEG_INF)
    for h in range(B_HEADS):
        bsb_ref[h] = jnp.where(vsb, _toeplitz(fsb[h:h + 1, :], s, s - 1), NEG_INF)


def _build_bias(t5_table, rel_table, s, la, lb, ka_pad, kb_pad):
    wa = 3 * LANES
    assert A_WIN - 1 + GROUP <= wa and la + 2 * s - 2 < ka_pad and lb + 2 * s - 2 < kb_pad
    bucket = lambda n: (_t5_bucket(n).astype(jnp.int32) & (T5_BUCKETS - 1)).reshape(1, -1)
    ja = bucket(jnp.arange(wa) - (A_WIN - 1) + A_PREV)
    jb = np.clip(np.arange(B_WIN + GROUP) - B_WIN + B_PREV, -B_REL_CLIP, B_REL_CLIP) + B_REL_CLIP
    jsa = bucket(la + s - 1 - jnp.arange(ka_pad))
    jsb = np.clip(lb + s - 1 - np.arange(kb_pad), -B_REL_CLIP, B_REL_CLIP) + B_REL_CLIP
    key = lambda n: np.broadcast_to(np.arange(n)[None, :], (s, n))
    vsa = (key(ka_pad) < la + s).astype(np.int32)
    vsb = (key(kb_pad) < lb + s).astype(np.int32)
    as_row = lambda v: jnp.asarray(v.astype(np.int32).reshape(1, -1))
    rng = lambda v: (int(v.min()), int(v.max()) + 1)

    vmem = pl.BlockSpec(memory_space=pltpu.VMEM)
    return pl.pallas_call(
        functools.partial(_bias_kernel, s=s, jb_range=rng(jb), jsb_range=rng(jsb)),
        out_shape=(
            jax.ShapeDtypeStruct((A_WIN, A_HEADS * GROUP), F32),
            jax.ShapeDtypeStruct((B_HEADS, B_WIN, GROUP), F32),
            jax.ShapeDtypeStruct((A_HEADS, s, ka_pad), F32),
            jax.ShapeDtypeStruct((B_HEADS, s, kb_pad), F32),
        ),
        in_specs=[vmem] * 10,
        out_specs=(vmem,) * 4,
        name="bias_tables",
    )(t5_table, rel_table, ja, as_row(jb), jsa, as_row(jsb),
      jnp.asarray(_window_valid(A_PREV, A_WIN)), jnp.asarray(_window_valid(B_PREV, B_WIN)),
      jnp.asarray(vsa), jnp.asarray(vsb))


def _ordering_zero(x, never):
    return jnp.where(never, x, 0.0)


def _add_to_corner(x, z, cols=(0,)):
    pieces, at = [], 0
    for c in cols:
        pieces += [x[0:16, at:c], x[0:16, c:c + LANES] + z]
        at = c + LANES
    pieces.append(x[0:16, at:])
    head = jnp.concatenate([q for q in pieces if q.shape[1]], axis=1)
    return jnp.concatenate([head, x[16:, :]], axis=0)


def _scores_a(g, ka_s, qt_s):
    tok = slice(GROUP * g, GROUP * (g + 1))
    kwin = ka_s[GROUP * g:GROUP * g + A_WIN, :]
    q = qt_s[0:A_QW, tok]
    zero = jnp.zeros((HEAD_DIM, GROUP), BF16)
    cols = []
    for i in range(A_HEADS):
        qi = q[HEAD_DIM * i:HEAD_DIM * (i + 1), :]
        cols.append(jnp.concatenate([qi, zero] if i < A_GROUP else [zero, qi], axis=0))
    return _dot(kwin, jnp.concatenate(cols, axis=1))


def _scores_b(g, pp, kb_s, qt_s):
    tok = slice(GROUP * g, GROUP * (g + 1))
    kwin = kb_s[GROUP * g:GROUP * g + B_WIN, 2 * LANES * pp:2 * LANES * (pp + 1)]
    qrow = A_QW + 2 * LANES * pp
    zero = jnp.zeros((HEAD_DIM, GROUP), BF16)
    cols = []
    for i in range(4):
        qi = qt_s[qrow + HEAD_DIM * i:qrow + HEAD_DIM * (i + 1), tok]
        cols.append(jnp.concatenate([zero] * i + [qi] + [zero] * (3 - i), axis=0))
    return _dot(kwin, jnp.concatenate(cols, axis=1))


def _softmax_keys(st, bias, sink_row, order_after, order_cols=(0,)):
    st = st + bias
    m = jnp.max(st, axis=0, keepdims=True)
    if sink_row is not None:
        m = jnp.maximum(m, sink_row)
    p = jnp.exp(st - m)
    if order_after is not None:
        p = _add_to_corner(p, order_after, order_cols)
    return p.astype(BF16), m


def _weighted_values(vwin, present, p, m, sink_row):
    ot = _dot(jnp.concatenate([vwin, present], axis=0), p)
    l = ot[2 * HEAD_DIM:2 * HEAD_DIM + 1, :]
    if sink_row is not None:
        l = l + jnp.exp(sink_row - m)
    return ot[0:2 * HEAD_DIM, :] * (1.0 / l)


def _values_a(g, vat_s, present_s, p, m, sink_row, oa_s):
    tok = slice(GROUP * g, GROUP * (g + 1))
    c0 = B_PREV - A_PREV + GROUP * g
    ot = _weighted_values(vat_s[:, GROUP * g:GROUP * g + A_WIN], present_s[:, c0:c0 + A_WIN], p, m, sink_row)
    for j in range(A_HEADS // 2):
        parts = []
        for i in (2 * j, 2 * j + 1):
            rows = slice(0, HEAD_DIM) if i < A_GROUP else slice(HEAD_DIM, 2 * HEAD_DIM)
            parts.append(ot[rows, GROUP * i:GROUP * (i + 1)])
        blk = jnp.concatenate(parts, axis=0)
        oa_s[tok, LANES * j:LANES * (j + 1)] = blk.T.astype(BF16)


def _values_b(g, p2, vbt_s, present_s, p, m, ob_s):
    tok = slice(GROUP * g, GROUP * (g + 1))
    keys = slice(GROUP * g, GROUP * g + B_WIN)
    lanes = slice(LANES * p2, LANES * (p2 + 1))
    ot = _weighted_values(vbt_s[lanes, keys], present_s[:, keys], p, m, None)
    blk = jnp.concatenate([ot[0:HEAD_DIM, 0:GROUP], ot[HEAD_DIM:2 * HEAD_DIM, GROUP:2 * GROUP]], axis=0)
    ob_s[tok, lanes] = blk.T.astype(BF16)


def _attend_tile(n_groups, never, ka_s, vat_s, kb_s, vbt_s, present_s, qt_s, bta_ref, btb_ref, sink_row, oa_s, ob_s,
                 fillers):
    steps = []
    for g in range(n_groups):
        steps.append(("a", g, 0))
        steps.extend(("b", g, pp) for pp in range(B_HEADS // 4))

    def scores(step):
        kind, g, pp = step
        return _scores_a(g, ka_s, qt_s) if kind == "a" else _scores_b(g, pp, kb_s, qt_s)

    ahead = [scores(step) for step in steps[:LOOKAHEAD]]
    for k, (kind, g, pp) in enumerate(steps):
        st, order_after = ahead.pop(0), None
        if k + LOOKAHEAD < len(steps):
            ahead.append(scores(steps[k + LOOKAHEAD]))
            order_after = _ordering_zero(ahead[-1][0:16, 0:LANES], never)
        if fillers.get(k) is not None:
            z = _ordering_zero(fillers[k](), never)
            order_after = z if order_after is None else order_after + z
        if kind == "a":
            p, m = _softmax_keys(st, bta_ref[...], sink_row, order_after)
            _values_a(g, vat_s, present_s, p, m, sink_row, oa_s)
        else:
            bias = jnp.concatenate([btb_ref[4 * pp + i] for i in range(4)], axis=1)
            p, m = _softmax_keys(st, bias, None, order_after, (0, 2 * GROUP))
            for j in range(2):
                cols = slice(2 * GROUP * j, 2 * GROUP * (j + 1))
                _values_b(g, 2 * pp + j, vbt_s, present_s, p[:, cols], m[:, cols], ob_s)


def _mixer_kernel(x_ref, gpre_ref, gpost_ref, win_ref, wt_ref,
                  woa_ref, wob_ref, wout_ref, sink_ref, bta_ref, btb_ref, never_ref, wupg_ref, wdown_ref,
                  x1_ref, kat_ref, vat_ref, kbt_ref, vbt_ref, wupg_b_ref, wdown_b_ref,
                  h_s, qt_s, ka_s, vat_s, kb_s, vbt_s, present_s, oa_s, ob_s, mg_s, sg_s,
                  ka_f, va_f, kb_f, vb_f, *, tile, n_tiles):
    i = pl.program_id(1)

    wupg_b_ref[...] = wupg_ref[...].astype(BF16)
    wdown_b_ref[...] = wdown_ref[...].astype(BF16)

    @pl.when(i == 0)
    def _():
        ka_s[0:A_PREV, :] = jnp.zeros((A_PREV, A_KVW), BF16)
        vat_s[:, 0:A_PREV] = jnp.zeros((A_KVW, A_PREV), BF16)
        kb_s[0:B_PREV, :] = jnp.zeros((B_PREV, B_W), BF16)
        vbt_s[:, 0:B_PREV] = jnp.zeros((B_W, B_PREV), BF16)
        present_s[:, 0:B_PREV] = jnp.zeros((16, B_PREV), BF16)

    @pl.when(i > 0)
    def _():
        ka_s[0:A_PREV, :] = ka_s[tile:tile + A_PREV, :]
        vat_s[:, 0:A_PREV] = vat_s[:, tile:tile + A_PREV]
        kb_s[0:B_PREV, :] = kb_s[tile:tile + B_PREV, :]
        vbt_s[:, 0:B_PREV] = vbt_s[:, tile:tile + B_PREV]
        present_s[:, 0:B_PREV] = present_s[:, tile:tile + B_PREV]

    x = x_ref[...]
    h = _rms(x, gpre_ref[...]).astype(BF16)
    h_s[...] = h

    kb = _dot(h, win_ref[:, _C_KB:_C_KB + B_W])
    kb_s[B_PREV:B_PREV + tile, :] = kb.astype(BF16)
    nq = A_QW + B_W
    pt = _dot_nt(wt_ref[...], h)
    qt_s[...] = (pt[0:nq, :] * QK_SCALE).astype(BF16)
    vat = pt[nq:nq + A_KVW, :]
    vbt = pt[nq + A_KVW:nq + A_KVW + B_W, :]
    ka = pt[nq + A_KVW + B_W:, :].T
    ka_s[A_PREV:A_PREV + tile, :] = ka.astype(BF16)
    vat_s[:, A_PREV:A_PREV + tile] = vat.astype(BF16)
    vbt_s[:, B_PREV:B_PREV + tile] = vbt.astype(BF16)
    first_row = lax.broadcasted_iota(jnp.int32, (16, tile), 0) == 0
    present_s[:, B_PREV:B_PREV + tile] = jnp.where(first_row, 1.0, 0.0).astype(BF16)
    ka_f[...] = ka[tile - A_PREV:, :]
    kb_f[...] = kb[tile - B_PREV:, :]
    va_f[...] = vat[:, tile - A_PREV:]
    vb_f[...] = vbt[:, tile - B_PREV:]

    gate_w = 512
    n_steps = (tile // GROUP) * (1 + B_HEADS // 4)

    def gate_piece(j):
        def run():
            g = _dot(h_s[...], win_ref[:, _C_GA + gate_w * j:_C_GA + gate_w * (j + 1)])
            sg_s[:, gate_w * j:gate_w * (j + 1)] = jax.nn.sigmoid(g)
            return g[0:16, 0:LANES]
        return run

    n_pieces = 2 * D_MODEL // gate_w
    fillers = {GATE_FIRST_STEP + GATE_STEP_STRIDE * j: gate_piece(j) for j in range(n_pieces)}
    assert max(fillers) < n_steps
    _attend_tile(tile // GROUP, never_ref[...] != 0, ka_s, vat_s, kb_s, vbt_s, present_s, qt_s,
                 bta_ref, btb_ref, sink_ref[...], oa_s, ob_s, fillers)

    oa = oa_s[...]
    ob = ob_s[...]
    cw = 512
    for c in range(D_MODEL // cw):
        cs = slice(cw * c, cw * (c + 1))
        ya = _dot(oa, woa_ref[:, cs])
        yb = _dot(ob, wob_ref[:, cs])
        sgb = sg_s[:, D_MODEL + cw * c:D_MODEL + cw * (c + 1)]
        mg_s[:, cs] = (sg_s[:, cs] * ya + sgb * yb).astype(BF16)
    z = _dot(mg_s[...], wout_ref[...])
    x1_ref[...] = x_ref[...] + _rms(z, gpost_ref[...])

    @pl.when(i == n_tiles - 1)
    def _():
        kat_ref[...] = ka_f[...]
        kbt_ref[...] = kb_f[...]
        vat_ref[...] = va_f[...].T
        vbt_ref[...] = vb_f[...].T


def _const_spec(shape):
    nd = len(shape)
    return pl.BlockSpec(shape, lambda *_: (0,) * nd, pipeline_mode=pl.Buffered(1))


def _prompt_mixer(x, gpre, gpost, w, sink_row, bta, btb, w_upg, w_down):
    bsz, seq, _ = x.shape
    tile = SEQ_TILE
    assert seq % tile == 0 and tile % GROUP == 0 and tile >= B_PREV
    n_tiles = seq // tile
    steps = bsz * n_tiles
    up_rows, down_rows = w_upg.shape[0] // steps, w_down.shape[0] // steps
    assert up_rows * steps == w_upg.shape[0] and down_rows * steps == w_down.shape[0]
    assert up_rows % 16 == 0 and down_rows % 16 == 0
    step_rows = lambda rows, width: pl.BlockSpec((rows, width), lambda b, i: (b * n_tiles + i, 0))
    row_spec = pl.BlockSpec((None, tile, D_MODEL), lambda b, i: (b, i, 0))

    def tail_spec(rows, width):
        return pl.BlockSpec((None, rows, width), lambda b, i: (b, 0, 0))

    consts = [gpre, gpost, w["in"], w["qvt"],
              w["oa"], w["ob"], w["out"], sink_row, bta, btb, jnp.zeros((16, LANES), jnp.int32)]
    return pl.pallas_call(
        functools.partial(_mixer_kernel, tile=tile, n_tiles=n_tiles),
        grid=(bsz, n_tiles),
        in_specs=[row_spec] + [_const_spec(c.shape) for c in consts]
        + [step_rows(up_rows, w_upg.shape[1]), step_rows(down_rows, w_down.shape[1])],
        out_specs=(row_spec, tail_spec(A_PREV, A_KVW), tail_spec(A_PREV, A_KVW),
                   tail_spec(B_PREV, B_W), tail_spec(B_PREV, B_W),
                   step_rows(up_rows, w_upg.shape[1]), step_rows(down_rows, w_down.shape[1])),
        out_shape=(
            jax.ShapeDtypeStruct((bsz, seq, D_MODEL), F32),
            jax.ShapeDtypeStruct((bsz, A_PREV, A_KVW), F32),
            jax.ShapeDtypeStruct((bsz, A_PREV, A_KVW), F32),
            jax.ShapeDtypeStruct((bsz, B_PREV, B_W), F32),
            jax.ShapeDtypeStruct((bsz, B_PREV, B_W), F32),
            jax.ShapeDtypeStruct(w_upg.shape, BF16),
            jax.ShapeDtypeStruct(w_down.shape, BF16),
        ),
        scratch_shapes=[
            pltpu.VMEM((tile, D_MODEL), BF16),
            pltpu.VMEM((A_QW + B_W, tile), BF16),
            pltpu.VMEM((A_PREV + tile, A_KVW), BF16),
            pltpu.VMEM((A_KVW, A_PREV + tile), BF16),
            pltpu.VMEM((B_PREV + tile, B_W), BF16),
            pltpu.VMEM((B_W, B_PREV + tile), BF16),
            pltpu.VMEM((16, B_PREV + tile), BF16),
            pltpu.VMEM((tile, A_QW), BF16),
            pltpu.VMEM((tile, B_W), BF16),
            pltpu.VMEM((tile, D_MODEL), BF16),
            pltpu.VMEM((tile, 2 * D_MODEL), F32),
            pltpu.VMEM((A_PREV, A_KVW), F32),
            pltpu.VMEM((A_KVW, A_PREV), F32),
            pltpu.VMEM((B_PREV, B_W), F32),
            pltpu.VMEM((B_W, B_PREV), F32),
        ],
        compiler_params=pltpu.CompilerParams(
            dimension_semantics=("arbitrary", "arbitrary"), vmem_limit_bytes=VMEM_LIMIT),
        name="prompt_mixer",
    )(x, *consts, w_upg, w_down)


def _conv_gelu_gate(u, gt, u1, u2, cw, cb):
    c = cb + cw[0:1, :] * u2
    c = c + cw[1:2, :] * u1
    c = c + cw[2:3, :] * u
    return jax.nn.gelu(c, approximate=True) * gt


def _ffn_kernel(x_ref, gpre_ref, gpost_ref, wupg_ref, cw_ref, cb_ref, wdown_ref, never_ref,
                x2_ref, tail_ref, carry_s, next_s, *, tile):
    i = pl.program_id(1)

    @pl.when(i == 0)
    def _():
        carry_s[...] = jnp.zeros_like(carry_s)

    @pl.when(i > 0)
    def _():
        carry_s[...] = next_s[...]

    x1 = x_ref[...]
    xn = _rms(x1, gpre_ref[...]).astype(BF16)
    never = never_ref[...] != 0
    row = lax.broadcasted_iota(jnp.int32, (tile, FF_CHUNK), 0)
    acc = jnp.zeros((tile, D_MODEL), F32)

    def up_gate(c):
        cols = slice(FF_CHUNK * c, FF_CHUNK * (c + 1))
        gcols = slice(D_FF + FF_CHUNK * c, D_FF + FF_CHUNK * (c + 1))
        return _dot(xn, wupg_ref[:, cols]), _dot(xn, wupg_ref[:, gcols])

    ahead = [up_gate(c) for c in range(min(FF_LOOKAHEAD, N_FF))]
    for c in range(N_FF):
        u, gt = ahead.pop(0)
        cols = slice(FF_CHUNK * c, FF_CHUNK * (c + 1))
        prev = carry_s[c]
        u1 = jnp.where(row == 0, prev[7:8, :], pltpu.roll(u, 1, 0))
        u2 = jnp.where(row == 0, prev[6:7, :], jnp.where(row == 1, prev[7:8, :], pltpu.roll(u, 2, 0)))
        next_s[c] = u[tile - 8:tile, :]
        tail_ref[:, cols] = u[tile - 2:tile, :]
        act = _conv_gelu_gate(u, gt, u1, u2, cw_ref[:, cols], cb_ref[:, cols])
        if c + FF_LOOKAHEAD < N_FF:
            ahead.append(up_gate(c + FF_LOOKAHEAD))
            act = _add_to_corner(act, _ordering_zero(ahead[-1][0][0:16, 0:LANES], never))
        acc = acc + _dot(act.astype(BF16), wdown_ref[cols, :])
    x2_ref[...] = x1 + _rms(acc, gpost_ref[...])


def _prompt_ffn(x1, gpre, gpost, w):
    bsz, seq, _ = x1.shape
    tile = SEQ_TILE
    assert seq % tile == 0
    row_spec = pl.BlockSpec((None, tile, D_MODEL), lambda b, i: (b, i, 0))
    consts = [gpre, gpost, w["upg"], w["cw"], w["cb"], w["down"], jnp.zeros((16, LANES), jnp.int32)]
    return pl.pallas_call(
        functools.partial(_ffn_kernel, tile=tile),
        grid=(bsz, seq // tile),
        in_specs=[row_spec] + [_const_spec(c.shape) for c in consts],
        out_specs=(row_spec, pl.BlockSpec((None, 2, D_FF), lambda b, i: (b, 0, 0))),
        out_shape=(jax.ShapeDtypeStruct((bsz, seq, D_MODEL), F32),
                   jax.ShapeDtypeStruct((bsz, 2, D_FF), F32)),
        scratch_shapes=[pltpu.VMEM((N_FF, 8, FF_CHUNK), F32), pltpu.VMEM((N_FF, 8, FF_CHUNK), F32)],
        compiler_params=pltpu.CompilerParams(
            dimension_semantics=("arbitrary", "arbitrary"), vmem_limit_bytes=VMEM_LIMIT),
        name="prompt_ffn",
    )(x1, *consts)


def _sample_proj_kernel(x_ref, g_ref, w_ref, p_ref):
    h = _rms(x_ref[...], g_ref[...]).astype(BF16)
    p_ref[...] = _dot(h, w_ref[...])


def _sample_proj(xs, gpre, w_in_b):
    rows = xs.shape[0]
    nblk = 2
    width = IN_WIDTH // nblk
    assert width % LANES == 0
    return pl.pallas_call(
        _sample_proj_kernel,
        grid=(nblk,),
        in_specs=[pl.BlockSpec((rows, D_MODEL), lambda j: (0, 0)),
                  pl.BlockSpec((1, D_MODEL), lambda j: (0, 0)),
                  pl.BlockSpec((D_MODEL, width), lambda j: (0, j))],
        out_specs=pl.BlockSpec((rows, width), lambda j: (0, j)),
        out_shape=jax.ShapeDtypeStruct((rows, IN_WIDTH), F32),
        compiler_params=pltpu.CompilerParams(dimension_semantics=("arbitrary",)),
        name="sample_proj",
    )(xs, gpre, w_in_b)


def _sample_attn_kernel(sink_ref, p_ref, cak_ref, cav_ref, cbk_ref, cbv_ref, bsa_ref, bsb_ref, never_ref,
                        oa_ref, ob_ref, kan_ref, van_ref, kbn_ref, vbn_ref, ka_s, va_s, kb_s, vb_s, *, s, la, lb):
    p = p_ref[...]
    kan_ref[...] = p[:, _C_KA:_C_KA + A_KVW]
    van_ref[...] = p[:, _C_VA:_C_VA + A_KVW]
    kbn_ref[...] = p[:, _C_KB:_C_KB + B_W]
    vbn_ref[...] = p[:, _C_VB:_C_VB + B_W]
    ka_pad = ka_s.shape[0]
    kb_pad = kb_s.shape[0]

    def fill(dst, cache_ref, new, n_cache, n_pad):
        dst[0:n_cache, :] = cache_ref[...].astype(BF16)
        dst[n_cache:n_cache + s, :] = new.astype(BF16)
        dst[n_cache + s:n_pad, :] = jnp.zeros((n_pad - n_cache - s, dst.shape[1]), BF16)

    fill(ka_s, cak_ref, p[:, _C_KA:_C_KA + A_KVW], la, ka_pad)
    fill(va_s, cav_ref, p[:, _C_VA:_C_VA + A_KVW], la, ka_pad)
    fill(kb_s, cbk_ref, p[:, _C_KB:_C_KB + B_W], lb, kb_pad)
    fill(vb_s, cbv_ref, p[:, _C_VB:_C_VB + B_W], lb, kb_pad)

    lo = lax.broadcasted_iota(jnp.int32, (s, LANES), 1) < HEAD_DIM
    top = lax.broadcasted_iota(jnp.int32, (2 * s, 1), 0) < s
    never = never_ref[...] != 0
    ka = ka_s[...]
    va = va_s[...]

    def scores(step):
        kind, j = step
        if kind == "a":
            q = p[:, _C_QA + LANES * j:_C_QA + LANES * (j + 1)] * QK_SCALE
            qr = pltpu.roll(q, HEAD_DIM, 1)
            if 2 * j < A_GROUP:
                qs = jnp.concatenate([jnp.where(lo, q, 0.0), jnp.where(lo, qr, 0.0)], axis=0)
            else:
                qs = jnp.concatenate([jnp.where(lo, 0.0, qr), jnp.where(lo, 0.0, q)], axis=0)
            bias = jnp.concatenate([bsa_ref[2 * j], bsa_ref[2 * j + 1]], axis=0)
            return _dot_nt(qs.astype(BF16), ka) + bias
        q = p[:, _C_QB + LANES * j:_C_QB + LANES * (j + 1)] * QK_SCALE
        qs = jnp.concatenate([jnp.where(lo, q, 0.0), jnp.where(lo, 0.0, q)], axis=0)
        bias = jnp.concatenate([bsb_ref[2 * j], bsb_ref[2 * j + 1]], axis=0)
        return _dot_nt(qs.astype(BF16), kb_s[:, LANES * j:LANES * (j + 1)]) + bias

    def softmax_pv(st, v, sink, order_after):
        m = jnp.max(st, axis=-1, keepdims=True)
        if sink is not None:
            m = jnp.maximum(m, sink)
        e = jnp.exp(st - m)
        l = jnp.sum(e, axis=-1, keepdims=True)
        if sink is not None:
            l = l + jnp.exp(sink - m)
        if order_after is not None:
            e = _add_to_corner(e, order_after)
        return _dot(e.astype(BF16), v) * (1.0 / l)

    steps = [("a", j) for j in range(A_HEADS // 2)] + [("b", j) for j in range(B_HEADS // 2)]
    st_next = scores(steps[0])
    for k, (kind, j) in enumerate(steps):
        st, order_after = st_next, None
        if k + 1 < len(steps):
            st_next = scores(steps[k + 1])
            order_after = _ordering_zero(st_next[0:16, 0:LANES], never)
        lanes = slice(LANES * j, LANES * (j + 1))
        if kind == "a":
            sink = jnp.where(top, sink_ref[2 * j], sink_ref[2 * j + 1])
            o = softmax_pv(st, va, sink, order_after)
            oe, oo = o[0:s], o[s:2 * s]
            if 2 * j < A_GROUP:
                oa_ref[:, lanes] = jnp.where(lo, oe, pltpu.roll(oo, HEAD_DIM, 1))
            else:
                oa_ref[:, lanes] = jnp.where(lo, pltpu.roll(oe, HEAD_DIM, 1), oo)
        else:
            o = softmax_pv(st, vb_s[:, lanes], None, order_after)
            ob_ref[:, lanes] = jnp.where(lo, o[0:s], o[s:2 * s])


def _sample_attn(sink, p, cak, cav, cbk, cbv, bsa, bsb, s):
    nb, la, _ = cak.shape
    lb = cbk.shape[1]
    ka_pad, kb_pad = bsa.shape[-1], bsb.shape[-1]
    assert s % 16 == 0 and la % 16 == 0 and lb % 16 == 0
    row = lambda w: pl.BlockSpec((s, w), lambda b: (b, 0))
    cache = lambda n, w: pl.BlockSpec((None, n, w), lambda b: (b, 0, 0))
    full = lambda a: pl.BlockSpec(a.shape, lambda b: (0,) * a.ndim)
    widths = (A_QW, B_W, A_KVW, A_KVW, B_W, B_W)
    return pl.pallas_call(
        functools.partial(_sample_attn_kernel, s=s, la=la, lb=lb),
        grid=(nb,),
        in_specs=[pl.BlockSpec(memory_space=pltpu.SMEM), row(IN_WIDTH),
                  cache(la, A_KVW), cache(la, A_KVW), cache(lb, B_W), cache(lb, B_W), full(bsa), full(bsb),
                  pl.BlockSpec((16, LANES), lambda b: (0, 0))],
        out_specs=tuple(row(w) for w in widths),
        out_shape=tuple(jax.ShapeDtypeStruct((nb * s, w), F32) for w in widths),
        scratch_shapes=[pltpu.VMEM((ka_pad, A_KVW), BF16), pltpu.VMEM((ka_pad, A_KVW), BF16),
                        pltpu.VMEM((kb_pad, B_W), BF16), pltpu.VMEM((kb_pad, B_W), BF16)],
        compiler_params=pltpu.CompilerParams(dimension_semantics=("arbitrary",)),
        name="sample_attn",
    )(sink, p, cak, cav, cbk, cbv, bsa, bsb, jnp.zeros((16, LANES), jnp.int32))


def _sample_tail_kernel(x_ref, oa_ref, ob_ref, p_ref, woa_ref, wob_ref, wout_ref,
                        gpost_ref, gpre2_ref, gpost2_ref, init_ref,
                        wup_ref, wgate_ref, cw_ref, cb_ref, wdown_ref,
                        x2_ref, u_ref, x1_s, xn_s, acc_s, *, s):
    c = pl.program_id(0)

    @pl.when(c == 0)
    def _():
        ya = _dot(oa_ref[...].astype(BF16), woa_ref[...])
        yb = _dot(ob_ref[...].astype(BF16), wob_ref[...])
        ga = p_ref[:, _C_GA:_C_GA + D_MODEL]
        gb = p_ref[:, _C_GB:_C_GB + D_MODEL]
        merged = jax.nn.sigmoid(ga) * ya + jax.nn.sigmoid(gb) * yb
        z = _dot(merged.astype(BF16), wout_ref[...])
        x1 = x_ref[...] + _rms(z, gpost_ref[...])
        x1_s[...] = x1
        xn_s[...] = _rms(x1, gpre2_ref[...]).astype(BF16)
        acc_s[...] = jnp.zeros_like(acc_s)

    xn = xn_s[...]
    u = _dot(xn, wup_ref[...])
    gt = _dot(xn, wgate_ref[...])
    u_ref[...] = u
    pos = lax.broadcasted_iota(jnp.int32, u.shape, 0) % s
    init = init_ref[...]
    u1 = jnp.where(pos == 0, pltpu.roll(init, u.shape[0] - 1, 0), pltpu.roll(u, 1, 0))
    u2 = jnp.where(pos < 2, init, pltpu.roll(u, 2, 0))
    act = _conv_gelu_gate(u, gt, u1, u2, cw_ref[...], cb_ref[...])
    acc_s[...] += _dot(act.astype(BF16), wdown_ref[...])

    @pl.when(c == pl.num_programs(0) - 1)
    def _():
        x2_ref[...] = x1_s[...] + _rms(acc_s[...], gpost2_ref[...])


def _sample_tail(xs, oa, ob, p, w, gpost, gpre2, gpost2, init, s):
    rows = xs.shape[0]
    full = lambda a: pl.BlockSpec(a.shape, lambda c: (0,) * a.ndim)
    cols = lambda r: pl.BlockSpec((r, FF_CHUNK), lambda c: (0, c))
    gate_cols = pl.BlockSpec((D_MODEL, FF_CHUNK), lambda c: (0, N_FF + c))
    down_rows = pl.BlockSpec((FF_CHUNK, D_MODEL), lambda c: (c, 0))
    args = [xs, oa, ob, p, w["oa"], w["ob"], w["out"], gpost, gpre2, gpost2]
    return pl.pallas_call(
        functools.partial(_sample_tail_kernel, s=s),
        grid=(N_FF,),
        in_specs=[full(a) for a in args] + [cols(rows), cols(D_MODEL), gate_cols, cols(3), cols(1), down_rows],
        out_specs=(full(xs), cols(rows)),
        out_shape=(jax.ShapeDtypeStruct((rows, D_MODEL), F32), jax.ShapeDtypeStruct((rows, D_FF), F32)),
        scratch_shapes=[pltpu.VMEM((rows, D_MODEL), F32), pltpu.VMEM((rows, D_MODEL), BF16),
                        pltpu.VMEM((rows, D_MODEL), F32)],
        compiler_params=pltpu.CompilerParams(dimension_semantics=("arbitrary",)),
        name="sample_tail",
    )(*args, init, w["upg"], w["upg"], w["cw"], w["cb"], w["down"])


def _prep_weights(w_in, w_oa, w_ob, w_out, conv_w, conv_b):
    wb = w_in.astype(BF16)
    qv = jnp.concatenate([wb[:, _C_QA:_C_QA + A_QW], wb[:, _C_QB:_C_QB + B_W],
                          wb[:, _C_VA:_C_VA + A_KVW], wb[:, _C_VB:_C_VB + B_W],
                          wb[:, _C_KA:_C_KA + A_KVW]], axis=1)
    return {
        "in": wb, "qvt": qv.T,
        "oa": w_oa.astype(BF16), "ob": w_ob.astype(BF16), "out": w_out.astype(BF16),
        "cw": conv_w, "cb": conv_b.reshape(1, D_FF),
    }


def _round_up(n, m):
    return (n + m - 1) // m * m


def kernel(x_prompt, x_sample, cache_a_k, cache_a_v, cache_b_k, cache_b_v, state_conv,
           w_in, w_oa, w_ob, w_out, sink_a, t5_table, rel_table_b,
           g_pre_mix, g_post_mix, g_pre_ffn, g_post_ffn, w_upg, conv_w, conv_b, w_down):
    depth = w_in.shape[0]
    bsz, seq, _ = x_prompt.shape
    nb, s, _ = x_sample.shape
    la, lb = cache_a_k.shape[2], cache_b_k.shape[2]
    ka_pad = _round_up(la + s, LANES)
    kb_pad = _round_up(lb + s, LANES)

    xp = x_prompt
    xs = x_sample.reshape(nb * s, D_MODEL)
    prompt_states = [[] for _ in range(5)]
    sample_states = [[] for _ in range(5)]
    for l in range(depth):
        w = _prep_weights(w_in[l], w_oa[l], w_ob[l], w_out[l], conv_w[l], conv_b[l])
        row = lambda g: g[l].reshape(1, D_MODEL)
        bta, btb, bsa, bsb = _build_bias(t5_table, rel_table_b[l], s, la, lb, ka_pad, kb_pad)
        sink_row = jnp.repeat(sink_a[l], GROUP).reshape(1, A_HEADS * GROUP)

        x1, ka_t, va_t, kb_t, vb_t, w["upg"], w["down"] = _prompt_mixer(
            xp, row(g_pre_mix), row(g_post_mix), w, sink_row, bta, btb, w_upg[l], w_down[l])
        xp, conv_t = _prompt_ffn(x1, row(g_pre_ffn), row(g_post_ffn), w)
        prompt_states[0].append(ka_t.reshape(bsz, A_PREV, A_KV_HEADS, HEAD_DIM))
        prompt_states[1].append(va_t.reshape(bsz, A_PREV, A_KV_HEADS, HEAD_DIM))
        prompt_states[2].append(kb_t.reshape(bsz, B_PREV, B_HEADS, HEAD_DIM))
        prompt_states[3].append(vb_t.reshape(bsz, B_PREV, B_HEADS, HEAD_DIM))
        prompt_states[4].append(conv_t)

        p = _sample_proj(xs, row(g_pre_mix), w["in"])
        oa, ob, ka_n, va_n, kb_n, vb_n = _sample_attn(
            sink_a[l], p, cache_a_k[l].reshape(nb, la, A_KVW), cache_a_v[l].reshape(nb, la, A_KVW),
            cache_b_k[l].reshape(nb, lb, B_W), cache_b_v[l].reshape(nb, lb, B_W), bsa, bsb, s)
        init = jnp.pad(state_conv[l], ((0, 0), (0, s - 2), (0, 0))).reshape(nb * s, D_FF)
        xs, u = _sample_tail(xs, oa, ob, p, w, row(g_post_mix), row(g_pre_ffn), row(g_post_ffn), init, s)
        sample_states[0].append(ka_n.reshape(nb, s, A_KV_HEADS, HEAD_DIM))
        sample_states[1].append(va_n.reshape(nb, s, A_KV_HEADS, HEAD_DIM))
        sample_states[2].append(kb_n.reshape(nb, s, B_HEADS, HEAD_DIM))
        sample_states[3].append(vb_n.reshape(nb, s, B_HEADS, HEAD_DIM))
        sample_states[4].append(u.reshape(nb, s, D_FF)[:, -2:])

    ps = [jnp.stack(a, axis=0) for a in prompt_states]
    ss = [jnp.stack(a, axis=0) for a in sample_states]
    return (xp, xs.reshape(nb, s, D_MODEL), ps[0], ps[1], ps[2], ps[3], ps[4],
            ss[0], ss[1], ss[2], ss[3], ss[4])
```

```python
import functools
import math

import numpy as np
import jax
import jax.numpy as jnp
from jax import lax
from jax.experimental import pallas as pl
from jax.experimental.pallas import tpu as pltpu

F32 = jnp.float32
BF16 = jnp.bfloat16

D_MODEL = 1024
CHUNK = 64
HEAD_DIM = 64
EPS = 1e-6
NEG_INF = -1e30
QK_SCALE = HEAD_DIM ** -0.5

A_HEADS = 8
A_KV_HEADS = 2
A_GROUP = A_HEADS // A_KV_HEADS
A_PREV = 128
T5_BUCKETS = 32
T5_MAX_DIST = 128

B_HEADS = 8
B_PREV = 512
B_REL_CLIP = 128

A_QW = A_HEADS * HEAD_DIM
A_KVW = A_KV_HEADS * HEAD_DIM
B_W = B_HEADS * HEAD_DIM
D_FF = 3072

_C_QA = 0
_C_KA = _C_QA + A_QW
_C_VA = _C_KA + A_KVW
_C_QB = _C_VA + A_KVW
_C_KB = _C_QB + B_W
_C_VB = _C_KB + B_W
_C_GA = _C_VB + B_W
_C_GB = _C_GA + D_MODEL
IN_WIDTH = _C_GB + D_MODEL

LANES = 128
BF16_ROWS = 16
MXU_COLS = 256
GROUP = 2 * CHUNK
A_WIN = A_PREV + GROUP
B_WIN = B_PREV + GROUP
SEQ_TILE = 512
FF_CHUNK = 512
N_FF = D_FF // FF_CHUNK
FF_LOOKAHEAD = 4
VMEM_LIMIT = 56 * 1024 * 1024
LOOKAHEAD = 1
GATE_FIRST_STEP = 1
GATE_STEP_STRIDE = 3


def _rms(x, g):
    y = x * lax.rsqrt(jnp.mean(x * x, axis=-1, keepdims=True) + EPS)
    return y * g


def _dot(a, b):
    return jnp.dot(a, b, preferred_element_type=F32)


def _dot_nt(a, b):
    return lax.dot_general(a, b, (((1,), (1,)), ((), ())), preferred_element_type=F32)


def _t5_bucket(n):
    half = T5_BUCKETS // 2
    max_exact = half // 2
    ret = jnp.where(n < 0, half, 0)
    a = jnp.abs(n)
    af = jnp.maximum(a, 1).astype(jnp.float32)
    large = max_exact + (jnp.log(af / max_exact) / math.log(T5_MAX_DIST / max_exact)
                         * (half - max_exact)).astype(jnp.int32)
    large = jnp.minimum(large, half - 1)
    return ret + jnp.where(a < max_exact, a, large)


def _window_valid(n_prev, win):
    j = np.arange(win)[:, None]
    ci = np.arange(GROUP)[None, :] // CHUNK
    return ((j >= CHUNK * ci) & (j < CHUNK * ci + n_prev + CHUNK)).astype(np.int32)


def _lookup(idx, table, lo, hi):
    acc = jnp.zeros((table.shape[0], idx.shape[1]), F32)
    for b in range(lo, hi):
        acc = jnp.where(idx == b, table[:, b:b + 1], acc)
    return acc


def _toeplitz(vec, rows, first):
    w = vec.shape[1]
    return pltpu.roll(jnp.broadcast_to(vec, (rows, w)), (w - first) % w, 1, stride=1, stride_axis=0)


def _bias_kernel(t5_ref, rel_ref, ja_ref, jb_ref, jsa_ref, jsb_ref, va_ref, vb_ref, vsa_ref, vsb_ref,
                 bta_ref, btb_ref, bsa_ref, bsb_ref, *, s, jb_range, jsb_range):
    t5 = t5_ref[...]
    rel = rel_ref[...]

    fa = _lookup(ja_ref[...], t5, 0, T5_BUCKETS)
    va = va_ref[...] != 0
    for h in range(A_HEADS):
        t = _toeplitz(fa[h:h + 1, :], A_WIN, A_WIN - 1)[:, 0:GROUP]
        bta_ref[:, GROUP * h:GROUP * (h + 1)] = jnp.where(va, t, NEG_INF)

    fb = _lookup(jb_ref[...], rel, *jb_range)
    for r in range(B_WIN // LANES):
        base = B_WIN - LANES * r
        vb = vb_ref[LANES * r:LANES * (r + 1), :] != 0
        for h in range(B_HEADS):
            seg = fb[h:h + 1, base - LANES:base + LANES]
            t = _toeplitz(seg, LANES, LANES)[:, 0:GROUP]
            btb_ref[h, LANES * r:LANES * (r + 1), :] = jnp.where(vb, t, NEG_INF)

    fsa = _lookup(jsa_ref[...], t5, 0, T5_BUCKETS)
    fsb = _lookup(jsb_ref[...], rel, *jsb_range)
    vsa = vsa_ref[...] != 0
    vsb = vsb_ref[...] != 0
    for h in range(A_HEADS):
        bsa_ref[h] = jnp.where(vsa, _toeplitz(fsa[h:h + 1, :], s, s - 1), NEG_INF)
    for h in range(B_HEADS):
        bsb_ref[h] = jnp.where(vsb, _toeplitz(fsb[h:h + 1, :], s, s - 1), NEG_INF)


def _build_bias(t5_table, rel_table, s, la, lb, ka_pad, kb_pad):
    wa = 3 * LANES
    assert A_WIN - 1 + GROUP <= wa and la + 2 * s - 2 < ka_pad and lb + 2 * s - 2 < kb_pad
    bucket = lambda n: (_t5_bucket(n).astype(jnp.int32) & (T5_BUCKETS - 1)).reshape(1, -1)
    ja = bucket(jnp.arange(wa) - (A_WIN - 1) + A_PREV)
    jb = np.clip(np.arange(B_WIN + GROUP) - B_WIN + B_PREV, -B_REL_CLIP, B_REL_CLIP) + B_REL_CLIP
    jsa = bucket(la + s - 1 - jnp.arange(ka_pad))
    jsb = np.clip(lb + s - 1 - np.arange(kb_pad), -B_REL_CLIP, B_REL_CLIP) + B_REL_CLIP
    key = lambda n: np.broadcast_to(np.arange(n)[None, :], (s, n))
    vsa = (key(ka_pad) < la + s).astype(np.int32)
    vsb = (key(kb_pad) < lb + s).astype(np.int32)
    as_row = lambda v: jnp.asarray(v.astype(np.int32).reshape(1, -1))
    rng = lambda v: (int(v.min()), int(v.max()) + 1)

    vmem = pl.BlockSpec(memory_space=pltpu.VMEM)
    return pl.pallas_call(
        functools.partial(_bias_kernel, s=s, jb_range=rng(jb), jsb_range=rng(jsb)),
        out_shape=(
            jax.ShapeDtypeStruct((A_WIN, A_HEADS * GROUP), F32),
            jax.ShapeDtypeStruct((B_HEADS, B_WIN, GROUP), F32),
            jax.ShapeDtypeStruct((A_HEADS, s, ka_pad), F32),
            jax.ShapeDtypeStruct((B_HEADS, s, kb_pad), F32),
        ),
        in_specs=[vmem] * 10,
        out_specs=(vmem,) * 4,
        name="bias_tables",
    )(t5_table, rel_table, ja, as_row(jb), jsa, as_row(jsb),
      jnp.asarray(_window_valid(A_PREV, A_WIN)), jnp.asarray(_window_valid(B_PREV, B_WIN)),
      jnp.asarray(vsa), jnp.asarray(vsb))


def _ordering_zero(x, never):
    return jnp.where(never, x, 0.0)


def _add_to_corner(x, z, cols=(0,)):
    pieces, at = [], 0
    for c in cols:
        pieces += [x[0:BF16_ROWS, at:c], x[0:BF16_ROWS, c:c + LANES] + z]
        at = c + LANES
    pieces.append(x[0:BF16_ROWS, at:])
    head = jnp.concatenate([q for q in pieces if q.shape[1]], axis=1)
    return jnp.concatenate([head, x[BF16_ROWS:, :]], axis=0)


def _scores_a(g, ka_s, qt_s):
    tok = slice(GROUP * g, GROUP * (g + 1))
    kwin = ka_s[GROUP * g:GROUP * g + A_WIN, :]
    q = qt_s[0:A_QW, tok]
    zero = jnp.zeros((HEAD_DIM, GROUP), BF16)
    cols = []
    for i in range(A_HEADS):
        qi = q[HEAD_DIM * i:HEAD_DIM * (i + 1), :]
        cols.append(jnp.concatenate([qi, zero] if i < A_GROUP else [zero, qi], axis=0))
    return _dot(kwin, jnp.concatenate(cols, axis=1))


def _scores_b(g, pp, kb_s, qt_s):
    tok = slice(GROUP * g, GROUP * (g + 1))
    kwin = kb_s[GROUP * g:GROUP * g + B_WIN, 2 * LANES * pp:2 * LANES * (pp + 1)]
    qrow = A_QW + 2 * LANES * pp
    zero = jnp.zeros((HEAD_DIM, GROUP), BF16)
    cols = []
    for i in range(4):
        qi = qt_s[qrow + HEAD_DIM * i:qrow + HEAD_DIM * (i + 1), tok]
        cols.append(jnp.concatenate([zero] * i + [qi] + [zero] * (3 - i), axis=0))
    return _dot(kwin, jnp.concatenate(cols, axis=1))


def _softmax_keys(st, bias, sink_row, order_after, order_cols=(0,)):
    st = st + bias
    m = jnp.max(st, axis=0, keepdims=True)
    if sink_row is not None:
        m = jnp.maximum(m, sink_row)
    p = jnp.exp(st - m)
    if order_after is not None:
        p = _add_to_corner(p, order_after, order_cols)
    return p.astype(BF16), m


def _weighted_values(vwin, present, p, m, sink_row):
    ot = _dot(jnp.concatenate([vwin, present], axis=0), p)
    l = ot[2 * HEAD_DIM:2 * HEAD_DIM + 1, :]
    if sink_row is not None:
        l = l + jnp.exp(sink_row - m)
    return ot[0:2 * HEAD_DIM, :] * (1.0 / l)


def _values_a(g, vat_s, present_s, p, m, sink_row, oa_s):
    tok = slice(GROUP * g, GROUP * (g + 1))
    c0 = B_PREV - A_PREV + GROUP * g
    ot = _weighted_values(vat_s[:, GROUP * g:GROUP * g + A_WIN], present_s[:, c0:c0 + A_WIN], p, m, sink_row)
    for j in range(A_HEADS // 2):
        parts = []
        for i in (2 * j, 2 * j + 1):
            rows = slice(0, HEAD_DIM) if i < A_GROUP else slice(HEAD_DIM, 2 * HEAD_DIM)
            parts.append(ot[rows, GROUP * i:GROUP * (i + 1)])
        blk = jnp.concatenate(parts, axis=0)
        oa_s[tok, LANES * j:LANES * (j + 1)] = blk.T.astype(BF16)


def _values_b(g, p2, vbt_s, present_s, p, m, ob_s):
    tok = slice(GROUP * g, GROUP * (g + 1))
    keys = slice(GROUP * g, GROUP * g + B_WIN)
    lanes = slice(LANES * p2, LANES * (p2 + 1))
    ot = _weighted_values(vbt_s[lanes, keys], present_s[:, keys], p, m, None)
    blk = jnp.concatenate([ot[0:HEAD_DIM, 0:GROUP], ot[HEAD_DIM:2 * HEAD_DIM, GROUP:2 * GROUP]], axis=0)
    ob_s[tok, lanes] = blk.T.astype(BF16)


def _attend_tile(n_groups, never, ka_s, vat_s, kb_s, vbt_s, present_s, qt_s, bta_ref, btb_ref, sink_row, oa_s, ob_s,
                 fillers):
    steps = []
    for g in range(n_groups):
        steps.append(("a", g, 0))
        steps.extend(("b", g, pp) for pp in range(B_HEADS // 4))

    def scores(step):
        kind, g, pp = step
        return _scores_a(g, ka_s, qt_s) if kind == "a" else _scores_b(g, pp, kb_s, qt_s)

    ahead = [scores(step) for step in steps[:LOOKAHEAD]]
    for k, (kind, g, pp) in enumerate(steps):
        st, order_after = ahead.pop(0), None
        if k + LOOKAHEAD < len(steps):
            ahead.append(scores(steps[k + LOOKAHEAD]))
            order_after = _ordering_zero(ahead[-1][0:BF16_ROWS, 0:LANES], never)
        if fillers.get(k) is not None:
            z = _ordering_zero(fillers[k](), never)
            order_after = z if order_after is None else order_after + z
        if kind == "a":
            p, m = _softmax_keys(st, bta_ref[...], sink_row, order_after)
            _values_a(g, vat_s, present_s, p, m, sink_row, oa_s)
        else:
            bias = jnp.concatenate([btb_ref[4 * pp + i] for i in range(4)], axis=1)
            p, m = _softmax_keys(st, bias, None, order_after, (0, 2 * GROUP))
            for j in range(2):
                cols = slice(2 * GROUP * j, 2 * GROUP * (j + 1))
                _values_b(g, 2 * pp + j, vbt_s, present_s, p[:, cols], m[:, cols], ob_s)


def _mixer_kernel(x_ref, gpre_ref, gpost_ref, win_ref, wt_ref,
                  woa_ref, wob_ref, wout_ref, sink_ref, bta_ref, btb_ref, never_ref, wupg_ref, wdown_ref,
                  x1_ref, kat_ref, vat_ref, kbt_ref, vbt_ref, wupg_b_ref, wdown_b_ref,
                  h_s, qt_s, ka_s, vat_s, kb_s, vbt_s, present_s, oa_s, ob_s, mg_s, sg_s,
                  ka_f, va_f, kb_f, vb_f, *, tile, n_tiles):
    i = pl.program_id(1)

    wupg_b_ref[...] = wupg_ref[...].astype(BF16)
    wdown_b_ref[...] = wdown_ref[...].astype(BF16)

    @pl.when(i == 0)
    def _():
        ka_s[0:A_PREV, :] = jnp.zeros((A_PREV, A_KVW), BF16)
        vat_s[:, 0:A_PREV] = jnp.zeros((A_KVW, A_PREV), BF16)
        kb_s[0:B_PREV, :] = jnp.zeros((B_PREV, B_W), BF16)
        vbt_s[:, 0:B_PREV] = jnp.zeros((B_W, B_PREV), BF16)
        present_s[:, 0:B_PREV] = jnp.zeros((BF16_ROWS, B_PREV), BF16)

    @pl.when(i > 0)
    def _():
        ka_s[0:A_PREV, :] = ka_s[tile:tile + A_PREV, :]
        vat_s[:, 0:A_PREV] = vat_s[:, tile:tile + A_PREV]
        kb_s[0:B_PREV, :] = kb_s[tile:tile + B_PREV, :]
        vbt_s[:, 0:B_PREV] = vbt_s[:, tile:tile + B_PREV]
        present_s[:, 0:B_PREV] = present_s[:, tile:tile + B_PREV]

    x = x_ref[...]
    h = _rms(x, gpre_ref[...]).astype(BF16)
    h_s[...] = h

    kb = _dot(h, win_ref[:, _C_KB:_C_KB + B_W])
    kb_s[B_PREV:B_PREV + tile, :] = kb.astype(BF16)
    nq = A_QW + B_W
    pt = _dot_nt(wt_ref[...], h)
    qt_s[...] = (pt[0:nq, :] * QK_SCALE).astype(BF16)
    vat = pt[nq:nq + A_KVW, :]
    vbt = pt[nq + A_KVW:nq + A_KVW + B_W, :]
    ka = pt[nq + A_KVW + B_W:, :].T
    ka_s[A_PREV:A_PREV + tile, :] = ka.astype(BF16)
    vat_s[:, A_PREV:A_PREV + tile] = vat.astype(BF16)
    vbt_s[:, B_PREV:B_PREV + tile] = vbt.astype(BF16)
    first_row = lax.broadcasted_iota(jnp.int32, (BF16_ROWS, tile), 0) == 0
    present_s[:, B_PREV:B_PREV + tile] = jnp.where(first_row, 1.0, 0.0).astype(BF16)
    ka_f[...] = ka[tile - A_PREV:, :]
    kb_f[...] = kb[tile - B_PREV:, :]
    va_f[...] = vat[:, tile - A_PREV:]
    vb_f[...] = vbt[:, tile - B_PREV:]

    gate_w = 2 * MXU_COLS
    n_steps = (tile // GROUP) * (1 + B_HEADS // 4)

    def gate_piece(j):
        def run():
            g = _dot(h_s[...], win_ref[:, _C_GA + gate_w * j:_C_GA + gate_w * (j + 1)])
            sg_s[:, gate_w * j:gate_w * (j + 1)] = jax.nn.sigmoid(g)
            return g[0:BF16_ROWS, 0:LANES]
        return run

    n_pieces = 2 * D_MODEL // gate_w
    fillers = {GATE_FIRST_STEP + GATE_STEP_STRIDE * j: gate_piece(j) for j in range(n_pieces)}
    assert max(fillers) < n_steps
    _attend_tile(tile // GROUP, never_ref[...] != 0, ka_s, vat_s, kb_s, vbt_s, present_s, qt_s,
                 bta_ref, btb_ref, sink_ref[...], oa_s, ob_s, fillers)

    oa = oa_s[...]
    ob = ob_s[...]
    cw = 2 * MXU_COLS
    for c in range(D_MODEL // cw):
        cs = slice(cw * c, cw * (c + 1))
        ya = _dot(oa, woa_ref[:, cs])
        yb = _dot(ob, wob_ref[:, cs])
        sgb = sg_s[:, D_MODEL + cw * c:D_MODEL + cw * (c + 1)]
        mg_s[:, cs] = (sg_s[:, cs] * ya + sgb * yb).astype(BF16)
    z = _dot(mg_s[...], wout_ref[...])
    x1_ref[...] = x_ref[...] + _rms(z, gpost_ref[...])

    @pl.when(i == n_tiles - 1)
    def _():
        kat_ref[...] = ka_f[...]
        kbt_ref[...] = kb_f[...]
        vat_ref[...] = va_f[...].T
        vbt_ref[...] = vb_f[...].T


def _const_spec(shape):
    nd = len(shape)
    return pl.BlockSpec(shape, lambda *_: (0,) * nd, pipeline_mode=pl.Buffered(1))


def _prompt_mixer(x, gpre, gpost, w, sink_row, bta, btb, w_upg, w_down):
    bsz, seq, _ = x.shape
    tile = SEQ_TILE
    assert seq % tile == 0 and tile % GROUP == 0 and tile >= B_PREV
    n_tiles = seq // tile
    steps = bsz * n_tiles
    up_rows, down_rows = w_upg.shape[0] // steps, w_down.shape[0] // steps
    assert up_rows * steps == w_upg.shape[0] and down_rows * steps == w_down.shape[0]
    assert up_rows % 16 == 0 and down_rows % 16 == 0
    step_rows = lambda rows, width: pl.BlockSpec((rows, width), lambda b, i: (b * n_tiles + i, 0))
    row_spec = pl.BlockSpec((None, tile, D_MODEL), lambda b, i: (b, i, 0))

    def tail_spec(rows, width):
        return pl.BlockSpec((None, rows, width), lambda b, i: (b, 0, 0))

    consts = [gpre, gpost, w["in"], w["qvt"],
              w["oa"], w["ob"], w["out"], sink_row, bta, btb, jnp.zeros((BF16_ROWS, LANES), jnp.int32)]
    return pl.pallas_call(
        functools.partial(_mixer_kernel, tile=tile, n_tiles=n_tiles),
        grid=(bsz, n_tiles),
        in_specs=[row_spec] + [_const_spec(c.shape) for c in consts]
        + [step_rows(up_rows, w_upg.shape[1]), step_rows(down_rows, w_down.shape[1])],
        out_specs=(row_spec, tail_spec(A_PREV, A_KVW), tail_spec(A_PREV, A_KVW),
                   tail_spec(B_PREV, B_W), tail_spec(B_PREV, B_W),
                   step_rows(up_rows, w_upg.shape[1]), step_rows(down_rows, w_down.shape[1])),
        out_shape=(
            jax.ShapeDtypeStruct((bsz, seq, D_MODEL), F32),
            jax.ShapeDtypeStruct((bsz, A_PREV, A_KVW), F32),
            jax.ShapeDtypeStruct((bsz, A_PREV, A_KVW), F32),
            jax.ShapeDtypeStruct((bsz, B_PREV, B_W), F32),
            jax.ShapeDtypeStruct((bsz, B_PREV, B_W), F32),
            jax.ShapeDtypeStruct(w_upg.shape, BF16),
            jax.ShapeDtypeStruct(w_down.shape, BF16),
        ),
        scratch_shapes=[
            pltpu.VMEM((tile, D_MODEL), BF16),
            pltpu.VMEM((A_QW + B_W, tile), BF16),
            pltpu.VMEM((A_PREV + tile, A_KVW), BF16),
            pltpu.VMEM((A_KVW, A_PREV + tile), BF16),
            pltpu.VMEM((B_PREV + tile, B_W), BF16),
            pltpu.VMEM((B_W, B_PREV + tile), BF16),
            pltpu.VMEM((BF16_ROWS, B_PREV + tile), BF16),
            pltpu.VMEM((tile, A_QW), BF16),
            pltpu.VMEM((tile, B_W), BF16),
            pltpu.VMEM((tile, D_MODEL), BF16),
            pltpu.VMEM((tile, 2 * D_MODEL), F32),
            pltpu.VMEM((A_PREV, A_KVW), F32),
            pltpu.VMEM((A_KVW, A_PREV), F32),
            pltpu.VMEM((B_PREV, B_W), F32),
            pltpu.VMEM((B_W, B_PREV), F32),
        ],
        compiler_params=pltpu.CompilerParams(
            dimension_semantics=("arbitrary", "arbitrary"), vmem_limit_bytes=VMEM_LIMIT),
        name="prompt_mixer",
    )(x, *consts, w_upg, w_down)


def _conv_gelu_gate(u, gt, u1, u2, cw, cb):
    c = cb + cw[0:1, :] * u2
    c = c + cw[1:2, :] * u1
    c = c + cw[2:3, :] * u
    return jax.nn.gelu(c, approximate=True) * gt


def _ffn_kernel(x_ref, gpre_ref, gpost_ref, wupg_ref, cw_ref, cb_ref, wdown_ref, never_ref,
                x2_ref, tail_ref, carry_s, next_s, *, tile):
    i = pl.program_id(1)

    @pl.when(i == 0)
    def _():
        carry_s[...] = jnp.zeros_like(carry_s)

    @pl.when(i > 0)
    def _():
        carry_s[...] = next_s[...]

    x1 = x_ref[...]
    xn = _rms(x1, gpre_ref[...]).astype(BF16)
    never = never_ref[...] != 0
    row = lax.broadcasted_iota(jnp.int32, (tile, FF_CHUNK), 0)
    acc = jnp.zeros((tile, D_MODEL), F32)

    def up_gate(c):
        cols = slice(FF_CHUNK * c, FF_CHUNK * (c + 1))
        gcols = slice(D_FF + FF_CHUNK * c, D_FF + FF_CHUNK * (c + 1))
        return _dot(xn, wupg_ref[:, cols]), _dot(xn, wupg_ref[:, gcols])

    ahead = [up_gate(c) for c in range(min(FF_LOOKAHEAD, N_FF))]
    for c in range(N_FF):
        u, gt = ahead.pop(0)
        cols = slice(FF_CHUNK * c, FF_CHUNK * (c + 1))
        prev = carry_s[c]
        u1 = jnp.where(row == 0, prev[7:8, :], pltpu.roll(u, 1, 0))
        u2 = jnp.where(row == 0, prev[6:7, :], jnp.where(row == 1, prev[7:8, :], pltpu.roll(u, 2, 0)))
        next_s[c] = u[tile - 8:tile, :]
        tail_ref[:, cols] = u[tile - 2:tile, :]
        act = _conv_gelu_gate(u, gt, u1, u2, cw_ref[:, cols], cb_ref[:, cols])
        if c + FF_LOOKAHEAD < N_FF:
            ahead.append(up_gate(c + FF_LOOKAHEAD))
            act = _add_to_corner(act, _ordering_zero(ahead[-1][0][0:BF16_ROWS, 0:LANES], never))
        acc = acc + _dot(act.astype(BF16), wdown_ref[cols, :])
    x2_ref[...] = x1 + _rms(acc, gpost_ref[...])


def _prompt_ffn(x1, gpre, gpost, w):
    bsz, seq, _ = x1.shape
    tile = SEQ_TILE
    assert seq % tile == 0
    row_spec = pl.BlockSpec((None, tile, D_MODEL), lambda b, i: (b, i, 0))
    consts = [gpre, gpost, w["upg"], w["cw"], w["cb"], w["down"], jnp.zeros((BF16_ROWS, LANES), jnp.int32)]
    return pl.pallas_call(
        functools.partial(_ffn_kernel, tile=tile),
        grid=(bsz, seq // tile),
        in_specs=[row_spec] + [_const_spec(c.shape) for c in consts],
        out_specs=(row_spec, pl.BlockSpec((None, 2, D_FF), lambda b, i: (b, 0, 0))),
        out_shape=(jax.ShapeDtypeStruct((bsz, seq, D_MODEL), F32),
                   jax.ShapeDtypeStruct((bsz, 2, D_FF), F32)),
        scratch_shapes=[pltpu.VMEM((N_FF, 8, FF_CHUNK), F32), pltpu.VMEM((N_FF, 8, FF_CHUNK), F32)],
        compiler_params=pltpu.CompilerParams(
            dimension_semantics=("arbitrary", "arbitrary"), vmem_limit_bytes=VMEM_LIMIT),
        name="prompt_ffn",
    )(x1, *consts)


def _sample_proj_kernel(x_ref, g_ref, w_ref, p_ref):
    h = _rms(x_ref[...], g_ref[...]).astype(BF16)
    p_ref[...] = _dot(h, w_ref[...])


def _sample_proj(xs, gpre, w_in_b):
    rows = xs.shape[0]
    nblk = 2
    width = IN_WIDTH // nblk
    assert width % LANES == 0
    return pl.pallas_call(
        _sample_proj_kernel,
        grid=(nblk,),
        in_specs=[pl.BlockSpec((rows, D_MODEL), lambda j: (0, 0)),
                  pl.BlockSpec((1, D_MODEL), lambda j: (0, 0)),
                  pl.BlockSpec((D_MODEL, width), lambda j: (0, j))],
        out_specs=pl.BlockSpec((rows, width), lambda j: (0, j)),
        out_shape=jax.ShapeDtypeStruct((rows, IN_WIDTH), F32),
        compiler_params=pltpu.CompilerParams(dimension_semantics=("arbitrary",)),
        name="sample_proj",
    )(xs, gpre, w_in_b)


def _sample_attn_kernel(sink_ref, p_ref, cak_ref, cav_ref, cbk_ref, cbv_ref, bsa_ref, bsb_ref, never_ref,
                        oa_ref, ob_ref, kan_ref, van_ref, kbn_ref, vbn_ref, ka_s, va_s, kb_s, vb_s, *, s, la, lb):
    p = p_ref[...]
    kan_ref[...] = p[:, _C_KA:_C_KA + A_KVW]
    van_ref[...] = p[:, _C_VA:_C_VA + A_KVW]
    kbn_ref[...] = p[:, _C_KB:_C_KB + B_W]
    vbn_ref[...] = p[:, _C_VB:_C_VB + B_W]
    ka_pad = ka_s.shape[0]
    kb_pad = kb_s.shape[0]

    def fill(dst, cache_ref, new, n_cache, n_pad):
        dst[0:n_cache, :] = cache_ref[...].astype(BF16)
        dst[n_cache:n_cache + s, :] = new.astype(BF16)
        dst[n_cache + s:n_pad, :] = jnp.zeros((n_pad - n_cache - s, dst.shape[1]), BF16)

    fill(ka_s, cak_ref, p[:, _C_KA:_C_KA + A_KVW], la, ka_pad)
    fill(va_s, cav_ref, p[:, _C_VA:_C_VA + A_KVW], la, ka_pad)
    fill(kb_s, cbk_ref, p[:, _C_KB:_C_KB + B_W], lb, kb_pad)
    fill(vb_s, cbv_ref, p[:, _C_VB:_C_VB + B_W], lb, kb_pad)

    lo = lax.broadcasted_iota(jnp.int32, (s, LANES), 1) < HEAD_DIM
    top = lax.broadcasted_iota(jnp.int32, (2 * s, 1), 0) < s
    never = never_ref[...] != 0
    ka = ka_s[...]
    va = va_s[...]

    def scores(step):
        kind, j = step
        if kind == "a":
            q = p[:, _C_QA + LANES * j:_C_QA + LANES * (j + 1)] * QK_SCALE
            qr = pltpu.roll(q, HEAD_DIM, 1)
            if 2 * j < A_GROUP:
                qs = jnp.concatenate([jnp.where(lo, q, 0.0), jnp.where(lo, qr, 0.0)], axis=0)
            else:
                qs = jnp.concatenate([jnp.where(lo, 0.0, qr), jnp.where(lo, 0.0, q)], axis=0)
            bias = jnp.concatenate([bsa_ref[2 * j], bsa_ref[2 * j + 1]], axis=0)
            return _dot_nt(qs.astype(BF16), ka) + bias
        q = p[:, _C_QB + LANES * j:_C_QB + LANES * (j + 1)] * QK_SCALE
        qs = jnp.concatenate([jnp.where(lo, q, 0.0), jnp.where(lo, 0.0, q)], axis=0)
        bias = jnp.concatenate([bsb_ref[2 * j], bsb_ref[2 * j + 1]], axis=0)
        return _dot_nt(qs.astype(BF16), kb_s[:, LANES * j:LANES * (j + 1)]) + bias

    def softmax_pv(st, v, sink, order_after):
        m = jnp.max(st, axis=-1, keepdims=True)
        if sink is not None:
            m = jnp.maximum(m, sink)
        e = jnp.exp(st - m)
        l = jnp.sum(e, axis=-1, keepdims=True)
        if sink is not None:
            l = l + jnp.exp(sink - m)
        if order_after is not None:
            e = _add_to_corner(e, order_after)
        return _dot(e.astype(BF16), v) * (1.0 / l)

    steps = [("a", j) for j in range(A_HEADS // 2)] + [("b", j) for j in range(B_HEADS // 2)]
    st_next = scores(steps[0])
    for k, (kind, j) in enumerate(steps):
        st, order_after = st_next, None
        if k + 1 < len(steps):
            st_next = scores(steps[k + 1])
            order_after = _ordering_zero(st_next[0:BF16_ROWS, 0:LANES], never)
        lanes = slice(LANES * j, LANES * (j + 1))
        if kind == "a":
            sink = jnp.where(top, sink_ref[2 * j], sink_ref[2 * j + 1])
            o = softmax_pv(st, va, sink, order_after)
            oe, oo = o[0:s], o[s:2 * s]
            if 2 * j < A_GROUP:
                oa_ref[:, lanes] = jnp.where(lo, oe, pltpu.roll(oo, HEAD_DIM, 1))
            else:
                oa_ref[:, lanes] = jnp.where(lo, pltpu.roll(oe, HEAD_DIM, 1), oo)
        else:
            o = softmax_pv(st, vb_s[:, lanes], None, order_after)
            ob_ref[:, lanes] = jnp.where(lo, o[0:s], o[s:2 * s])


def _sample_attn(sink, p, cak, cav, cbk, cbv, bsa, bsb, s):
    nb, la, _ = cak.shape
    lb = cbk.shape[1]
    ka_pad, kb_pad = bsa.shape[-1], bsb.shape[-1]
    assert s % 16 == 0 and la % 16 == 0 and lb % 16 == 0
    row = lambda w: pl.BlockSpec((s, w), lambda b: (b, 0))
    cache = lambda n, w: pl.BlockSpec((None, n, w), lambda b: (b, 0, 0))
    full = lambda a: pl.BlockSpec(a.shape, lambda b: (0,) * a.ndim)
    widths = (A_QW, B_W, A_KVW, A_KVW, B_W, B_W)
    return pl.pallas_call(
        functools.partial(_sample_attn_kernel, s=s, la=la, lb=lb),
        grid=(nb,),
        in_specs=[pl.BlockSpec(memory_space=pltpu.SMEM), row(IN_WIDTH),
                  cache(la, A_KVW), cache(la, A_KVW), cache(lb, B_W), cache(lb, B_W), full(bsa), full(bsb),
                  pl.BlockSpec((BF16_ROWS, LANES), lambda b: (0, 0))],
        out_specs=tuple(row(w) for w in widths),
        out_shape=tuple(jax.ShapeDtypeStruct((nb * s, w), F32) for w in widths),
        scratch_shapes=[pltpu.VMEM((ka_pad, A_KVW), BF16), pltpu.VMEM((ka_pad, A_KVW), BF16),
                        pltpu.VMEM((kb_pad, B_W), BF16), pltpu.VMEM((kb_pad, B_W), BF16)],
        compiler_params=pltpu.CompilerParams(dimension_semantics=("arbitrary",)),
        name="sample_attn",
    )(sink, p, cak, cav, cbk, cbv, bsa, bsb, jnp.zeros((BF16_ROWS, LANES), jnp.int32))


def _sample_tail_kernel(x_ref, oa_ref, ob_ref, p_ref, woa_ref, wob_ref, wout_ref,
                        gpost_ref, gpre2_ref, gpost2_ref, init_ref,
                        wup_ref, wgate_ref, cw_ref, cb_ref, wdown_ref,
                        x2_ref, u_ref, x1_s, xn_s, acc_s, *, s):
    c = pl.program_id(0)

    @pl.when(c == 0)
    def _():
        ya = _dot(oa_ref[...].astype(BF16), woa_ref[...])
        yb = _dot(ob_ref[...].astype(BF16), wob_ref[...])
        ga = p_ref[:, _C_GA:_C_GA + D_MODEL]
        gb = p_ref[:, _C_GB:_C_GB + D_MODEL]
        merged = jax.nn.sigmoid(ga) * ya + jax.nn.sigmoid(gb) * yb
        z = _dot(merged.astype(BF16), wout_ref[...])
        x1 = x_ref[...] + _rms(z, gpost_ref[...])
        x1_s[...] = x1
        xn_s[...] = _rms(x1, gpre2_ref[...]).astype(BF16)
        acc_s[...] = jnp.zeros_like(acc_s)

    xn = xn_s[...]
    u = _dot(xn, wup_ref[...])
    gt = _dot(xn, wgate_ref[...])
    u_ref[...] = u
    pos = lax.broadcasted_iota(jnp.int32, u.shape, 0) % s
    init = init_ref[...]
    u1 = jnp.where(pos == 0, pltpu.roll(init, u.shape[0] - 1, 0), pltpu.roll(u, 1, 0))
    u2 = jnp.where(pos < 2, init, pltpu.roll(u, 2, 0))
    act = _conv_gelu_gate(u, gt, u1, u2, cw_ref[...], cb_ref[...])
    acc_s[...] += _dot(act.astype(BF16), wdown_ref[...])

    @pl.when(c == pl.num_programs(0) - 1)
    def _():
        x2_ref[...] = x1_s[...] + _rms(acc_s[...], gpost2_ref[...])


def _sample_tail(xs, oa, ob, p, w, gpost, gpre2, gpost2, init, s):
    rows = xs.shape[0]
    full = lambda a: pl.BlockSpec(a.shape, lambda c: (0,) * a.ndim)
    cols = lambda r: pl.BlockSpec((r, FF_CHUNK), lambda c: (0, c))
    gate_cols = pl.BlockSpec((D_MODEL, FF_CHUNK), lambda c: (0, N_FF + c))
    down_rows = pl.BlockSpec((FF_CHUNK, D_MODEL), lambda c: (c, 0))
    args = [xs, oa, ob, p, w["oa"], w["ob"], w["out"], gpost, gpre2, gpost2]
    return pl.pallas_call(
        functools.partial(_sample_tail_kernel, s=s),
        grid=(N_FF,),
        in_specs=[full(a) for a in args] + [cols(rows), cols(D_MODEL), gate_cols, cols(3), cols(1), down_rows],
        out_specs=(full(xs), cols(rows)),
        out_shape=(jax.ShapeDtypeStruct((rows, D_MODEL), F32), jax.ShapeDtypeStruct((rows, D_FF), F32)),
        scratch_shapes=[pltpu.VMEM((rows, D_MODEL), F32), pltpu.VMEM((rows, D_MODEL), BF16),
                        pltpu.VMEM((rows, D_MODEL), F32)],
        compiler_params=pltpu.CompilerParams(dimension_semantics=("arbitrary",)),
        name="sample_tail",
    )(*args, init, w["upg"], w["upg"], w["cw"], w["cb"], w["down"])


def _prep_weights(w_in, w_oa, w_ob, w_out, conv_w, conv_b):
    wb = w_in.astype(BF16)
    qv = jnp.concatenate([wb[:, _C_QA:_C_QA + A_QW], wb[:, _C_QB:_C_QB + B_W],
                          wb[:, _C_VA:_C_VA + A_KVW], wb[:, _C_VB:_C_VB + B_W],
                          wb[:, _C_KA:_C_KA + A_KVW]], axis=1)
    return {
        "in": wb, "qvt": qv.T,
        "oa": w_oa.astype(BF16), "ob": w_ob.astype(BF16), "out": w_out.astype(BF16),
        "cw": conv_w, "cb": conv_b.reshape(1, D_FF),
    }


def _round_up(n, m):
    return (n + m - 1) // m * m


def kernel(x_prompt, x_sample, cache_a_k, cache_a_v, cache_b_k, cache_b_v, state_conv,
           w_in, w_oa, w_ob, w_out, sink_a, t5_table, rel_table_b,
           g_pre_mix, g_post_mix, g_pre_ffn, g_post_ffn, w_upg, conv_w, conv_b, w_down):
    depth = w_in.shape[0]
    bsz, seq, _ = x_prompt.shape
    nb, s, _ = x_sample.shape
    la, lb = cache_a_k.shape[2], cache_b_k.shape[2]
    ka_pad = _round_up(la + s, LANES)
    kb_pad = _round_up(lb + s, LANES)

    xp = x_prompt
    xs = x_sample.reshape(nb * s, D_MODEL)
    prompt_states = [[] for _ in range(5)]
    sample_states = [[] for _ in range(5)]
    for l in range(depth):
        w = _prep_weights(w_in[l], w_oa[l], w_ob[l], w_out[l], conv_w[l], conv_b[l])
        row = lambda g: g[l].reshape(1, D_MODEL)
        bta, btb, bsa, bsb = _build_bias(t5_table, rel_table_b[l], s, la, lb, ka_pad, kb_pad)
        sink_row = jnp.repeat(sink_a[l], GROUP).reshape(1, A_HEADS * GROUP)

        x1, ka_t, va_t, kb_t, vb_t, w["upg"], w["down"] = _prompt_mixer(
            xp, row(g_pre_mix), row(g_post_mix), w, sink_row, bta, btb, w_upg[l], w_down[l])
        xp, conv_t = _prompt_ffn(x1, row(g_pre_ffn), row(g_post_ffn), w)
        prompt_states[0].append(ka_t.reshape(bsz, A_PREV, A_KV_HEADS, HEAD_DIM))
        prompt_states[1].append(va_t.reshape(bsz, A_PREV, A_KV_HEADS, HEAD_DIM))
        prompt_states[2].append(kb_t.reshape(bsz, B_PREV, B_HEADS, HEAD_DIM))
        prompt_states[3].append(vb_t.reshape(bsz, B_PREV, B_HEADS, HEAD_DIM))
        prompt_states[4].append(conv_t)

        p = _sample_proj(xs, row(g_pre_mix), w["in"])
        oa, ob, ka_n, va_n, kb_n, vb_n = _sample_attn(
            sink_a[l], p, cache_a_k[l].reshape(nb, la, A_KVW), cache_a_v[l].reshape(nb, la, A_KVW),
            cache_b_k[l].reshape(nb, lb, B_W), cache_b_v[l].reshape(nb, lb, B_W), bsa, bsb, s)
        init = jnp.pad(state_conv[l], ((0, 0), (0, s - 2), (0, 0))).reshape(nb * s, D_FF)
        xs, u = _sample_tail(xs, oa, ob, p, w, row(g_post_mix), row(g_pre_ffn), row(g_post_ffn), init, s)
        sample_states[0].append(ka_n.reshape(nb, s, A_KV_HEADS, HEAD_DIM))
        sample_states[1].append(va_n.reshape(nb, s, A_KV_HEADS, HEAD_DIM))
        sample_states[2].append(kb_n.reshape(nb, s, B_HEADS, HEAD_DIM))
        sample_states[3].append(vb_n.reshape(nb, s, B_HEADS, HEAD_DIM))
        sample_states[4].append(u.reshape(nb, s, D_FF)[:, -2:])

    ps = [jnp.stack(a, axis=0) for a in prompt_states]
    ss = [jnp.stack(a, axis=0) for a in sample_states]
    return (xp, xs.reshape(nb, s, D_MODEL), ps[0], ps[1], ps[2], ps[3], ps[4],
            ss[0], ss[1], ss[2], ss[3], ss[4])
```

```python
import functools
import math

import numpy as np
import jax
import jax.numpy as jnp
from jax import lax
from jax.experimental import pallas as pl
from jax.experimental.pallas import tpu as pltpu

F32 = jnp.float32
BF16 = jnp.bfloat16

D_MODEL = 1024
CHUNK = 64
HEAD_DIM = 64
EPS = 1e-6
NEG_INF = -1e30
QK_SCALE = HEAD_DIM ** -0.5
LOG2E = math.log2(math.e)

A_HEADS = 8
A_KV_HEADS = 2
A_GROUP = A_HEADS // A_KV_HEADS
A_PREV = 128
T5_BUCKETS = 32
T5_MAX_DIST = 128

B_HEADS = 8
B_PREV = 512
B_REL_CLIP = 128

A_QW = A_HEADS * HEAD_DIM
A_KVW = A_KV_HEADS * HEAD_DIM
B_W = B_HEADS * HEAD_DIM
D_FF = 3072

_C_QA = 0
_C_KA = _C_QA + A_QW
_C_VA = _C_KA + A_KVW
_C_QB = _C_VA + A_KVW
_C_KB = _C_QB + B_W
_C_VB = _C_KB + B_W
_C_GA = _C_VB + B_W
_C_GB = _C_GA + D_MODEL
IN_WIDTH = _C_GB + D_MODEL

LANES = 128
BF16_ROWS = 16
MXU_COLS = 256
GROUP = 2 * CHUNK
A_WIN = A_PREV + GROUP
B_WIN = B_PREV + GROUP
SEQ_TILE = 512
FF_CHUNK = 512
N_FF = D_FF // FF_CHUNK
FF_LOOKAHEAD = 4
VMEM_LIMIT = 56 * 1024 * 1024
LOOKAHEAD = 1
GATE_FIRST_STEP = 1
GATE_STEP_STRIDE = 3


def _rms(x, g):
    y = x * lax.rsqrt(jnp.mean(x * x, axis=-1, keepdims=True) + EPS)
    return y * g


def _dot(a, b):
    return jnp.dot(a, b, preferred_element_type=F32)


def _dot_nt(a, b):
    return lax.dot_general(a, b, (((1,), (1,)), ((), ())), preferred_element_type=F32)


def _t5_bucket(n):
    half = T5_BUCKETS // 2
    max_exact = half // 2
    ret = jnp.where(n < 0, half, 0)
    a = jnp.abs(n)
    af = jnp.maximum(a, 1).astype(jnp.float32)
    large = max_exact + (jnp.log(af / max_exact) / math.log(T5_MAX_DIST / max_exact)
                         * (half - max_exact)).astype(jnp.int32)
    large = jnp.minimum(large, half - 1)
    return ret + jnp.where(a < max_exact, a, large)


def _window_valid(n_prev, win):
    j = np.arange(win)[:, None]
    ci = np.arange(GROUP)[None, :] // CHUNK
    return ((j >= CHUNK * ci) & (j < CHUNK * ci + n_prev + CHUNK)).astype(np.int32)


def _lookup(idx, table, lo, hi):
    acc = jnp.zeros((table.shape[0], idx.shape[1]), F32)
    for b in range(lo, hi):
        acc = jnp.where(idx == b, table[:, b:b + 1], acc)
    return acc


def _toeplitz(vec, rows, first):
    w = vec.shape[1]
    return pltpu.roll(jnp.broadcast_to(vec, (rows, w)), (w - first) % w, 1, stride=1, stride_axis=0)


def _bias_kernel(t5_ref, rel_ref, ja_ref, jb_ref, jsa_ref, jsb_ref, va_ref, vb_ref, vsa_ref, vsb_ref,
                 bta_ref, btb_ref, bsa_ref, bsb_ref, *, s, jb_range, jsb_range):
    t5 = t5_ref[...]
    rel = rel_ref[...]

    fa = _lookup(ja_ref[...], t5, 0, T5_BUCKETS) * LOG2E
    va = va_ref[...] != 0
    for h in range(A_HEADS):
        t = _toeplitz(fa[h:h + 1, :], A_WIN, A_WIN - 1)[:, 0:GROUP]
        bta_ref[:, GROUP * h:GROUP * (h + 1)] = jnp.where(va, t, NEG_INF)

    fb = _lookup(jb_ref[...], rel, *jb_range) * LOG2E
    for r in range(B_WIN // LANES):
        base = B_WIN - LANES * r
        vb = vb_ref[LANES * r:LANES * (r + 1), :] != 0
        for h in range(B_HEADS):
            seg = fb[h:h + 1, base - LANES:base + LANES]
            t = _toeplitz(seg, LANES, LANES)[:, 0:GROUP]
            btb_ref[h, LANES * r:LANES * (r + 1), :] = jnp.where(vb, t, NEG_INF)

    fsa = _lookup(jsa_ref[...], t5, 0, T5_BUCKETS)
    fsb = _lookup(jsb_ref[...], rel, *jsb_range)
    vsa = vsa_ref[...] != 0
    vsb = vsb_ref[...] != 0
    for h in range(A_HEADS):
        bsa_ref[h] = jnp.where(vsa, _toeplitz(fsa[h:h + 1, :], s, s - 1), NEG_INF)
    for h in range(B_HEADS):
        bsb_ref[h] = jnp.where(vsb, _toeplitz(fsb[h:h + 1, :], s, s - 1), NEG_INF)


def _build_bias(t5_table, rel_table, s, la, lb, ka_pad, kb_pad):
    wa = 3 * LANES
    assert A_WIN - 1 + GROUP <= wa and la + 2 * s - 2 < ka_pad and lb + 2 * s - 2 < kb_pad
    bucket = lambda n: (_t5_bucket(n).astype(jnp.int32) & (T5_BUCKETS - 1)).reshape(1, -1)
    ja = bucket(jnp.arange(wa) - (A_WIN - 1) + A_PREV)
    jb = np.clip(np.arange(B_WIN + GROUP) - B_WIN + B_PREV, -B_REL_CLIP, B_REL_CLIP) + B_REL_CLIP
    jsa = bucket(la + s - 1 - jnp.arange(ka_pad))
    jsb = np.clip(lb + s - 1 - np.arange(kb_pad), -B_REL_CLIP, B_REL_CLIP) + B_REL_CLIP
    key = lambda n: np.broadcast_to(np.arange(n)[None, :], (s, n))
    vsa = (key(ka_pad) < la + s).astype(np.int32)
    vsb = (key(kb_pad) < lb + s).astype(np.int32)
    as_row = lambda v: jnp.asarray(v.astype(np.int32).reshape(1, -1))
    rng = lambda v: (int(v.min()), int(v.max()) + 1)

    vmem = pl.BlockSpec(memory_space=pltpu.VMEM)
    return pl.pallas_call(
        functools.partial(_bias_kernel, s=s, jb_range=rng(jb), jsb_range=rng(jsb)),
        out_shape=(
            jax.ShapeDtypeStruct((A_WIN, A_HEADS * GROUP), F32),
            jax.ShapeDtypeStruct((B_HEADS, B_WIN, GROUP), F32),
            jax.ShapeDtypeStruct((A_HEADS, s, ka_pad), F32),
            jax.ShapeDtypeStruct((B_HEADS, s, kb_pad), F32),
        ),
        in_specs=[vmem] * 10,
        out_specs=(vmem,) * 4,
        name="bias_tables",
    )(t5_table, rel_table, ja, as_row(jb), jsa, as_row(jsb),
      jnp.asarray(_window_valid(A_PREV, A_WIN)), jnp.asarray(_window_valid(B_PREV, B_WIN)),
      jnp.asarray(vsa), jnp.asarray(vsb))


def _ordering_zero(x, never):
    return jnp.where(never, x, 0.0)


def _add_to_corner(x, z, cols=(0,)):
    pieces, at = [], 0
    for c in cols:
        pieces += [x[0:BF16_ROWS, at:c], x[0:BF16_ROWS, c:c + LANES] + z]
        at = c + LANES
    pieces.append(x[0:BF16_ROWS, at:])
    head = jnp.concatenate([q for q in pieces if q.shape[1]], axis=1)
    return jnp.concatenate([head, x[BF16_ROWS:, :]], axis=0)


def _scores_a(g, ka_s, qt_s):
    tok = slice(GROUP * g, GROUP * (g + 1))
    kwin = ka_s[GROUP * g:GROUP * g + A_WIN, :]
    q = qt_s[0:A_QW, tok]
    zero = jnp.zeros((HEAD_DIM, GROUP), BF16)
    cols = []
    for i in range(A_HEADS):
        qi = q[HEAD_DIM * i:HEAD_DIM * (i + 1), :]
        cols.append(jnp.concatenate([qi, zero] if i < A_GROUP else [zero, qi], axis=0))
    return _dot(kwin, jnp.concatenate(cols, axis=1))


def _scores_b(g, pp, kb_s, qt_s):
    tok = slice(GROUP * g, GROUP * (g + 1))
    kwin = kb_s[GROUP * g:GROUP * g + B_WIN, 2 * LANES * pp:2 * LANES * (pp + 1)]
    qrow = A_QW + 2 * LANES * pp
    zero = jnp.zeros((HEAD_DIM, GROUP), BF16)
    cols = []
    for i in range(4):
        qi = qt_s[qrow + HEAD_DIM * i:qrow + HEAD_DIM * (i + 1), tok]
        cols.append(jnp.concatenate([zero] * i + [qi] + [zero] * (3 - i), axis=0))
    return _dot(kwin, jnp.concatenate(cols, axis=1))


def _softmax_keys(st, bias, sink_row, order_after, order_cols=(0,)):
    st = st + bias
    m = jnp.max(st, axis=0, keepdims=True)
    if sink_row is not None:
        m = jnp.maximum(m, sink_row)
    p = jnp.exp2(st - m)
    if order_after is not None:
        p = _add_to_corner(p, order_after, order_cols)
    return p.astype(BF16), m


def _weighted_values(vwin, present, p, m, sink_row):
    ot = _dot(jnp.concatenate([vwin, present], axis=0), p)
    l = ot[2 * HEAD_DIM:2 * HEAD_DIM + 1, :]
    if sink_row is not None:
        l = l + jnp.exp2(sink_row - m)
    return ot[0:2 * HEAD_DIM, :] * (1.0 / l)


def _values_a(g, vat_s, present_s, p, m, sink_row, oa_s):
    tok = slice(GROUP * g, GROUP * (g + 1))
    c0 = B_PREV - A_PREV + GROUP * g
    ot = _weighted_values(vat_s[:, GROUP * g:GROUP * g + A_WIN], present_s[:, c0:c0 + A_WIN], p, m, sink_row)
    for j in range(A_HEADS // 2):
        parts = []
        for i in (2 * j, 2 * j + 1):
            rows = slice(0, HEAD_DIM) if i < A_GROUP else slice(HEAD_DIM, 2 * HEAD_DIM)
            parts.append(ot[rows, GROUP * i:GROUP * (i + 1)])
        blk = jnp.concatenate(parts, axis=0)
        oa_s[tok, LANES * j:LANES * (j + 1)] = blk.astype(BF16).T


def _values_b(g, p2, vbt_s, present_s, p, m, ob_s):
    tok = slice(GROUP * g, GROUP * (g + 1))
    keys = slice(GROUP * g, GROUP * g + B_WIN)
    lanes = slice(LANES * p2, LANES * (p2 + 1))
    ot = _weighted_values(vbt_s[lanes, keys], present_s[:, keys], p, m, None)
    blk = jnp.concatenate([ot[0:HEAD_DIM, 0:GROUP], ot[HEAD_DIM:2 * HEAD_DIM, GROUP:2 * GROUP]], axis=0)
    ob_s[tok, lanes] = blk.astype(BF16).T


def _attend_tile(n_groups, never, ka_s, vat_s, kb_s, vbt_s, present_s, qt_s, bta_ref, btb_ref, sink_row, oa_s, ob_s,
                 fillers):
    steps = []
    for g in range(n_groups):
        steps.append(("a", g, 0))
        steps.extend(("b", g, pp) for pp in range(B_HEADS // 4))

    def scores(step):
        kind, g, pp = step
        return _scores_a(g, ka_s, qt_s) if kind == "a" else _scores_b(g, pp, kb_s, qt_s)

    ahead = [scores(step) for step in steps[:LOOKAHEAD]]
    for k, (kind, g, pp) in enumerate(steps):
        st, order_after = ahead.pop(0), None
        if k + LOOKAHEAD < len(steps):
            ahead.append(scores(steps[k + LOOKAHEAD]))
            order_after = _ordering_zero(ahead[-1][0:BF16_ROWS, 0:LANES], never)
        if fillers.get(k) is not None:
            z = _ordering_zero(fillers[k](), never)
            order_after = z if order_after is None else order_after + z
        if kind == "a":
            p, m = _softmax_keys(st, bta_ref[...], sink_row, order_after)
            _values_a(g, vat_s, present_s, p, m, sink_row, oa_s)
        else:
            bias = jnp.concatenate([btb_ref[4 * pp + i] for i in range(4)], axis=1)
            p, m = _softmax_keys(st, bias, None, order_after, (0, 2 * GROUP))
            for j in range(2):
                cols = slice(2 * GROUP * j, 2 * GROUP * (j + 1))
                _values_b(g, 2 * pp + j, vbt_s, present_s, p[:, cols], m[:, cols], ob_s)


def _mixer_kernel(x_ref, gpre_ref, gpost_ref, win_ref, wt_ref,
                  woa_ref, wob_ref, wout_ref, sink_ref, bta_ref, btb_ref, never_ref, wupg_ref, wdown_ref,
                  x1_ref, kat_ref, vat_ref, kbt_ref, vbt_ref, wupg_b_ref, wdown_b_ref,
                  h_s, qt_s, ka_s, vat_s, kb_s, vbt_s, present_s, oa_s, ob_s, mg_s, sg_s,
                  ka_f, va_f, kb_f, vb_f, *, tile, n_tiles):
    i = pl.program_id(1)

    wupg_b_ref[...] = wupg_ref[...].astype(BF16)
    wdown_b_ref[...] = wdown_ref[...].astype(BF16)

    @pl.when(i == 0)
    def _():
        ka_s[0:A_PREV, :] = jnp.zeros((A_PREV, A_KVW), BF16)
        vat_s[:, 0:A_PREV] = jnp.zeros((A_KVW, A_PREV), BF16)
        kb_s[0:B_PREV, :] = jnp.zeros((B_PREV, B_W), BF16)
        vbt_s[:, 0:B_PREV] = jnp.zeros((B_W, B_PREV), BF16)
        present_s[:, 0:B_PREV] = jnp.zeros((BF16_ROWS, B_PREV), BF16)

    @pl.when(i > 0)
    def _():
        ka_s[0:A_PREV, :] = ka_s[tile:tile + A_PREV, :]
        vat_s[:, 0:A_PREV] = vat_s[:, tile:tile + A_PREV]
        kb_s[0:B_PREV, :] = kb_s[tile:tile + B_PREV, :]
        vbt_s[:, 0:B_PREV] = vbt_s[:, tile:tile + B_PREV]
        present_s[:, 0:B_PREV] = present_s[:, tile:tile + B_PREV]

    x = x_ref[...]
    h = _rms(x, gpre_ref[...]).astype(BF16)
    h_s[...] = h

    kb = _dot(h, win_ref[:, _C_KB:_C_KB + B_W])
    kb_s[B_PREV:B_PREV + tile, :] = kb.astype(BF16)
    nq = A_QW + B_W
    pt = _dot_nt(wt_ref[...], h)
    qt_s[...] = (pt[0:nq, :] * (QK_SCALE * LOG2E)).astype(BF16)
    vat = pt[nq:nq + A_KVW, :]
    vbt = pt[nq + A_KVW:nq + A_KVW + B_W, :]
    ka = pt[nq + A_KVW + B_W:, :].T
    ka_s[A_PREV:A_PREV + tile, :] = ka.astype(BF16)
    vat_s[:, A_PREV:A_PREV + tile] = vat.astype(BF16)
    vbt_s[:, B_PREV:B_PREV + tile] = vbt.astype(BF16)
    first_row = lax.broadcasted_iota(jnp.int32, (BF16_ROWS, tile), 0) == 0
    present_s[:, B_PREV:B_PREV + tile] = jnp.where(first_row, 1.0, 0.0).astype(BF16)
    ka_f[...] = ka[tile - A_PREV:, :]
    kb_f[...] = kb[tile - B_PREV:, :]
    va_f[...] = vat[:, tile - A_PREV:]
    vb_f[...] = vbt[:, tile - B_PREV:]

    gate_w = 2 * MXU_COLS
    n_steps = (tile // GROUP) * (1 + B_HEADS // 4)

    def gate_piece(j):
        def run():
            g = _dot(h_s[...], win_ref[:, _C_GA + gate_w * j:_C_GA + gate_w * (j + 1)])
            sg_s[:, gate_w * j:gate_w * (j + 1)] = jax.nn.sigmoid(g)
            return g[0:BF16_ROWS, 0:LANES]
        return run

    n_pieces = 2 * D_MODEL // gate_w
    fillers = {GATE_FIRST_STEP + GATE_STEP_STRIDE * j: gate_piece(j) for j in range(n_pieces)}
    assert max(fillers) < n_steps
    _attend_tile(tile // GROUP, never_ref[...] != 0, ka_s, vat_s, kb_s, vbt_s, present_s, qt_s,
                 bta_ref, btb_ref, sink_ref[...] * LOG2E, oa_s, ob_s, fillers)

    oa = oa_s[...]
    ob = ob_s[...]
    cw = 2 * MXU_COLS
    for c in range(D_MODEL // cw):
        cs = slice(cw * c, cw * (c + 1))
        ya = _dot(oa, woa_ref[:, cs])
        yb = _dot(ob, wob_ref[:, cs])
        sgb = sg_s[:, D_MODEL + cw * c:D_MODEL + cw * (c + 1)]
        mg_s[:, cs] = (sg_s[:, cs] * ya + sgb * yb).astype(BF16)
    z = _dot(mg_s[...], wout_ref[...])
    x1_ref[...] = x_ref[...] + _rms(z, gpost_ref[...])

    @pl.when(i == n_tiles - 1)
    def _():
        kat_ref[...] = ka_f[...]
        kbt_ref[...] = kb_f[...]
        vat_ref[...] = va_f[...].T
        vbt_ref[...] = vb_f[...].T


def _const_spec(shape):
    nd = len(shape)
    return pl.BlockSpec(shape, lambda *_: (0,) * nd, pipeline_mode=pl.Buffered(1))


def _prompt_mixer(x, gpre, gpost, w, sink_row, bta, btb, w_upg, w_down):
    bsz, seq, _ = x.shape
    tile = SEQ_TILE
    assert seq % tile == 0 and tile % GROUP == 0 and tile >= B_PREV
    n_tiles = seq // tile
    steps = bsz * n_tiles
    up_rows, down_rows = w_upg.shape[0] // steps, w_down.shape[0] // steps
    assert up_rows * steps == w_upg.shape[0] and down_rows * steps == w_down.shape[0]
    assert up_rows % 16 == 0 and down_rows % 16 == 0
    step_rows = lambda rows, width: pl.BlockSpec((rows, width), lambda b, i: (b * n_tiles + i, 0))
    row_spec = pl.BlockSpec((None, tile, D_MODEL), lambda b, i: (b, i, 0))

    def tail_spec(rows, width):
        return pl.BlockSpec((None, rows, width), lambda b, i: (b, 0, 0))

    consts = [gpre, gpost, w["in"], w["qvt"],
              w["oa"], w["ob"], w["out"], sink_row, bta, btb, jnp.zeros((BF16_ROWS, LANES), jnp.int32)]
    return pl.pallas_call(
        functools.partial(_mixer_kernel, tile=tile, n_tiles=n_tiles),
        grid=(bsz, n_tiles),
        in_specs=[row_spec] + [_const_spec(c.shape) for c in consts]
        + [step_rows(up_rows, w_upg.shape[1]), step_rows(down_rows, w_down.shape[1])],
        out_specs=(row_spec, tail_spec(A_PREV, A_KVW), tail_spec(A_PREV, A_KVW),
                   tail_spec(B_PREV, B_W), tail_spec(B_PREV, B_W),
                   step_rows(up_rows, w_upg.shape[1]), step_rows(down_rows, w_down.shape[1])),
        out_shape=(
            jax.ShapeDtypeStruct((bsz, seq, D_MODEL), F32),
            jax.ShapeDtypeStruct((bsz, A_PREV, A_KVW), F32),
            jax.ShapeDtypeStruct((bsz, A_PREV, A_KVW), F32),
            jax.ShapeDtypeStruct((bsz, B_PREV, B_W), F32),
            jax.ShapeDtypeStruct((bsz, B_PREV, B_W), F32),
            jax.ShapeDtypeStruct(w_upg.shape, BF16),
            jax.ShapeDtypeStruct(w_down.shape, BF16),
        ),
        scratch_shapes=[
            pltpu.VMEM((tile, D_MODEL), BF16),
            pltpu.VMEM((A_QW + B_W, tile), BF16),
            pltpu.VMEM((A_PREV + tile, A_KVW), BF16),
            pltpu.VMEM((A_KVW, A_PREV + tile), BF16),
            pltpu.VMEM((B_PREV + tile, B_W), BF16),
            pltpu.VMEM((B_W, B_PREV + tile), BF16),
            pltpu.VMEM((BF16_ROWS, B_PREV + tile), BF16),
            pltpu.VMEM((tile, A_QW), BF16),
            pltpu.VMEM((tile, B_W), BF16),
            pltpu.VMEM((tile, D_MODEL), BF16),
            pltpu.VMEM((tile, 2 * D_MODEL), F32),
            pltpu.VMEM((A_PREV, A_KVW), F32),
            pltpu.VMEM((A_KVW, A_PREV), F32),
            pltpu.VMEM((B_PREV, B_W), F32),
            pltpu.VMEM((B_W, B_PREV), F32),
        ],
        compiler_params=pltpu.CompilerParams(
            dimension_semantics=("arbitrary", "arbitrary"), vmem_limit_bytes=VMEM_LIMIT),
        name="prompt_mixer",
    )(x, *consts, w_upg, w_down)


def _conv_gelu_gate(u, gt, u1, u2, cw, cb):
    c = cb + cw[0:1, :] * u2
    c = c + cw[1:2, :] * u1
    c = c + cw[2:3, :] * u
    return jax.nn.gelu(c, approximate=True) * gt


def _ffn_kernel(x_ref, gpre_ref, gpost_ref, wupg_ref, cw_ref, cb_ref, wdown_ref, never_ref,
                x2_ref, tail_ref, carry_s, next_s, *, tile):
    i = pl.program_id(1)

    @pl.when(i == 0)
    def _():
        carry_s[...] = jnp.zeros_like(carry_s)

    @pl.when(i > 0)
    def _():
        carry_s[...] = next_s[...]

    x1 = x_ref[...]
    xn = _rms(x1, gpre_ref[...]).astype(BF16)
    never = never_ref[...] != 0
    row = lax.broadcasted_iota(jnp.int32, (tile, FF_CHUNK), 0)
    acc = jnp.zeros((tile, D_MODEL), F32)

    def up_gate(c):
        cols = slice(FF_CHUNK * c, FF_CHUNK * (c + 1))
        gcols = slice(D_FF + FF_CHUNK * c, D_FF + FF_CHUNK * (c + 1))
        return _dot(xn, wupg_ref[:, cols]), _dot(xn, wupg_ref[:, gcols])

    ahead = [up_gate(c) for c in range(min(FF_LOOKAHEAD, N_FF))]
    for c in range(N_FF):
        u, gt = ahead.pop(0)
        cols = slice(FF_CHUNK * c, FF_CHUNK * (c + 1))
        prev = carry_s[c]
        u1 = jnp.where(row == 0, prev[7:8, :], pltpu.roll(u, 1, 0))
        u2 = jnp.where(row == 0, prev[6:7, :], jnp.where(row == 1, prev[7:8, :], pltpu.roll(u, 2, 0)))
        next_s[c] = u[tile - 8:tile, :]
        tail_ref[:, cols] = u[tile - 2:tile, :]
        act = _conv_gelu_gate(u, gt, u1, u2, cw_ref[:, cols], cb_ref[:, cols])
        if c + FF_LOOKAHEAD < N_FF:
            ahead.append(up_gate(c + FF_LOOKAHEAD))
            act = _add_to_corner(act, _ordering_zero(ahead[-1][0][0:BF16_ROWS, 0:LANES], never))
        acc = acc + _dot(act.astype(BF16), wdown_ref[cols, :])
    x2_ref[...] = x1 + _rms(acc, gpost_ref[...])


def _prompt_ffn(x1, gpre, gpost, w):
    bsz, seq, _ = x1.shape
    tile = SEQ_TILE
    assert seq % tile == 0
    row_spec = pl.BlockSpec((None, tile, D_MODEL), lambda b, i: (b, i, 0))
    consts = [gpre, gpost, w["upg"], w["cw"], w["cb"], w["down"], jnp.zeros((BF16_ROWS, LANES), jnp.int32)]
    return pl.pallas_call(
        functools.partial(_ffn_kernel, tile=tile),
        grid=(bsz, seq // tile),
        in_specs=[row_spec] + [_const_spec(c.shape) for c in consts],
        out_specs=(row_spec, pl.BlockSpec((None, 2, D_FF), lambda b, i: (b, 0, 0))),
        out_shape=(jax.ShapeDtypeStruct((bsz, seq, D_MODEL), F32),
                   jax.ShapeDtypeStruct((bsz, 2, D_FF), F32)),
        scratch_shapes=[pltpu.VMEM((N_FF, 8, FF_CHUNK), F32), pltpu.VMEM((N_FF, 8, FF_CHUNK), F32)],
        compiler_params=pltpu.CompilerParams(
            dimension_semantics=("arbitrary", "arbitrary"), vmem_limit_bytes=VMEM_LIMIT),
        name="prompt_ffn",
    )(x1, *consts)


def _sample_proj_kernel(x_ref, g_ref, w_ref, p_ref):
    h = _rms(x_ref[...], g_ref[...]).astype(BF16)
    p_ref[...] = _dot(h, w_ref[...])


def _sample_proj(xs, gpre, w_in_b):
    rows = xs.shape[0]
    nblk = 2
    width = IN_WIDTH // nblk
    assert width % LANES == 0
    return pl.pallas_call(
        _sample_proj_kernel,
        grid=(nblk,),
        in_specs=[pl.BlockSpec((rows, D_MODEL), lambda j: (0, 0)),
                  pl.BlockSpec((1, D_MODEL), lambda j: (0, 0)),
                  pl.BlockSpec((D_MODEL, width), lambda j: (0, j))],
        out_specs=pl.BlockSpec((rows, width), lambda j: (0, j)),
        out_shape=jax.ShapeDtypeStruct((rows, IN_WIDTH), F32),
        compiler_params=pltpu.CompilerParams(dimension_semantics=("arbitrary",)),
        name="sample_proj",
    )(xs, gpre, w_in_b)


def _sample_attn_kernel(sink_ref, p_ref, cak_ref, cav_ref, cbk_ref, cbv_ref, bsa_ref, bsb_ref, never_ref,
                        oa_ref, ob_ref, kan_ref, van_ref, kbn_ref, vbn_ref, ka_s, va_s, kb_s, vb_s, *, s, la, lb):
    p = p_ref[...]
    kan_ref[...] = p[:, _C_KA:_C_KA + A_KVW]
    van_ref[...] = p[:, _C_VA:_C_VA + A_KVW]
    kbn_ref[...] = p[:, _C_KB:_C_KB + B_W]
    vbn_ref[...] = p[:, _C_VB:_C_VB + B_W]
    ka_pad = ka_s.shape[0]
    kb_pad = kb_s.shape[0]

    def fill(dst, cache_ref, new, n_cache, n_pad):
        dst[0:n_cache, :] = cache_ref[...].astype(BF16)
        dst[n_cache:n_cache + s, :] = new.astype(BF16)
        dst[n_cache + s:n_pad, :] = jnp.zeros((n_pad - n_cache - s, dst.shape[1]), BF16)

    fill(ka_s, cak_ref, p[:, _C_KA:_C_KA + A_KVW], la, ka_pad)
    fill(va_s, cav_ref, p[:, _C_VA:_C_VA + A_KVW], la, ka_pad)
    fill(kb_s, cbk_ref, p[:, _C_KB:_C_KB + B_W], lb, kb_pad)
    fill(vb_s, cbv_ref, p[:, _C_VB:_C_VB + B_W], lb, kb_pad)

    lo = lax.broadcasted_iota(jnp.int32, (s, LANES), 1) < HEAD_DIM
    top = lax.broadcasted_iota(jnp.int32, (2 * s, 1), 0) < s
    never = never_ref[...] != 0
    ka = ka_s[...]
    va = va_s[...]

    def scores(step):
        kind, j = step
        if kind == "a":
            q = p[:, _C_QA + LANES * j:_C_QA + LANES * (j + 1)] * QK_SCALE
            qr = pltpu.roll(q, HEAD_DIM, 1)
            if 2 * j < A_GROUP:
                qs = jnp.concatenate([jnp.where(lo, q, 0.0), jnp.where(lo, qr, 0.0)], axis=0)
            else:
                qs = jnp.concatenate([jnp.where(lo, 0.0, qr), jnp.where(lo, 0.0, q)], axis=0)
            bias = jnp.concatenate([bsa_ref[2 * j], bsa_ref[2 * j + 1]], axis=0)
            return _dot_nt(qs.astype(BF16), ka) + bias
        q = p[:, _C_QB + LANES * j:_C_QB + LANES * (j + 1)] * QK_SCALE
        qs = jnp.concatenate([jnp.where(lo, q, 0.0), jnp.where(lo, 0.0, q)], axis=0)
        bias = jnp.concatenate([bsb_ref[2 * j], bsb_ref[2 * j + 1]], axis=0)
        return _dot_nt(qs.astype(BF16), kb_s[:, LANES * j:LANES * (j + 1)]) + bias

    def softmax_pv(st, v, sink, order_after):
        m = jnp.max(st, axis=-1, keepdims=True)
        if sink is not None:
            m = jnp.maximum(m, sink)
        e = jnp.exp(st - m)
        l = jnp.sum(e, axis=-1, keepdims=True)
        if sink is not None:
            l = l + jnp.exp(sink - m)
        if order_after is not None:
            e = _add_to_corner(e, order_after)
        return _dot(e.astype(BF16), v) * (1.0 / l)

    steps = [("a", j) for j in range(A_HEADS // 2)] + [("b", j) for j in range(B_HEADS // 2)]
    st_next = scores(steps[0])
    for k, (kind, j) in enumerate(steps):
        st, order_after = st_next, None
        if k + 1 < len(steps):
            st_next = scores(steps[k + 1])
            order_after = _ordering_zero(st_next[0:BF16_ROWS, 0:LANES], never)
        lanes = slice(LANES * j, LANES * (j + 1))
        if kind == "a":
            sink = jnp.where(top, sink_ref[2 * j], sink_ref[2 * j + 1])
            o = softmax_pv(st, va, sink, order_after)
            oe, oo = o[0:s], o[s:2 * s]
            if 2 * j < A_GROUP:
                oa_ref[:, lanes] = jnp.where(lo, oe, pltpu.roll(oo, HEAD_DIM, 1))
            else:
                oa_ref[:, lanes] = jnp.where(lo, pltpu.roll(oe, HEAD_DIM, 1), oo)
        else:
            o = softmax_pv(st, vb_s[:, lanes], None, order_after)
            ob_ref[:, lanes] = jnp.where(lo, o[0:s], o[s:2 * s])


def _sample_attn(sink, p, cak, cav, cbk, cbv, bsa, bsb, s):
    nb, la, _ = cak.shape
    lb = cbk.shape[1]
    ka_pad, kb_pad = bsa.shape[-1], bsb.shape[-1]
    assert s % 16 == 0 and la % 16 == 0 and lb % 16 == 0
    row = lambda w: pl.BlockSpec((s, w), lambda b: (b, 0))
    cache = lambda n, w: pl.BlockSpec((None, n, w), lambda b: (b, 0, 0))
    full = lambda a: pl.BlockSpec(a.shape, lambda b: (0,) * a.ndim)
    widths = (A_QW, B_W, A_KVW, A_KVW, B_W, B_W)
    return pl.pallas_call(
        functools.partial(_sample_attn_kernel, s=s, la=la, lb=lb),
        grid=(nb,),
        in_specs=[pl.BlockSpec(memory_space=pltpu.SMEM), row(IN_WIDTH),
                  cache(la, A_KVW), cache(la, A_KVW), cache(lb, B_W), cache(lb, B_W), full(bsa), full(bsb),
                  pl.BlockSpec((BF16_ROWS, LANES), lambda b: (0, 0))],
        out_specs=tuple(row(w) for w in widths),
        out_shape=tuple(jax.ShapeDtypeStruct((nb * s, w), F32) for w in widths),
        scratch_shapes=[pltpu.VMEM((ka_pad, A_KVW), BF16), pltpu.VMEM((ka_pad, A_KVW), BF16),
                        pltpu.VMEM((kb_pad, B_W), BF16), pltpu.VMEM((kb_pad, B_W), BF16)],
        compiler_params=pltpu.CompilerParams(dimension_semantics=("arbitrary",)),
        name="sample_attn",
    )(sink, p, cak, cav, cbk, cbv, bsa, bsb, jnp.zeros((BF16_ROWS, LANES), jnp.int32))


def _sample_tail_kernel(x_ref, oa_ref, ob_ref, p_ref, woa_ref, wob_ref, wout_ref,
                        gpost_ref, gpre2_ref, gpost2_ref, init_ref,
                        wup_ref, wgate_ref, cw_ref, cb_ref, wdown_ref,
                        x2_ref, u_ref, x1_s, xn_s, acc_s, *, s):
    c = pl.program_id(0)

    @pl.when(c == 0)
    def _():
        ya = _dot(oa_ref[...].astype(BF16), woa_ref[...])
        yb = _dot(ob_ref[...].astype(BF16), wob_ref[...])
        ga = p_ref[:, _C_GA:_C_GA + D_MODEL]
        gb = p_ref[:, _C_GB:_C_GB + D_MODEL]
        merged = jax.nn.sigmoid(ga) * ya + jax.nn.sigmoid(gb) * yb
        z = _dot(merged.astype(BF16), wout_ref[...])
        x1 = x_ref[...] + _rms(z, gpost_ref[...])
        x1_s[...] = x1
        xn_s[...] = _rms(x1, gpre2_ref[...]).astype(BF16)
        acc_s[...] = jnp.zeros_like(acc_s)

    xn = xn_s[...]
    u = _dot(xn, wup_ref[...])
    gt = _dot(xn, wgate_ref[...])
    u_ref[...] = u
    pos = lax.broadcasted_iota(jnp.int32, u.shape, 0) % s
    init = init_ref[...]
    u1 = jnp.where(pos == 0, pltpu.roll(init, u.shape[0] - 1, 0), pltpu.roll(u, 1, 0))
    u2 = jnp.where(pos < 2, init, pltpu.roll(u, 2, 0))
    act = _conv_gelu_gate(u, gt, u1, u2, cw_ref[...], cb_ref[...])
    acc_s[...] += _dot(act.astype(BF16), wdown_ref[...])

    @pl.when(c == pl.num_programs(0) - 1)
    def _():
        x2_ref[...] = x1_s[...] + _rms(acc_s[...], gpost2_ref[...])


def _sample_tail(xs, oa, ob, p, w, gpost, gpre2, gpost2, init, s):
    rows = xs.shape[0]
    full = lambda a: pl.BlockSpec(a.shape, lambda c: (0,) * a.ndim)
    cols = lambda r: pl.BlockSpec((r, FF_CHUNK), lambda c: (0, c))
    gate_cols = pl.BlockSpec((D_MODEL, FF_CHUNK), lambda c: (0, N_FF + c))
    down_rows = pl.BlockSpec((FF_CHUNK, D_MODEL), lambda c: (c, 0))
    args = [xs, oa, ob, p, w["oa"], w["ob"], w["out"], gpost, gpre2, gpost2]
    return pl.pallas_call(
        functools.partial(_sample_tail_kernel, s=s),
        grid=(N_FF,),
        in_specs=[full(a) for a in args] + [cols(rows), cols(D_MODEL), gate_cols, cols(3), cols(1), down_rows],
        out_specs=(full(xs), cols(rows)),
        out_shape=(jax.ShapeDtypeStruct((rows, D_MODEL), F32), jax.ShapeDtypeStruct((rows, D_FF), F32)),
        scratch_shapes=[pltpu.VMEM((rows, D_MODEL), F32), pltpu.VMEM((rows, D_MODEL), BF16),
                        pltpu.VMEM((rows, D_MODEL), F32)],
        compiler_params=pltpu.CompilerParams(dimension_semantics=("arbitrary",)),
        name="sample_tail",
    )(*args, init, w["upg"], w["upg"], w["cw"], w["cb"], w["down"])


def _prep_weights(w_in, w_oa, w_ob, w_out, conv_w, conv_b):
    wb = w_in.astype(BF16)
    qv = jnp.concatenate([wb[:, _C_QA:_C_QA + A_QW], wb[:, _C_QB:_C_QB + B_W],
                          wb[:, _C_VA:_C_VA + A_KVW], wb[:, _C_VB:_C_VB + B_W],
                          wb[:, _C_KA:_C_KA + A_KVW]], axis=1)
    return {
        "in": wb, "qvt": qv.T,
        "oa": w_oa.astype(BF16), "ob": w_ob.astype(BF16), "out": w_out.astype(BF16),
        "cw": conv_w, "cb": conv_b.reshape(1, D_FF),
    }


def _round_up(n, m):
    return (n + m - 1) // m * m


def kernel(x_prompt, x_sample, cache_a_k, cache_a_v, cache_b_k, cache_b_v, state_conv,
           w_in, w_oa, w_ob, w_out, sink_a, t5_table, rel_table_b,
           g_pre_mix, g_post_mix, g_pre_ffn, g_post_ffn, w_upg, conv_w, conv_b, w_down):
    depth = w_in.shape[0]
    bsz, seq, _ = x_prompt.shape
    nb, s, _ = x_sample.shape
    la, lb = cache_a_k.shape[2], cache_b_k.shape[2]
    ka_pad = _round_up(la + s, LANES)
    kb_pad = _round_up(lb + s, LANES)

    xp = x_prompt
    xs = x_sample.reshape(nb * s, D_MODEL)
    prompt_states = [[] for _ in range(5)]
    sample_states = [[] for _ in range(5)]
    for l in range(depth):
        w = _prep_weights(w_in[l], w_oa[l], w_ob[l], w_out[l], conv_w[l], conv_b[l])
        row = lambda g: g[l].reshape(1, D_MODEL)
        bta, btb, bsa, bsb = _build_bias(t5_table, rel_table_b[l], s, la, lb, ka_pad, kb_pad)
        sink_row = jnp.repeat(sink_a[l], GROUP).reshape(1, A_HEADS * GROUP)

        x1, ka_t, va_t, kb_t, vb_t, w["upg"], w["down"] = _prompt_mixer(
            xp, row(g_pre_mix), row(g_post_mix), w, sink_row, bta, btb, w_upg[l], w_down[l])
        xp, conv_t = _prompt_ffn(x1, row(g_pre_ffn), row(g_post_ffn), w)
        prompt_states[0].append(ka_t.reshape(bsz, A_PREV, A_KV_HEADS, HEAD_DIM))
        prompt_states[1].append(va_t.reshape(bsz, A_PREV, A_KV_HEADS, HEAD_DIM))
        prompt_states[2].append(kb_t.reshape(bsz, B_PREV, B_HEADS, HEAD_DIM))
        prompt_states[3].append(vb_t.reshape(bsz, B_PREV, B_HEADS, HEAD_DIM))
        prompt_states[4].append(conv_t)

        p = _sample_proj(xs, row(g_pre_mix), w["in"])
        oa, ob, ka_n, va_n, kb_n, vb_n = _sample_attn(
            sink_a[l], p, cache_a_k[l].reshape(nb, la, A_KVW), cache_a_v[l].reshape(nb, la, A_KVW),
            cache_b_k[l].reshape(nb, lb, B_W), cache_b_v[l].reshape(nb, lb, B_W), bsa, bsb, s)
        init = jnp.pad(state_conv[l], ((0, 0), (0, s - 2), (0, 0))).reshape(nb * s, D_FF)
        xs, u = _sample_tail(xs, oa, ob, p, w, row(g_post_mix), row(g_pre_ffn), row(g_post_ffn), init, s)
        sample_states[0].append(ka_n.reshape(nb, s, A_KV_HEADS, HEAD_DIM))
        sample_states[1].append(va_n.reshape(nb, s, A_KV_HEADS, HEAD_DIM))
        sample_states[2].append(kb_n.reshape(nb, s, B_HEADS, HEAD_DIM))
        sample_states[3].append(vb_n.reshape(nb, s, B_HEADS, HEAD_DIM))
        sample_states[4].append(u.reshape(nb, s, D_FF)[:, -2:])

    ps = [jnp.stack(a, axis=0) for a in prompt_states]
    ss = [jnp.stack(a, axis=0) for a in sample_states]
    return (xp, xs.reshape(nb, s, D_MODEL), ps[0], ps[1], ps[2], ps[3], ps[4],
            ss[0], ss[1], ss[2], ss[3], ss[4])
```

```python
import functools
import math

import numpy as np
import jax
import jax.numpy as jnp
from jax import lax
from jax.experimental import pallas as pl
from jax.experimental.pallas import tpu as pltpu

F32 = jnp.float32
BF16 = jnp.bfloat16

D_MODEL = 1024
CHUNK = 64
HEAD_DIM = 64
EPS = 1e-6
NEG_INF = -1e30
QK_SCALE = HEAD_DIM ** -0.5
LOG2E = math.log2(math.e)

A_HEADS = 8
A_KV_HEADS = 2
A_GROUP = A_HEADS // A_KV_HEADS
A_PREV = 128
T5_BUCKETS = 32
T5_MAX_DIST = 128

B_HEADS = 8
B_PREV = 512
B_REL_CLIP = 128

A_QW = A_HEADS * HEAD_DIM
A_KVW = A_KV_HEADS * HEAD_DIM
B_W = B_HEADS * HEAD_DIM
D_FF = 3072

_C_QA = 0
_C_KA = _C_QA + A_QW
_C_VA = _C_KA + A_KVW
_C_QB = _C_VA + A_KVW
_C_KB = _C_QB + B_W
_C_VB = _C_KB + B_W
_C_GA = _C_VB + B_W
_C_GB = _C_GA + D_MODEL
IN_WIDTH = _C_GB + D_MODEL

LANES = 128
BF16_ROWS = 16
MXU_COLS = 256
GROUP = 2 * CHUNK
A_WIN = A_PREV + GROUP
B_WIN = B_PREV + GROUP
SEQ_TILE = 512
FF_CHUNK = 512
N_FF = D_FF // FF_CHUNK
SAMPLE_FF_SUB = 3
FF_LOOKAHEAD = 4
VMEM_LIMIT = 56 * 1024 * 1024
LOOKAHEAD = 1
GATE_FIRST_STEP = 1
GATE_STEP_STRIDE = 3


def _rms(x, g):
    y = x * lax.rsqrt(jnp.mean(x * x, axis=-1, keepdims=True) + EPS)
    return y * g


def _dot(a, b):
    return jnp.dot(a, b, preferred_element_type=F32)


def _dot_nt(a, b):
    return lax.dot_general(a, b, (((1,), (1,)), ((), ())), preferred_element_type=F32)


def _t5_bucket(n):
    half = T5_BUCKETS // 2
    max_exact = half // 2
    ret = jnp.where(n < 0, half, 0)
    a = jnp.abs(n)
    af = jnp.maximum(a, 1).astype(jnp.float32)
    large = max_exact + (jnp.log(af / max_exact) / math.log(T5_MAX_DIST / max_exact)
                         * (half - max_exact)).astype(jnp.int32)
    large = jnp.minimum(large, half - 1)
    return ret + jnp.where(a < max_exact, a, large)


def _window_valid(n_prev, win):
    j = np.arange(win)[:, None]
    ci = np.arange(GROUP)[None, :] // CHUNK
    return ((j >= CHUNK * ci) & (j < CHUNK * ci + n_prev + CHUNK)).astype(np.int32)


def _lookup(idx, table, lo, hi):
    acc = jnp.zeros((table.shape[0], idx.shape[1]), F32)
    for b in range(lo, hi):
        acc = jnp.where(idx == b, table[:, b:b + 1], acc)
    return acc


def _toeplitz(vec, rows, first):
    w = vec.shape[1]
    return pltpu.roll(jnp.broadcast_to(vec, (rows, w)), (w - first) % w, 1, stride=1, stride_axis=0)


def _bias_kernel(t5_ref, rel_ref, ja_ref, jb_ref, jsa_ref, jsb_ref, va_ref, vb_ref, vsa_ref, vsb_ref,
                 bta_ref, btb_ref, bsa_ref, bsb_ref, *, s, jb_range, jsb_range):
    t5 = t5_ref[...]
    rel = rel_ref[...]

    fa = _lookup(ja_ref[...], t5, 0, T5_BUCKETS) * LOG2E
    va = va_ref[...] != 0
    for h in range(A_HEADS):
        t = _toeplitz(fa[h:h + 1, :], A_WIN, A_WIN - 1)[:, 0:GROUP]
        bta_ref[:, GROUP * h:GROUP * (h + 1)] = jnp.where(va, t, NEG_INF)

    fb = _lookup(jb_ref[...], rel, *jb_range) * LOG2E
    for r in range(B_WIN // LANES):
        base = B_WIN - LANES * r
        vb = vb_ref[LANES * r:LANES * (r + 1), :] != 0
        for h in range(B_HEADS):
            seg = fb[h:h + 1, base - LANES:base + LANES]
            t = _toeplitz(seg, LANES, LANES)[:, 0:GROUP]
            btb_ref[h, LANES * r:LANES * (r + 1), :] = jnp.where(vb, t, NEG_INF)

    fsa = _lookup(jsa_ref[...], t5, 0, T5_BUCKETS)
    fsb = _lookup(jsb_ref[...], rel, *jsb_range)
    vsa = vsa_ref[...] != 0
    vsb = vsb_ref[...] != 0
    for h in range(A_HEADS):
        bsa_ref[h] = jnp.where(vsa, _toeplitz(fsa[h:h + 1, :], s, s - 1), NEG_INF)
    for h in range(B_HEADS):
        bsb_ref[h] = jnp.where(vsb, _toeplitz(fsb[h:h + 1, :], s, s - 1), NEG_INF)


def _build_bias(t5_table, rel_table, s, la, lb, ka_pad, kb_pad):
    wa = 3 * LANES
    assert A_WIN - 1 + GROUP <= wa and la + 2 * s - 2 < ka_pad and lb + 2 * s - 2 < kb_pad
    bucket = lambda n: (_t5_bucket(n).astype(jnp.int32) & (T5_BUCKETS - 1)).reshape(1, -1)
    ja = bucket(jnp.arange(wa) - (A_WIN - 1) + A_PREV)
    jb = np.clip(np.arange(B_WIN + GROUP) - B_WIN + B_PREV, -B_REL_CLIP, B_REL_CLIP) + B_REL_CLIP
    jsa = bucket(la + s - 1 - jnp.arange(ka_pad))
    jsb = np.clip(lb + s - 1 - np.arange(kb_pad), -B_REL_CLIP, B_REL_CLIP) + B_REL_CLIP
    key = lambda n: np.broadcast_to(np.arange(n)[None, :], (s, n))
    vsa = (key(ka_pad) < la + s).astype(np.int32)
    vsb = (key(kb_pad) < lb + s).astype(np.int32)
    as_row = lambda v: jnp.asarray(v.astype(np.int32).reshape(1, -1))
    rng = lambda v: (int(v.min()), int(v.max()) + 1)

    vmem = pl.BlockSpec(memory_space=pltpu.VMEM)
    return pl.pallas_call(
        functools.partial(_bias_kernel, s=s, jb_range=rng(jb), jsb_range=rng(jsb)),
        out_shape=(
            jax.ShapeDtypeStruct((A_WIN, A_HEADS * GROUP), F32),
            jax.ShapeDtypeStruct((B_HEADS, B_WIN, GROUP), F32),
            jax.ShapeDtypeStruct((A_HEADS, s, ka_pad), F32),
            jax.ShapeDtypeStruct((B_HEADS, s, kb_pad), F32),
        ),
        in_specs=[vmem] * 10,
        out_specs=(vmem,) * 4,
        name="bias_tables",
    )(t5_table, rel_table, ja, as_row(jb), jsa, as_row(jsb),
      jnp.asarray(_window_valid(A_PREV, A_WIN)), jnp.asarray(_window_valid(B_PREV, B_WIN)),
      jnp.asarray(vsa), jnp.asarray(vsb))


def _ordering_zero(x, never):
    return jnp.where(never, x, 0.0)


def _add_to_corner(x, z, cols=(0,)):
    pieces, at = [], 0
    for c in cols:
        pieces += [x[0:BF16_ROWS, at:c], x[0:BF16_ROWS, c:c + LANES] + z]
        at = c + LANES
    pieces.append(x[0:BF16_ROWS, at:])
    head = jnp.concatenate([q for q in pieces if q.shape[1]], axis=1)
    return jnp.concatenate([head, x[BF16_ROWS:, :]], axis=0)


def _scores_a(g, ka_s, qt_s):
    tok = slice(GROUP * g, GROUP * (g + 1))
    kwin = ka_s[GROUP * g:GROUP * g + A_WIN, :]
    q = qt_s[0:A_QW, tok]
    zero = jnp.zeros((HEAD_DIM, GROUP), BF16)
    cols = []
    for i in range(A_HEADS):
        qi = q[HEAD_DIM * i:HEAD_DIM * (i + 1), :]
        cols.append(jnp.concatenate([qi, zero] if i < A_GROUP else [zero, qi], axis=0))
    return _dot(kwin, jnp.concatenate(cols, axis=1))


def _scores_b(g, pp, kb_s, qt_s):
    tok = slice(GROUP * g, GROUP * (g + 1))
    kwin = kb_s[GROUP * g:GROUP * g + B_WIN, 2 * LANES * pp:2 * LANES * (pp + 1)]
    qrow = A_QW + 2 * LANES * pp
    zero = jnp.zeros((HEAD_DIM, GROUP), BF16)
    cols = []
    for i in range(4):
        qi = qt_s[qrow + HEAD_DIM * i:qrow + HEAD_DIM * (i + 1), tok]
        cols.append(jnp.concatenate([zero] * i + [qi] + [zero] * (3 - i), axis=0))
    return _dot(kwin, jnp.concatenate(cols, axis=1))


def _softmax_keys(st, bias, sink_row, order_after, order_cols=(0,)):
    st = st + bias
    m = jnp.max(st, axis=0, keepdims=True)
    if sink_row is not None:
        m = jnp.maximum(m, sink_row)
    p = jnp.exp2(st - m)
    if order_after is not None:
        p = _add_to_corner(p, order_after, order_cols)
    return p.astype(BF16), m


def _weighted_values(vwin, present, p, m, sink_row):
    ot = _dot(jnp.concatenate([vwin, present], axis=0), p)
    l = ot[2 * HEAD_DIM:2 * HEAD_DIM + 1, :]
    if sink_row is not None:
        l = l + jnp.exp2(sink_row - m)
    return ot[0:2 * HEAD_DIM, :] * (1.0 / l)


def _values_a(g, vat_s, present_s, p, m, sink_row, oa_s):
    tok = slice(GROUP * g, GROUP * (g + 1))
    c0 = B_PREV - A_PREV + GROUP * g
    ot = _weighted_values(vat_s[:, GROUP * g:GROUP * g + A_WIN], present_s[:, c0:c0 + A_WIN], p, m, sink_row)
    for j in range(A_HEADS // 2):
        parts = []
        for i in (2 * j, 2 * j + 1):
            rows = slice(0, HEAD_DIM) if i < A_GROUP else slice(HEAD_DIM, 2 * HEAD_DIM)
            parts.append(ot[rows, GROUP * i:GROUP * (i + 1)])
        blk = jnp.concatenate(parts, axis=0)
        oa_s[tok, LANES * j:LANES * (j + 1)] = blk.astype(BF16).T


def _values_b(g, p2, vbt_s, present_s, p, m, ob_s):
    tok = slice(GROUP * g, GROUP * (g + 1))
    keys = slice(GROUP * g, GROUP * g + B_WIN)
    lanes = slice(LANES * p2, LANES * (p2 + 1))
    ot = _weighted_values(vbt_s[lanes, keys], present_s[:, keys], p, m, None)
    blk = jnp.concatenate([ot[0:HEAD_DIM, 0:GROUP], ot[HEAD_DIM:2 * HEAD_DIM, GROUP:2 * GROUP]], axis=0)
    ob_s[tok, lanes] = blk.astype(BF16).T


def _attend_tile(n_groups, never, ka_s, vat_s, kb_s, vbt_s, present_s, qt_s, bta_ref, btb_ref, sink_row, oa_s, ob_s,
                 fillers):
    steps = []
    for g in range(n_groups):
        steps.append(("a", g, 0))
        steps.extend(("b", g, pp) for pp in range(B_HEADS // 4))

    def scores(step):
        kind, g, pp = step
        return _scores_a(g, ka_s, qt_s) if kind == "a" else _scores_b(g, pp, kb_s, qt_s)

    ahead = [scores(step) for step in steps[:LOOKAHEAD]]
    for k, (kind, g, pp) in enumerate(steps):
        st, order_after = ahead.pop(0), None
        if k + LOOKAHEAD < len(steps):
            ahead.append(scores(steps[k + LOOKAHEAD]))
            order_after = _ordering_zero(ahead[-1][0:BF16_ROWS, 0:LANES], never)
        if fillers.get(k) is not None:
            z = _ordering_zero(fillers[k](), never)
            order_after = z if order_after is None else order_after + z
        if kind == "a":
            p, m = _softmax_keys(st, bta_ref[...], sink_row, order_after)
            _values_a(g, vat_s, present_s, p, m, sink_row, oa_s)
        else:
            bias = jnp.concatenate([btb_ref[4 * pp + i] for i in range(4)], axis=1)
            p, m = _softmax_keys(st, bias, None, order_after, (0, 2 * GROUP))
            for j in range(2):
                cols = slice(2 * GROUP * j, 2 * GROUP * (j + 1))
                _values_b(g, 2 * pp + j, vbt_s, present_s, p[:, cols], m[:, cols], ob_s)


def _mixer_kernel(x_ref, gpre_ref, gpost_ref, win_ref, wt_ref,
                  woa_ref, wob_ref, wout_ref, sink_ref, bta_ref, btb_ref, never_ref, wupg_ref, wdown_ref,
                  x1_ref, kat_ref, vat_ref, kbt_ref, vbt_ref, wupg_b_ref, wdown_b_ref,
                  h_s, qt_s, ka_s, vat_s, kb_s, vbt_s, present_s, oa_s, ob_s, mg_s, sg_s,
                  ka_f, va_f, kb_f, vb_f, *, tile, n_tiles):
    i = pl.program_id(1)

    wupg_b_ref[...] = wupg_ref[...].astype(BF16)
    wdown_b_ref[...] = wdown_ref[...].astype(BF16)

    @pl.when(i == 0)
    def _():
        ka_s[0:A_PREV, :] = jnp.zeros((A_PREV, A_KVW), BF16)
        vat_s[:, 0:A_PREV] = jnp.zeros((A_KVW, A_PREV), BF16)
        kb_s[0:B_PREV, :] = jnp.zeros((B_PREV, B_W), BF16)
        vbt_s[:, 0:B_PREV] = jnp.zeros((B_W, B_PREV), BF16)
        present_s[:, 0:B_PREV] = jnp.zeros((BF16_ROWS, B_PREV), BF16)

    @pl.when(i > 0)
    def _():
        ka_s[0:A_PREV, :] = ka_s[tile:tile + A_PREV, :]
        vat_s[:, 0:A_PREV] = vat_s[:, tile:tile + A_PREV]
        kb_s[0:B_PREV, :] = kb_s[tile:tile + B_PREV, :]
        vbt_s[:, 0:B_PREV] = vbt_s[:, tile:tile + B_PREV]
        present_s[:, 0:B_PREV] = present_s[:, tile:tile + B_PREV]

    x = x_ref[...]
    h = _rms(x, gpre_ref[...]).astype(BF16)
    h_s[...] = h

    kb = _dot(h, win_ref[:, _C_KB:_C_KB + B_W])
    kb_s[B_PREV:B_PREV + tile, :] = kb.astype(BF16)
    nq = A_QW + B_W
    pt = _dot_nt(wt_ref[...], h)
    qt_s[...] = (pt[0:nq, :] * (QK_SCALE * LOG2E)).astype(BF16)
    vat = pt[nq:nq + A_KVW, :]
    vbt = pt[nq + A_KVW:nq + A_KVW + B_W, :]
    ka = pt[nq + A_KVW + B_W:, :].T
    ka_s[A_PREV:A_PREV + tile, :] = ka.astype(BF16)
    vat_s[:, A_PREV:A_PREV + tile] = vat.astype(BF16)
    vbt_s[:, B_PREV:B_PREV + tile] = vbt.astype(BF16)
    first_row = lax.broadcasted_iota(jnp.int32, (BF16_ROWS, tile), 0) == 0
    present_s[:, B_PREV:B_PREV + tile] = jnp.where(first_row, 1.0, 0.0).astype(BF16)
    ka_f[...] = ka[tile - A_PREV:, :]
    kb_f[...] = kb[tile - B_PREV:, :]
    va_f[...] = vat[:, tile - A_PREV:]
    vb_f[...] = vbt[:, tile - B_PREV:]

    gate_w = 2 * MXU_COLS
    n_steps = (tile // GROUP) * (1 + B_HEADS // 4)

    def gate_piece(j):
        def run():
            g = _dot(h_s[...], win_ref[:, _C_GA + gate_w * j:_C_GA + gate_w * (j + 1)])
            sg_s[:, gate_w * j:gate_w * (j + 1)] = jax.nn.sigmoid(g)
            return g[0:BF16_ROWS, 0:LANES]
        return run

    n_pieces = 2 * D_MODEL // gate_w
    fillers = {GATE_FIRST_STEP + GATE_STEP_STRIDE * j: gate_piece(j) for j in range(n_pieces)}
    assert max(fillers) < n_steps
    _attend_tile(tile // GROUP, never_ref[...] != 0, ka_s, vat_s, kb_s, vbt_s, present_s, qt_s,
                 bta_ref, btb_ref, sink_ref[...] * LOG2E, oa_s, ob_s, fillers)

    oa = oa_s[...]
    ob = ob_s[...]
    cw = 2 * MXU_COLS
    for c in range(D_MODEL // cw):
        cs = slice(cw * c, cw * (c + 1))
        ya = _dot(oa, woa_ref[:, cs])
        yb = _dot(ob, wob_ref[:, cs])
        sgb = sg_s[:, D_MODEL + cw * c:D_MODEL + cw * (c + 1)]
        mg_s[:, cs] = (sg_s[:, cs] * ya + sgb * yb).astype(BF16)
    z = _dot(mg_s[...], wout_ref[...])
    x1_ref[...] = x_ref[...] + _rms(z, gpost_ref[...])

    @pl.when(i == n_tiles - 1)
    def _():
        kat_ref[...] = ka_f[...]
        kbt_ref[...] = kb_f[...]
        vat_ref[...] = va_f[...].T
        vbt_ref[...] = vb_f[...].T


def _const_spec(shape):
    nd = len(shape)
    return pl.BlockSpec(shape, lambda *_: (0,) * nd, pipeline_mode=pl.Buffered(1))


def _prompt_mixer(x, gpre, gpost, w, sink_row, bta, btb, w_upg, w_down):
    bsz, seq, _ = x.shape
    tile = SEQ_TILE
    assert seq % tile == 0 and tile % GROUP == 0 and tile >= B_PREV
    n_tiles = seq // tile
    steps = bsz * n_tiles
    up_rows, down_rows = w_upg.shape[0] // steps, w_down.shape[0] // steps
    assert up_rows * steps == w_upg.shape[0] and down_rows * steps == w_down.shape[0]
    assert up_rows % 16 == 0 and down_rows % 16 == 0
    step_rows = lambda rows, width: pl.BlockSpec((rows, width), lambda b, i: (b * n_tiles + i, 0))
    row_spec = pl.BlockSpec((None, tile, D_MODEL), lambda b, i: (b, i, 0))

    def tail_spec(rows, width):
        return pl.BlockSpec((None, rows, width), lambda b, i: (b, 0, 0))

    consts = [gpre, gpost, w["in"], w["qvt"],
              w["oa"], w["ob"], w["out"], sink_row, bta, btb, jnp.zeros((BF16_ROWS, LANES), jnp.int32)]
    return pl.pallas_call(
        functools.partial(_mixer_kernel, tile=tile, n_tiles=n_tiles),
        grid=(bsz, n_tiles),
        in_specs=[row_spec] + [_const_spec(c.shape) for c in consts]
        + [step_rows(up_rows, w_upg.shape[1]), step_rows(down_rows, w_down.shape[1])],
        out_specs=(row_spec, tail_spec(A_PREV, A_KVW), tail_spec(A_PREV, A_KVW),
                   tail_spec(B_PREV, B_W), tail_spec(B_PREV, B_W),
                   step_rows(up_rows, w_upg.shape[1]), step_rows(down_rows, w_down.shape[1])),
        out_shape=(
            jax.ShapeDtypeStruct((bsz, seq, D_MODEL), F32),
            jax.ShapeDtypeStruct((bsz, A_PREV, A_KVW), F32),
            jax.ShapeDtypeStruct((bsz, A_PREV, A_KVW), F32),
            jax.ShapeDtypeStruct((bsz, B_PREV, B_W), F32),
            jax.ShapeDtypeStruct((bsz, B_PREV, B_W), F32),
            jax.ShapeDtypeStruct(w_upg.shape, BF16),
            jax.ShapeDtypeStruct(w_down.shape, BF16),
        ),
        scratch_shapes=[
            pltpu.VMEM((tile, D_MODEL), BF16),
            pltpu.VMEM((A_QW + B_W, tile), BF16),
            pltpu.VMEM((A_PREV + tile, A_KVW), BF16),
            pltpu.VMEM((A_KVW, A_PREV + tile), BF16),
            pltpu.VMEM((B_PREV + tile, B_W), BF16),
            pltpu.VMEM((B_W, B_PREV + tile), BF16),
            pltpu.VMEM((BF16_ROWS, B_PREV + tile), BF16),
            pltpu.VMEM((tile, A_QW), BF16),
            pltpu.VMEM((tile, B_W), BF16),
            pltpu.VMEM((tile, D_MODEL), BF16),
            pltpu.VMEM((tile, 2 * D_MODEL), F32),
            pltpu.VMEM((A_PREV, A_KVW), F32),
            pltpu.VMEM((A_KVW, A_PREV), F32),
            pltpu.VMEM((B_PREV, B_W), F32),
            pltpu.VMEM((B_W, B_PREV), F32),
        ],
        compiler_params=pltpu.CompilerParams(
            dimension_semantics=("arbitrary", "arbitrary"), vmem_limit_bytes=VMEM_LIMIT),
        name="prompt_mixer",
    )(x, *consts, w_upg, w_down)


def _conv_gelu_gate(u, gt, u1, u2, cw, cb):
    c = cb + cw[0:1, :] * u2
    c = c + cw[1:2, :] * u1
    c = c + cw[2:3, :] * u
    return jax.nn.gelu(c, approximate=True) * gt


def _ffn_kernel(x_ref, gpre_ref, gpost_ref, wupg_ref, cw_ref, cb_ref, wdown_ref, never_ref,
                x2_ref, tail_ref, carry_s, next_s, *, tile):
    i = pl.program_id(1)

    @pl.when(i == 0)
    def _():
        carry_s[...] = jnp.zeros_like(carry_s)

    @pl.when(i > 0)
    def _():
        carry_s[...] = next_s[...]

    x1 = x_ref[...]
    xn = _rms(x1, gpre_ref[...]).astype(BF16)
    never = never_ref[...] != 0
    row = lax.broadcasted_iota(jnp.int32, (tile, FF_CHUNK), 0)
    acc = jnp.zeros((tile, D_MODEL), F32)

    def up_gate(c):
        cols = slice(FF_CHUNK * c, FF_CHUNK * (c + 1))
        gcols = slice(D_FF + FF_CHUNK * c, D_FF + FF_CHUNK * (c + 1))
        return _dot(xn, wupg_ref[:, cols]), _dot(xn, wupg_ref[:, gcols])

    ahead = [up_gate(c) for c in range(min(FF_LOOKAHEAD, N_FF))]
    for c in range(N_FF):
        u, gt = ahead.pop(0)
        cols = slice(FF_CHUNK * c, FF_CHUNK * (c + 1))
        prev = carry_s[c]
        u1 = jnp.where(row == 0, prev[7:8, :], pltpu.roll(u, 1, 0))
        u2 = jnp.where(row == 0, prev[6:7, :], jnp.where(row == 1, prev[7:8, :], pltpu.roll(u, 2, 0)))
        next_s[c] = u[tile - 8:tile, :]
        tail_ref[:, cols] = u[tile - 2:tile, :]
        act = _conv_gelu_gate(u, gt, u1, u2, cw_ref[:, cols], cb_ref[:, cols])
        if c + FF_LOOKAHEAD < N_FF:
            ahead.append(up_gate(c + FF_LOOKAHEAD))
            act = _add_to_corner(act, _ordering_zero(ahead[-1][0][0:BF16_ROWS, 0:LANES], never))
        acc = acc + _dot(act.astype(BF16), wdown_ref[cols, :])
    x2_ref[...] = x1 + _rms(acc, gpost_ref[...])


def _prompt_ffn(x1, gpre, gpost, w):
    bsz, seq, _ = x1.shape
    tile = SEQ_TILE
    assert seq % tile == 0
    row_spec = pl.BlockSpec((None, tile, D_MODEL), lambda b, i: (b, i, 0))
    consts = [gpre, gpost, w["upg"], w["cw"], w["cb"], w["down"], jnp.zeros((BF16_ROWS, LANES), jnp.int32)]
    return pl.pallas_call(
        functools.partial(_ffn_kernel, tile=tile),
        grid=(bsz, seq // tile),
        in_specs=[row_spec] + [_const_spec(c.shape) for c in consts],
        out_specs=(row_spec, pl.BlockSpec((None, 2, D_FF), lambda b, i: (b, 0, 0))),
        out_shape=(jax.ShapeDtypeStruct((bsz, seq, D_MODEL), F32),
                   jax.ShapeDtypeStruct((bsz, 2, D_FF), F32)),
        scratch_shapes=[pltpu.VMEM((N_FF, 8, FF_CHUNK), F32), pltpu.VMEM((N_FF, 8, FF_CHUNK), F32)],
        compiler_params=pltpu.CompilerParams(
            dimension_semantics=("arbitrary", "arbitrary"), vmem_limit_bytes=VMEM_LIMIT),
        name="prompt_ffn",
    )(x1, *consts)


def _sample_proj_kernel(x_ref, g_ref, w_ref, p_ref):
    h = _rms(x_ref[...], g_ref[...]).astype(BF16)
    p_ref[...] = _dot(h, w_ref[...])


def _sample_proj(xs, gpre, w_in_b):
    rows = xs.shape[0]
    nblk = 2
    width = IN_WIDTH // nblk
    assert width % LANES == 0
    return pl.pallas_call(
        _sample_proj_kernel,
        grid=(nblk,),
        in_specs=[pl.BlockSpec((rows, D_MODEL), lambda j: (0, 0)),
                  pl.BlockSpec((1, D_MODEL), lambda j: (0, 0)),
                  pl.BlockSpec((D_MODEL, width), lambda j: (0, j))],
        out_specs=pl.BlockSpec((rows, width), lambda j: (0, j)),
        out_shape=jax.ShapeDtypeStruct((rows, IN_WIDTH), F32),
        compiler_params=pltpu.CompilerParams(dimension_semantics=("arbitrary",)),
        name="sample_proj",
    )(xs, gpre, w_in_b)


def _sample_attn_kernel(sink_ref, p_ref, cak_ref, cav_ref, cbk_ref, cbv_ref, bsa_ref, bsb_ref, never_ref,
                        oa_ref, ob_ref, kan_ref, van_ref, kbn_ref, vbn_ref, ka_s, va_s, kb_s, vb_s, *, s, la, lb):
    p = p_ref[...]
    kan_ref[...] = p[:, _C_KA:_C_KA + A_KVW]
    van_ref[...] = p[:, _C_VA:_C_VA + A_KVW]
    kbn_ref[...] = p[:, _C_KB:_C_KB + B_W]
    vbn_ref[...] = p[:, _C_VB:_C_VB + B_W]
    ka_pad = ka_s.shape[0]
    kb_pad = kb_s.shape[0]

    def fill(dst, cache_ref, new, n_cache, n_pad):
        dst[0:n_cache, :] = cache_ref[...].astype(BF16)
        dst[n_cache:n_cache + s, :] = new.astype(BF16)
        dst[n_cache + s:n_pad, :] = jnp.zeros((n_pad - n_cache - s, dst.shape[1]), BF16)

    fill(ka_s, cak_ref, p[:, _C_KA:_C_KA + A_KVW], la, ka_pad)
    fill(va_s, cav_ref, p[:, _C_VA:_C_VA + A_KVW], la, ka_pad)
    fill(kb_s, cbk_ref, p[:, _C_KB:_C_KB + B_W], lb, kb_pad)
    fill(vb_s, cbv_ref, p[:, _C_VB:_C_VB + B_W], lb, kb_pad)

    lo = lax.broadcasted_iota(jnp.int32, (s, LANES), 1) < HEAD_DIM
    top = lax.broadcasted_iota(jnp.int32, (2 * s, 1), 0) < s
    never = never_ref[...] != 0
    ka = ka_s[...]
    va = va_s[...]

    def scores(step):
        kind, j = step
        if kind == "a":
            q = p[:, _C_QA + LANES * j:_C_QA + LANES * (j + 1)] * QK_SCALE
            qr = pltpu.roll(q, HEAD_DIM, 1)
            if 2 * j < A_GROUP:
                qs = jnp.concatenate([jnp.where(lo, q, 0.0), jnp.where(lo, qr, 0.0)], axis=0)
            else:
                qs = jnp.concatenate([jnp.where(lo, 0.0, qr), jnp.where(lo, 0.0, q)], axis=0)
            bias = jnp.concatenate([bsa_ref[2 * j], bsa_ref[2 * j + 1]], axis=0)
            return _dot_nt(qs.astype(BF16), ka) + bias
        q = p[:, _C_QB + LANES * j:_C_QB + LANES * (j + 1)] * QK_SCALE
        qs = jnp.concatenate([jnp.where(lo, q, 0.0), jnp.where(lo, 0.0, q)], axis=0)
        bias = jnp.concatenate([bsb_ref[2 * j], bsb_ref[2 * j + 1]], axis=0)
        return _dot_nt(qs.astype(BF16), kb_s[:, LANES * j:LANES * (j + 1)]) + bias

    def softmax_pv(st, v, sink, order_after):
        m = jnp.max(st, axis=-1, keepdims=True)
        if sink is not None:
            m = jnp.maximum(m, sink)
        e = jnp.exp(st - m)
        l = jnp.sum(e, axis=-1, keepdims=True)
        if sink is not None:
            l = l + jnp.exp(sink - m)
        if order_after is not None:
            e = _add_to_corner(e, order_after)
        return _dot(e.astype(BF16), v) * (1.0 / l)

    steps = [("a", j) for j in range(A_HEADS // 2)] + [("b", j) for j in range(B_HEADS // 2)]
    st_next = scores(steps[0])
    for k, (kind, j) in enumerate(steps):
        st, order_after = st_next, None
        if k + 1 < len(steps):
            st_next = scores(steps[k + 1])
            order_after = _ordering_zero(st_next[0:BF16_ROWS, 0:LANES], never)
        lanes = slice(LANES * j, LANES * (j + 1))
        if kind == "a":
            sink = jnp.where(top, sink_ref[2 * j], sink_ref[2 * j + 1])
            o = softmax_pv(st, va, sink, order_after)
            oe, oo = o[0:s], o[s:2 * s]
            if 2 * j < A_GROUP:
                oa_ref[:, lanes] = jnp.where(lo, oe, pltpu.roll(oo, HEAD_DIM, 1))
            else:
                oa_ref[:, lanes] = jnp.where(lo, pltpu.roll(oe, HEAD_DIM, 1), oo)
        else:
            o = softmax_pv(st, vb_s[:, lanes], None, order_after)
            ob_ref[:, lanes] = jnp.where(lo, o[0:s], o[s:2 * s])


def _sample_attn(sink, p, cak, cav, cbk, cbv, bsa, bsb, s):
    nb, la, _ = cak.shape
    lb = cbk.shape[1]
    ka_pad, kb_pad = bsa.shape[-1], bsb.shape[-1]
    assert s % 16 == 0 and la % 16 == 0 and lb % 16 == 0
    row = lambda w: pl.BlockSpec((s, w), lambda b: (b, 0))
    cache = lambda n, w: pl.BlockSpec((None, n, w), lambda b: (b, 0, 0))
    full = lambda a: pl.BlockSpec(a.shape, lambda b: (0,) * a.ndim)
    widths = (A_QW, B_W, A_KVW, A_KVW, B_W, B_W)
    return pl.pallas_call(
        functools.partial(_sample_attn_kernel, s=s, la=la, lb=lb),
        grid=(nb,),
        in_specs=[pl.BlockSpec(memory_space=pltpu.SMEM), row(IN_WIDTH),
                  cache(la, A_KVW), cache(la, A_KVW), cache(lb, B_W), cache(lb, B_W), full(bsa), full(bsb),
                  pl.BlockSpec((BF16_ROWS, LANES), lambda b: (0, 0))],
        out_specs=tuple(row(w) for w in widths),
        out_shape=tuple(jax.ShapeDtypeStruct((nb * s, w), F32) for w in widths),
        scratch_shapes=[pltpu.VMEM((ka_pad, A_KVW), BF16), pltpu.VMEM((ka_pad, A_KVW), BF16),
                        pltpu.VMEM((kb_pad, B_W), BF16), pltpu.VMEM((kb_pad, B_W), BF16)],
        compiler_params=pltpu.CompilerParams(dimension_semantics=("arbitrary",)),
        name="sample_attn",
    )(sink, p, cak, cav, cbk, cbv, bsa, bsb, jnp.zeros((BF16_ROWS, LANES), jnp.int32))


def _sample_tail_kernel(x_ref, oa_ref, ob_ref, p_ref, woa_ref, wob_ref, wout_ref,
                        gpost_ref, gpre2_ref, gpost2_ref, never_ref, init_ref,
                        wup_ref, wgate_ref, cw_ref, cb_ref, wdown_ref,
                        x2_ref, u_ref, x1_s, xn_s, acc_s, *, s, n_sub):
    c = pl.program_id(0)

    @pl.when(c == 0)
    def _():
        ya = _dot(oa_ref[...].astype(BF16), woa_ref[...])
        yb = _dot(ob_ref[...].astype(BF16), wob_ref[...])
        ga = p_ref[:, _C_GA:_C_GA + D_MODEL]
        gb = p_ref[:, _C_GB:_C_GB + D_MODEL]
        merged = jax.nn.sigmoid(ga) * ya + jax.nn.sigmoid(gb) * yb
        z = _dot(merged.astype(BF16), wout_ref[...])
        x1 = x_ref[...] + _rms(z, gpost_ref[...])
        x1_s[...] = x1
        xn_s[...] = _rms(x1, gpre2_ref[...]).astype(BF16)
        acc_s[...] = jnp.zeros_like(acc_s)

    xn = xn_s[...]
    never = never_ref[...] != 0
    rows = xn.shape[0]
    pos = lax.broadcasted_iota(jnp.int32, (rows, FF_CHUNK), 0) % s
    sub = lambda k: slice(FF_CHUNK * k, FF_CHUNK * (k + 1))
    up_gate = lambda k: (_dot(xn, wup_ref[:, sub(k)]), _dot(xn, wgate_ref[:, sub(k)]))
    ahead = [up_gate(k) for k in range(n_sub)]
    acc = acc_s[...]
    for k in range(n_sub):
        u, gt = ahead[k]
        u_ref[:, sub(k)] = u
        init = init_ref[:, sub(k)]
        u1 = jnp.where(pos == 0, pltpu.roll(init, rows - 1, 0), pltpu.roll(u, 1, 0))
        u2 = jnp.where(pos < 2, init, pltpu.roll(u, 2, 0))
        act = _conv_gelu_gate(u, gt, u1, u2, cw_ref[:, sub(k)], cb_ref[:, sub(k)])
        if k == 0:
            act = _add_to_corner(act, _ordering_zero(ahead[-1][0][0:BF16_ROWS, 0:LANES], never))
        acc = acc + _dot(act.astype(BF16), wdown_ref[sub(k), :])
    acc_s[...] = acc

    @pl.when(c == pl.num_programs(0) - 1)
    def _():
        x2_ref[...] = x1_s[...] + _rms(acc_s[...], gpost2_ref[...])


def _sample_tail(xs, oa, ob, p, w, gpost, gpre2, gpost2, init, s):
    rows = xs.shape[0]
    n_sub = SAMPLE_FF_SUB
    width = n_sub * FF_CHUNK
    n_steps = D_FF // width
    assert n_steps * width == D_FF
    full = lambda a: pl.BlockSpec(a.shape, lambda c: (0,) * a.ndim)
    cols = lambda r: pl.BlockSpec((r, width), lambda c: (0, c))
    gate_cols = pl.BlockSpec((D_MODEL, width), lambda c: (0, n_steps + c))
    down_rows = pl.BlockSpec((width, D_MODEL), lambda c: (c, 0))
    args = [xs, oa, ob, p, w["oa"], w["ob"], w["out"], gpost, gpre2, gpost2, jnp.zeros((BF16_ROWS, LANES), jnp.int32)]
    return pl.pallas_call(
        functools.partial(_sample_tail_kernel, s=s, n_sub=n_sub),
        grid=(n_steps,),
        in_specs=[full(a) for a in args] + [cols(rows), cols(D_MODEL), gate_cols, cols(3), cols(1), down_rows],
        out_specs=(full(xs), cols(rows)),
        out_shape=(jax.ShapeDtypeStruct((rows, D_MODEL), F32), jax.ShapeDtypeStruct((rows, D_FF), F32)),
        scratch_shapes=[pltpu.VMEM((rows, D_MODEL), F32), pltpu.VMEM((rows, D_MODEL), BF16),
                        pltpu.VMEM((rows, D_MODEL), F32)],
        compiler_params=pltpu.CompilerParams(dimension_semantics=("arbitrary",), vmem_limit_bytes=VMEM_LIMIT),
        name="sample_tail",
    )(*args, init, w["upg"], w["upg"], w["cw"], w["cb"], w["down"])


def _prep_weights(w_in, w_oa, w_ob, w_out, conv_w, conv_b):
    wb = w_in.astype(BF16)
    qv = jnp.concatenate([wb[:, _C_QA:_C_QA + A_QW], wb[:, _C_QB:_C_QB + B_W],
                          wb[:, _C_VA:_C_VA + A_KVW], wb[:, _C_VB:_C_VB + B_W],
                          wb[:, _C_KA:_C_KA + A_KVW]], axis=1)
    return {
        "in": wb, "qvt": qv.T,
        "oa": w_oa.astype(BF16), "ob": w_ob.astype(BF16), "out": w_out.astype(BF16),
        "cw": conv_w, "cb": conv_b.reshape(1, D_FF),
    }


def _round_up(n, m):
    return (n + m - 1) // m * m


def kernel(x_prompt, x_sample, cache_a_k, cache_a_v, cache_b_k, cache_b_v, state_conv,
           w_in, w_oa, w_ob, w_out, sink_a, t5_table, rel_table_b,
           g_pre_mix, g_post_mix, g_pre_ffn, g_post_ffn, w_upg, conv_w, conv_b, w_down):
    depth = w_in.shape[0]
    bsz, seq, _ = x_prompt.shape
    nb, s, _ = x_sample.shape
    la, lb = cache_a_k.shape[2], cache_b_k.shape[2]
    ka_pad = _round_up(la + s, LANES)
    kb_pad = _round_up(lb + s, LANES)

    xp = x_prompt
    xs = x_sample.reshape(nb * s, D_MODEL)
    prompt_states = [[] for _ in range(5)]
    sample_states = [[] for _ in range(5)]
    for l in range(depth):
        w = _prep_weights(w_in[l], w_oa[l], w_ob[l], w_out[l], conv_w[l], conv_b[l])
        row = lambda g: g[l].reshape(1, D_MODEL)
        bta, btb, bsa, bsb = _build_bias(t5_table, rel_table_b[l], s, la, lb, ka_pad, kb_pad)
        sink_row = jnp.repeat(sink_a[l], GROUP).reshape(1, A_HEADS * GROUP)

        x1, ka_t, va_t, kb_t, vb_t, w["upg"], w["down"] = _prompt_mixer(
            xp, row(g_pre_mix), row(g_post_mix), w, sink_row, bta, btb, w_upg[l], w_down[l])
        xp, conv_t = _prompt_ffn(x1, row(g_pre_ffn), row(g_post_ffn), w)
        prompt_states[0].append(ka_t.reshape(bsz, A_PREV, A_KV_HEADS, HEAD_DIM))
        prompt_states[1].append(va_t.reshape(bsz, A_PREV, A_KV_HEADS, HEAD_DIM))
        prompt_states[2].append(kb_t.reshape(bsz, B_PREV, B_HEADS, HEAD_DIM))
        prompt_states[3].append(vb_t.reshape(bsz, B_PREV, B_HEADS, HEAD_DIM))
        prompt_states[4].append(conv_t)

        p = _sample_proj(xs, row(g_pre_mix), w["in"])
        oa, ob, ka_n, va_n, kb_n, vb_n = _sample_attn(
            sink_a[l], p, cache_a_k[l].reshape(nb, la, A_KVW), cache_a_v[l].reshape(nb, la, A_KVW),
            cache_b_k[l].reshape(nb, lb, B_W), cache_b_v[l].reshape(nb, lb, B_W), bsa, bsb, s)
        init = jnp.pad(state_conv[l], ((0, 0), (0, s - 2), (0, 0))).reshape(nb * s, D_FF)
        xs, u = _sample_tail(xs, oa, ob, p, w, row(g_post_mix), row(g_pre_ffn), row(g_post_ffn), init, s)
        sample_states[0].append(ka_n.reshape(nb, s, A_KV_HEADS, HEAD_DIM))
        sample_states[1].append(va_n.reshape(nb, s, A_KV_HEADS, HEAD_DIM))
        sample_states[2].append(kb_n.reshape(nb, s, B_HEADS, HEAD_DIM))
        sample_states[3].append(vb_n.reshape(nb, s, B_HEADS, HEAD_DIM))
        sample_states[4].append(u.reshape(nb, s, D_FF)[:, -2:])

    ps = [jnp.stack(a, axis=0) for a in prompt_states]
    ss = [jnp.stack(a, axis=0) for a in sample_states]
    return (xp, xs.reshape(nb, s, D_MODEL), ps[0], ps[1], ps[2], ps[3], ps[4],
            ss[0], ss[1], ss[2], ss[3], ss[4])
```

```python
import functools
import math

import numpy as np
import jax
import jax.numpy as jnp
from jax import lax
from jax.experimental import pallas as pl
from jax.experimental.pallas import tpu as pltpu

F32 = jnp.float32
BF16 = jnp.bfloat16

D_MODEL = 1024
CHUNK = 64
HEAD_DIM = 64
EPS = 1e-6
NEG_INF = -1e30
QK_SCALE = HEAD_DIM ** -0.5
LOG2E = math.log2(math.e)

A_HEADS = 8
A_KV_HEADS = 2
A_GROUP = A_HEADS // A_KV_HEADS
A_PREV = 128
T5_BUCKETS = 32
T5_MAX_DIST = 128

B_HEADS = 8
B_PREV = 512
B_REL_CLIP = 128

A_QW = A_HEADS * HEAD_DIM
A_KVW = A_KV_HEADS * HEAD_DIM
B_W = B_HEADS * HEAD_DIM
D_FF = 3072

_C_QA = 0
_C_KA = _C_QA + A_QW
_C_VA = _C_KA + A_KVW
_C_QB = _C_VA + A_KVW
_C_KB = _C_QB + B_W
_C_VB = _C_KB + B_W
_C_GA = _C_VB + B_W
_C_GB = _C_GA + D_MODEL
IN_WIDTH = _C_GB + D_MODEL

LANES = 128
BF16_ROWS = 16
MXU_COLS = 256
GROUP = 2 * CHUNK
A_WIN = A_PREV + GROUP
B_WIN = B_PREV + GROUP
SEQ_TILE = 512
FF_CHUNK = 512
N_FF = D_FF // FF_CHUNK
SAMPLE_FF_SUB = 3
FF_LOOKAHEAD = 4
VMEM_LIMIT = 56 * 1024 * 1024
LOOKAHEAD = 1
GATE_FIRST_STEP = 1
GATE_STEP_STRIDE = 3


def _rms(x, g):
    y = x * lax.rsqrt(jnp.mean(x * x, axis=-1, keepdims=True) + EPS)
    return y * g


def _dot(a, b):
    return jnp.dot(a, b, preferred_element_type=F32)


def _dot_nt(a, b):
    return lax.dot_general(a, b, (((1,), (1,)), ((), ())), preferred_element_type=F32)


def _t5_bucket(n):
    half = T5_BUCKETS // 2
    max_exact = half // 2
    ret = jnp.where(n < 0, half, 0)
    a = jnp.abs(n)
    af = jnp.maximum(a, 1).astype(jnp.float32)
    large = max_exact + (jnp.log(af / max_exact) / math.log(T5_MAX_DIST / max_exact)
                         * (half - max_exact)).astype(jnp.int32)
    large = jnp.minimum(large, half - 1)
    return ret + jnp.where(a < max_exact, a, large)


def _window_valid(n_prev, win):
    j = np.arange(win)[:, None]
    ci = np.arange(GROUP)[None, :] // CHUNK
    return ((j >= CHUNK * ci) & (j < CHUNK * ci + n_prev + CHUNK)).astype(np.int32)


def _lookup(idx, table, lo, hi):
    acc = jnp.zeros((table.shape[0], idx.shape[1]), F32)
    for b in range(lo, hi):
        acc = jnp.where(idx == b, table[:, b:b + 1], acc)
    return acc


def _toeplitz(vec, rows, first):
    w = vec.shape[1]
    return pltpu.roll(jnp.broadcast_to(vec, (rows, w)), (w - first) % w, 1, stride=1, stride_axis=0)


def _bias_kernel(t5_ref, rel_ref, ja_ref, jb_ref, jsa_ref, jsb_ref, va_ref, vb_ref, vsa_ref, vsb_ref,
                 bta_ref, btb_ref, bsa_ref, bsb_ref, *, s, jb_range, jsb_range):
    t5 = t5_ref[...]
    rel = rel_ref[...]

    fa = _lookup(ja_ref[...], t5, 0, T5_BUCKETS) * LOG2E
    va = va_ref[...] != 0
    for h in range(A_HEADS):
        t = _toeplitz(fa[h:h + 1, :], A_WIN, A_WIN - 1)[:, 0:GROUP]
        bta_ref[:, GROUP * h:GROUP * (h + 1)] = jnp.where(va, t, NEG_INF)

    fb = _lookup(jb_ref[...], rel, *jb_range) * LOG2E
    for r in range(B_WIN // LANES):
        base = B_WIN - LANES * r
        vb = vb_ref[LANES * r:LANES * (r + 1), :] != 0
        for h in range(B_HEADS):
            seg = fb[h:h + 1, base - LANES:base + LANES]
            t = _toeplitz(seg, LANES, LANES)[:, 0:GROUP]
            btb_ref[h, LANES * r:LANES * (r + 1), :] = jnp.where(vb, t, NEG_INF)

    fsa = _lookup(jsa_ref[...], t5, 0, T5_BUCKETS)
    fsb = _lookup(jsb_ref[...], rel, *jsb_range)
    vsa = vsa_ref[...] != 0
    vsb = vsb_ref[...] != 0
    for h in range(A_HEADS):
        bsa_ref[h] = jnp.where(vsa, _toeplitz(fsa[h:h + 1, :], s, s - 1), NEG_INF)
    for h in range(B_HEADS):
        bsb_ref[h] = jnp.where(vsb, _toeplitz(fsb[h:h + 1, :], s, s - 1), NEG_INF)


def _build_bias(t5_table, rel_table, s, la, lb, ka_pad, kb_pad):
    wa = 3 * LANES
    assert A_WIN - 1 + GROUP <= wa and la + 2 * s - 2 < ka_pad and lb + 2 * s - 2 < kb_pad
    bucket = lambda n: (_t5_bucket(n).astype(jnp.int32) & (T5_BUCKETS - 1)).reshape(1, -1)
    ja = bucket(jnp.arange(wa) - (A_WIN - 1) + A_PREV)
    jb = np.clip(np.arange(B_WIN + GROUP) - B_WIN + B_PREV, -B_REL_CLIP, B_REL_CLIP) + B_REL_CLIP
    jsa = bucket(la + s - 1 - jnp.arange(ka_pad))
    jsb = np.clip(lb + s - 1 - np.arange(kb_pad), -B_REL_CLIP, B_REL_CLIP) + B_REL_CLIP
    key = lambda n: np.broadcast_to(np.arange(n)[None, :], (s, n))
    vsa = (key(ka_pad) < la + s).astype(np.int32)
    vsb = (key(kb_pad) < lb + s).astype(np.int32)
    as_row = lambda v: jnp.asarray(v.astype(np.int32).reshape(1, -1))
    rng = lambda v: (int(v.min()), int(v.max()) + 1)

    vmem = pl.BlockSpec(memory_space=pltpu.VMEM)
    return pl.pallas_call(
        functools.partial(_bias_kernel, s=s, jb_range=rng(jb), jsb_range=rng(jsb)),
        out_shape=(
            jax.ShapeDtypeStruct((A_WIN, A_HEADS * GROUP), F32),
            jax.ShapeDtypeStruct((B_HEADS, B_WIN, GROUP), F32),
            jax.ShapeDtypeStruct((A_HEADS, s, ka_pad), F32),
            jax.ShapeDtypeStruct((B_HEADS, s, kb_pad), F32),
        ),
        in_specs=[vmem] * 10,
        out_specs=(vmem,) * 4,
        name="bias_tables",
    )(t5_table, rel_table, ja, as_row(jb), jsa, as_row(jsb),
      jnp.asarray(_window_valid(A_PREV, A_WIN)), jnp.asarray(_window_valid(B_PREV, B_WIN)),
      jnp.asarray(vsa), jnp.asarray(vsb))


def _ordering_zero(x, never):
    return jnp.where(never, x, 0.0)


def _add_to_corner(x, z):
    head = jnp.concatenate([x[0:BF16_ROWS, 0:LANES] + z, x[0:BF16_ROWS, LANES:]], axis=1)
    return jnp.concatenate([head, x[BF16_ROWS:, :]], axis=0)


def _scores_a(g, ka_s, qt_s):
    tok = slice(GROUP * g, GROUP * (g + 1))
    kwin = ka_s[GROUP * g:GROUP * g + A_WIN, :]
    q = qt_s[0:A_QW, tok]
    zero = jnp.zeros((HEAD_DIM, GROUP), BF16)
    cols = []
    for i in range(A_HEADS):
        qi = q[HEAD_DIM * i:HEAD_DIM * (i + 1), :]
        cols.append(jnp.concatenate([qi, zero] if i < A_GROUP else [zero, qi], axis=0))
    return _dot(kwin, jnp.concatenate(cols, axis=1))


def _scores_b(g, pp, kb_s, qt_s):
    tok = slice(GROUP * g, GROUP * (g + 1))
    kwin = kb_s[GROUP * g:GROUP * g + B_WIN, 2 * LANES * pp:2 * LANES * (pp + 1)]
    qrow = A_QW + 2 * LANES * pp
    zero = jnp.zeros((HEAD_DIM, GROUP), BF16)
    cols = []
    for i in range(4):
        qi = qt_s[qrow + HEAD_DIM * i:qrow + HEAD_DIM * (i + 1), tok]
        cols.append(jnp.concatenate([zero] * i + [qi] + [zero] * (3 - i), axis=0))
    return _dot(kwin, jnp.concatenate(cols, axis=1))


def _softmax_keys(st, bias, sink_row, order_after):
    st = st + bias
    m = jnp.max(st, axis=0, keepdims=True)
    if sink_row is not None:
        m = jnp.maximum(m, sink_row)
    p = jnp.exp2(st - m)
    if order_after is not None:
        p = _add_to_corner(p, order_after)
    return p.astype(BF16), m


def _weighted_values(vwin, present, p, m, sink_row):
    ot = _dot(jnp.concatenate([vwin, present], axis=0), p)
    l = ot[2 * HEAD_DIM:2 * HEAD_DIM + 1, :]
    if sink_row is not None:
        l = l + jnp.exp2(sink_row - m)
    return ot[0:2 * HEAD_DIM, :] * (1.0 / l)


def _values_a(g, vat_s, present_s, p, m, sink_row, oa_s):
    tok = slice(GROUP * g, GROUP * (g + 1))
    c0 = B_PREV - A_PREV + GROUP * g
    ot = _weighted_values(vat_s[:, GROUP * g:GROUP * g + A_WIN], present_s[:, c0:c0 + A_WIN], p, m, sink_row)
    for j in range(A_HEADS // 2):
        parts = []
        for i in (2 * j, 2 * j + 1):
            rows = slice(0, HEAD_DIM) if i < A_GROUP else slice(HEAD_DIM, 2 * HEAD_DIM)
            parts.append(ot[rows, GROUP * i:GROUP * (i + 1)])
        blk = jnp.concatenate(parts, axis=0)
        oa_s[tok, LANES * j:LANES * (j + 1)] = blk.astype(BF16).T


def _values_b(g, p2, vbt_s, present_s, p, m, ob_s):
    tok = slice(GROUP * g, GROUP * (g + 1))
    keys = slice(GROUP * g, GROUP * g + B_WIN)
    lanes = slice(LANES * p2, LANES * (p2 + 1))
    ot = _weighted_values(vbt_s[lanes, keys], present_s[:, keys], p, m, None)
    blk = jnp.concatenate([ot[0:HEAD_DIM, 0:GROUP], ot[HEAD_DIM:2 * HEAD_DIM, GROUP:2 * GROUP]], axis=0)
    ob_s[tok, lanes] = blk.astype(BF16).T


def _attend_tile(n_groups, never, ka_s, vat_s, kb_s, vbt_s, present_s, qt_s, bta_ref, btb_ref, sink_row, oa_s, ob_s,
                 fillers):
    steps = []
    for g in range(n_groups):
        steps.append(("a", g, 0))
        steps.extend(("b", g, pp) for pp in range(B_HEADS // 4))

    def scores(step):
        kind, g, pp = step
        return _scores_a(g, ka_s, qt_s) if kind == "a" else _scores_b(g, pp, kb_s, qt_s)

    ahead = [scores(step) for step in steps[:LOOKAHEAD]]
    for k, (kind, g, pp) in enumerate(steps):
        st, order_after = ahead.pop(0), None
        if k + LOOKAHEAD < len(steps):
            ahead.append(scores(steps[k + LOOKAHEAD]))
            order_after = _ordering_zero(ahead[-1][0:BF16_ROWS, 0:LANES], never)
        if fillers.get(k) is not None:
            z = _ordering_zero(fillers[k](), never)
            order_after = z if order_after is None else order_after + z
        if kind == "a":
            p, m = _softmax_keys(st, bta_ref[...], sink_row, order_after)
            _values_a(g, vat_s, present_s, p, m, sink_row, oa_s)
        else:
            bias = jnp.concatenate([btb_ref[4 * pp + i] for i in range(4)], axis=1)
            p, m = _softmax_keys(st, bias, None, order_after)
            for j in range(2):
                cols = slice(2 * GROUP * j, 2 * GROUP * (j + 1))
                _values_b(g, 2 * pp + j, vbt_s, present_s, p[:, cols], m[:, cols], ob_s)


def _mixer_kernel(x_ref, gpre_ref, gpost_ref, win_ref, wt_ref,
                  woa_ref, wob_ref, wout_ref, sink_ref, bta_ref, btb_ref, never_ref, wupg_ref, wdown_ref,
                  x1_ref, kat_ref, vat_ref, kbt_ref, vbt_ref, wupg_b_ref, wdown_b_ref,
                  h_s, qt_s, ka_s, vat_s, kb_s, vbt_s, present_s, oa_s, ob_s, mg_s, sg_s,
                  ka_f, va_f, kb_f, vb_f, *, tile, n_tiles):
    i = pl.program_id(1)

    wupg_b_ref[...] = wupg_ref[...].astype(BF16)
    wdown_b_ref[...] = wdown_ref[...].astype(BF16)

    @pl.when(i == 0)
    def _():
        ka_s[0:A_PREV, :] = jnp.zeros((A_PREV, A_KVW), BF16)
        vat_s[:, 0:A_PREV] = jnp.zeros((A_KVW, A_PREV), BF16)
        kb_s[0:B_PREV, :] = jnp.zeros((B_PREV, B_W), BF16)
        vbt_s[:, 0:B_PREV] = jnp.zeros((B_W, B_PREV), BF16)
        present_s[:, 0:B_PREV] = jnp.zeros((BF16_ROWS, B_PREV), BF16)

    @pl.when(i > 0)
    def _():
        ka_s[0:A_PREV, :] = ka_s[tile:tile + A_PREV, :]
        vat_s[:, 0:A_PREV] = vat_s[:, tile:tile + A_PREV]
        kb_s[0:B_PREV, :] = kb_s[tile:tile + B_PREV, :]
        vbt_s[:, 0:B_PREV] = vbt_s[:, tile:tile + B_PREV]
        present_s[:, 0:B_PREV] = present_s[:, tile:tile + B_PREV]

    x = x_ref[...]
    h = _rms(x, gpre_ref[...]).astype(BF16)
    h_s[...] = h

    kb = _dot(h, win_ref[:, _C_KB:_C_KB + B_W])
    kb_s[B_PREV:B_PREV + tile, :] = kb.astype(BF16)
    nq = A_QW + B_W
    pt = _dot_nt(wt_ref[...], h)
    qt_s[...] = (pt[0:nq, :] * (QK_SCALE * LOG2E)).astype(BF16)
    vat = pt[nq:nq + A_KVW, :]
    vbt = pt[nq + A_KVW:nq + A_KVW + B_W, :]
    ka = pt[nq + A_KVW + B_W:, :].T
    ka_s[A_PREV:A_PREV + tile, :] = ka.astype(BF16)
    vat_s[:, A_PREV:A_PREV + tile] = vat.astype(BF16)
    vbt_s[:, B_PREV:B_PREV + tile] = vbt.astype(BF16)
    first_row = lax.broadcasted_iota(jnp.int32, (BF16_ROWS, tile), 0) == 0
    present_s[:, B_PREV:B_PREV + tile] = jnp.where(first_row, 1.0, 0.0).astype(BF16)
    ka_f[...] = ka[tile - A_PREV:, :]
    kb_f[...] = kb[tile - B_PREV:, :]
    va_f[...] = vat[:, tile - A_PREV:]
    vb_f[...] = vbt[:, tile - B_PREV:]

    gate_w = 2 * MXU_COLS
    n_steps = (tile // GROUP) * (1 + B_HEADS // 4)

    def gate_piece(j):
        def run():
            g = _dot(h_s[...], win_ref[:, _C_GA + gate_w * j:_C_GA + gate_w * (j + 1)])
            sg_s[:, gate_w * j:gate_w * (j + 1)] = jax.nn.sigmoid(g)
            return g[0:BF16_ROWS, 0:LANES]
        return run

    n_pieces = 2 * D_MODEL // gate_w
    fillers = {GATE_FIRST_STEP + GATE_STEP_STRIDE * j: gate_piece(j) for j in range(n_pieces)}
    assert max(fillers) < n_steps
    _attend_tile(tile // GROUP, never_ref[...] != 0, ka_s, vat_s, kb_s, vbt_s, present_s, qt_s,
                 bta_ref, btb_ref, sink_ref[...] * LOG2E, oa_s, ob_s, fillers)

    oa = oa_s[...]
    ob = ob_s[...]
    cw = 2 * MXU_COLS
    for c in range(D_MODEL // cw):
        cs = slice(cw * c, cw * (c + 1))
        ya = _dot(oa, woa_ref[:, cs])
        yb = _dot(ob, wob_ref[:, cs])
        sgb = sg_s[:, D_MODEL + cw * c:D_MODEL + cw * (c + 1)]
        mg_s[:, cs] = (sg_s[:, cs] * ya + sgb * yb).astype(BF16)
    z = _dot(mg_s[...], wout_ref[...])
    x1_ref[...] = x_ref[...] + _rms(z, gpost_ref[...])

    @pl.when(i == n_tiles - 1)
    def _():
        kat_ref[...] = ka_f[...]
        kbt_ref[...] = kb_f[...]
        vat_ref[...] = va_f[...].T
        vbt_ref[...] = vb_f[...].T


def _const_spec(shape):
    nd = len(shape)
    return pl.BlockSpec(shape, lambda *_: (0,) * nd, pipeline_mode=pl.Buffered(1))


def _prompt_mixer(x, gpre, gpost, w, sink_row, bta, btb, w_upg, w_down):
    bsz, seq, _ = x.shape
    tile = SEQ_TILE
    assert seq % tile == 0 and tile % GROUP == 0 and tile >= B_PREV
    n_tiles = seq // tile
    steps = bsz * n_tiles
    up_rows, down_rows = w_upg.shape[0] // steps, w_down.shape[0] // steps
    assert up_rows * steps == w_upg.shape[0] and down_rows * steps == w_down.shape[0]
    assert up_rows % 16 == 0 and down_rows % 16 == 0
    step_rows = lambda rows, width: pl.BlockSpec((rows, width), lambda b, i: (b * n_tiles + i, 0))
    row_spec = pl.BlockSpec((None, tile, D_MODEL), lambda b, i: (b, i, 0))

    def tail_spec(rows, width):
        return pl.BlockSpec((None, rows, width), lambda b, i: (b, 0, 0))

    consts = [gpre, gpost, w["in"], w["qvt"],
              w["oa"], w["ob"], w["out"], sink_row, bta, btb, jnp.zeros((BF16_ROWS, LANES), jnp.int32)]
    return pl.pallas_call(
        functools.partial(_mixer_kernel, tile=tile, n_tiles=n_tiles),
        grid=(bsz, n_tiles),
        in_specs=[row_spec] + [_const_spec(c.shape) for c in consts]
        + [step_rows(up_rows, w_upg.shape[1]), step_rows(down_rows, w_down.shape[1])],
        out_specs=(row_spec, tail_spec(A_PREV, A_KVW), tail_spec(A_PREV, A_KVW),
                   tail_spec(B_PREV, B_W), tail_spec(B_PREV, B_W),
                   step_rows(up_rows, w_upg.shape[1]), step_rows(down_rows, w_down.shape[1])),
        out_shape=(
            jax.ShapeDtypeStruct((bsz, seq, D_MODEL), F32),
            jax.ShapeDtypeStruct((bsz, A_PREV, A_KVW), F32),
            jax.ShapeDtypeStruct((bsz, A_PREV, A_KVW), F32),
            jax.ShapeDtypeStruct((bsz, B_PREV, B_W), F32),
            jax.ShapeDtypeStruct((bsz, B_PREV, B_W), F32),
            jax.ShapeDtypeStruct(w_upg.shape, BF16),
            jax.ShapeDtypeStruct(w_down.shape, BF16),
        ),
        scratch_shapes=[
            pltpu.VMEM((tile, D_MODEL), BF16),
            pltpu.VMEM((A_QW + B_W, tile), BF16),
            pltpu.VMEM((A_PREV + tile, A_KVW), BF16),
            pltpu.VMEM((A_KVW, A_PREV + tile), BF16),
            pltpu.VMEM((B_PREV + tile, B_W), BF16),
            pltpu.VMEM((B_W, B_PREV + tile), BF16),
            pltpu.VMEM((BF16_ROWS, B_PREV + tile), BF16),
            pltpu.VMEM((tile, A_QW), BF16),
            pltpu.VMEM((tile, B_W), BF16),
            pltpu.VMEM((tile, D_MODEL), BF16),
            pltpu.VMEM((tile, 2 * D_MODEL), F32),
            pltpu.VMEM((A_PREV, A_KVW), F32),
            pltpu.VMEM((A_KVW, A_PREV), F32),
            pltpu.VMEM((B_PREV, B_W), F32),
            pltpu.VMEM((B_W, B_PREV), F32),
        ],
        compiler_params=pltpu.CompilerParams(
            dimension_semantics=("arbitrary", "arbitrary"), vmem_limit_bytes=VMEM_LIMIT),
        name="prompt_mixer",
    )(x, *consts, w_upg, w_down)


def _conv_gelu_gate(u, gt, u1, u2, cw, cb):
    c = cb + cw[0:1, :] * u2
    c = c + cw[1:2, :] * u1
    c = c + cw[2:3, :] * u
    return jax.nn.gelu(c, approximate=True) * gt


def _ffn_kernel(x_ref, gpre_ref, gpost_ref, wupg_ref, cw_ref, cb_ref, wdown_ref, never_ref,
                x2_ref, tail_ref, carry_s, next_s, *, tile):
    i = pl.program_id(1)

    @pl.when(i == 0)
    def _():
        carry_s[...] = jnp.zeros_like(carry_s)

    @pl.when(i > 0)
    def _():
        carry_s[...] = next_s[...]

    x1 = x_ref[...]
    xn = _rms(x1, gpre_ref[...]).astype(BF16)
    never = never_ref[...] != 0
    row = lax.broadcasted_iota(jnp.int32, (tile, FF_CHUNK), 0)
    acc = jnp.zeros((tile, D_MODEL), F32)

    def up_gate(c):
        cols = slice(FF_CHUNK * c, FF_CHUNK * (c + 1))
        gcols = slice(D_FF + FF_CHUNK * c, D_FF + FF_CHUNK * (c + 1))
        return _dot(xn, wupg_ref[:, cols]), _dot(xn, wupg_ref[:, gcols])

    ahead = [up_gate(c) for c in range(min(FF_LOOKAHEAD, N_FF))]
    for c in range(N_FF):
        u, gt = ahead.pop(0)
        cols = slice(FF_CHUNK * c, FF_CHUNK * (c + 1))
        prev = carry_s[c]
        u1 = jnp.where(row == 0, prev[7:8, :], pltpu.roll(u, 1, 0))
        u2 = jnp.where(row == 0, prev[6:7, :], jnp.where(row == 1, prev[7:8, :], pltpu.roll(u, 2, 0)))
        next_s[c] = u[tile - 8:tile, :]
        tail_ref[:, cols] = u[tile - 2:tile, :]
        act = _conv_gelu_gate(u, gt, u1, u2, cw_ref[:, cols], cb_ref[:, cols])
        if c + FF_LOOKAHEAD < N_FF:
            ahead.append(up_gate(c + FF_LOOKAHEAD))
            act = _add_to_corner(act, _ordering_zero(ahead[-1][0][0:BF16_ROWS, 0:LANES], never))
        acc = acc + _dot(act.astype(BF16), wdown_ref[cols, :])
    x2_ref[...] = x1 + _rms(acc, gpost_ref[...])


def _prompt_ffn(x1, gpre, gpost, w):
    bsz, seq, _ = x1.shape
    tile = SEQ_TILE
    assert seq % tile == 0
    row_spec = pl.BlockSpec((None, tile, D_MODEL), lambda b, i: (b, i, 0))
    consts = [gpre, gpost, w["upg"], w["cw"], w["cb"], w["down"], jnp.zeros((BF16_ROWS, LANES), jnp.int32)]
    return pl.pallas_call(
        functools.partial(_ffn_kernel, tile=tile),
        grid=(bsz, seq // tile),
        in_specs=[row_spec] + [_const_spec(c.shape) for c in consts],
        out_specs=(row_spec, pl.BlockSpec((None, 2, D_FF), lambda b, i: (b, 0, 0))),
        out_shape=(jax.ShapeDtypeStruct((bsz, seq, D_MODEL), F32),
                   jax.ShapeDtypeStruct((bsz, 2, D_FF), F32)),
        scratch_shapes=[pltpu.VMEM((N_FF, 8, FF_CHUNK), F32), pltpu.VMEM((N_FF, 8, FF_CHUNK), F32)],
        compiler_params=pltpu.CompilerParams(
            dimension_semantics=("arbitrary", "arbitrary"), vmem_limit_bytes=VMEM_LIMIT),
        name="prompt_ffn",
    )(x1, *consts)


def _sample_proj_kernel(x_ref, g_ref, w_ref, p_ref):
    h = _rms(x_ref[...], g_ref[...]).astype(BF16)
    p_ref[...] = _dot(h, w_ref[...])


def _sample_proj(xs, gpre, w_in_b):
    rows = xs.shape[0]
    nblk = 2
    width = IN_WIDTH // nblk
    assert width % LANES == 0
    return pl.pallas_call(
        _sample_proj_kernel,
        grid=(nblk,),
        in_specs=[pl.BlockSpec((rows, D_MODEL), lambda j: (0, 0)),
                  pl.BlockSpec((1, D_MODEL), lambda j: (0, 0)),
                  pl.BlockSpec((D_MODEL, width), lambda j: (0, j))],
        out_specs=pl.BlockSpec((rows, width), lambda j: (0, j)),
        out_shape=jax.ShapeDtypeStruct((rows, IN_WIDTH), F32),
        compiler_params=pltpu.CompilerParams(dimension_semantics=("arbitrary",)),
        name="sample_proj",
    )(xs, gpre, w_in_b)


def _sample_attn_kernel(sink_ref, p_ref, cak_ref, cav_ref, cbk_ref, cbv_ref, bsa_ref, bsb_ref, never_ref,
                        oa_ref, ob_ref, kan_ref, van_ref, kbn_ref, vbn_ref, ka_s, va_s, kb_s, vb_s, *, s, la, lb):
    p = p_ref[...]
    kan_ref[...] = p[:, _C_KA:_C_KA + A_KVW]
    van_ref[...] = p[:, _C_VA:_C_VA + A_KVW]
    kbn_ref[...] = p[:, _C_KB:_C_KB + B_W]
    vbn_ref[...] = p[:, _C_VB:_C_VB + B_W]
    ka_pad = ka_s.shape[0]
    kb_pad = kb_s.shape[0]

    def fill(dst, cache_ref, new, n_cache, n_pad):
        dst[0:n_cache, :] = cache_ref[...].astype(BF16)
        dst[n_cache:n_cache + s, :] = new.astype(BF16)
        dst[n_cache + s:n_pad, :] = jnp.zeros((n_pad - n_cache - s, dst.shape[1]), BF16)

    fill(ka_s, cak_ref, p[:, _C_KA:_C_KA + A_KVW], la, ka_pad)
    fill(va_s, cav_ref, p[:, _C_VA:_C_VA + A_KVW], la, ka_pad)
    fill(kb_s, cbk_ref, p[:, _C_KB:_C_KB + B_W], lb, kb_pad)
    fill(vb_s, cbv_ref, p[:, _C_VB:_C_VB + B_W], lb, kb_pad)

    lo = lax.broadcasted_iota(jnp.int32, (s, LANES), 1) < HEAD_DIM
    top = lax.broadcasted_iota(jnp.int32, (2 * s, 1), 0) < s
    never = never_ref[...] != 0
    ka = ka_s[...]
    va = va_s[...]

    def scores(step):
        kind, j = step
        if kind == "a":
            q = p[:, _C_QA + LANES * j:_C_QA + LANES * (j + 1)] * QK_SCALE
            qr = pltpu.roll(q, HEAD_DIM, 1)
            if 2 * j < A_GROUP:
                qs = jnp.concatenate([jnp.where(lo, q, 0.0), jnp.where(lo, qr, 0.0)], axis=0)
            else:
                qs = jnp.concatenate([jnp.where(lo, 0.0, qr), jnp.where(lo, 0.0, q)], axis=0)
            bias = jnp.concatenate([bsa_ref[2 * j], bsa_ref[2 * j + 1]], axis=0)
            return _dot_nt(qs.astype(BF16), ka) + bias
        q = p[:, _C_QB + LANES * j:_C_QB + LANES * (j + 1)] * QK_SCALE
        qs = jnp.concatenate([jnp.where(lo, q, 0.0), jnp.where(lo, 0.0, q)], axis=0)
        bias = jnp.concatenate([bsb_ref[2 * j], bsb_ref[2 * j + 1]], axis=0)
        return _dot_nt(qs.astype(BF16), kb_s[:, LANES * j:LANES * (j + 1)]) + bias

    def softmax_pv(st, v, sink, order_after):
        m = jnp.max(st, axis=-1, keepdims=True)
        if sink is not None:
            m = jnp.maximum(m, sink)
        e = jnp.exp(st - m)
        l = jnp.sum(e, axis=-1, keepdims=True)
        if sink is not None:
            l = l + jnp.exp(sink - m)
        if order_after is not None:
            e = _add_to_corner(e, order_after)
        return _dot(e.astype(BF16), v) * (1.0 / l)

    steps = [("a", j) for j in range(A_HEADS // 2)] + [("b", j) for j in range(B_HEADS // 2)]
    st_next = scores(steps[0])
    for k, (kind, j) in enumerate(steps):
        st, order_after = st_next, None
        if k + 1 < len(steps):
            st_next = scores(steps[k + 1])
            order_after = _ordering_zero(st_next[0:BF16_ROWS, 0:LANES], never)
        lanes = slice(LANES * j, LANES * (j + 1))
        if kind == "a":
            sink = jnp.where(top, sink_ref[2 * j], sink_ref[2 * j + 1])
            o = softmax_pv(st, va, sink, order_after)
            oe, oo = o[0:s], o[s:2 * s]
            if 2 * j < A_GROUP:
                oa_ref[:, lanes] = jnp.where(lo, oe, pltpu.roll(oo, HEAD_DIM, 1))
            else:
                oa_ref[:, lanes] = jnp.where(lo, pltpu.roll(oe, HEAD_DIM, 1), oo)
        else:
            o = softmax_pv(st, vb_s[:, lanes], None, order_after)
            ob_ref[:, lanes] = jnp.where(lo, o[0:s], o[s:2 * s])


def _sample_attn(sink, p, cak, cav, cbk, cbv, bsa, bsb, s):
    nb, la, _ = cak.shape
    lb = cbk.shape[1]
    ka_pad, kb_pad = bsa.shape[-1], bsb.shape[-1]
    assert s % 16 == 0 and la % 16 == 0 and lb % 16 == 0
    row = lambda w: pl.BlockSpec((s, w), lambda b: (b, 0))
    cache = lambda n, w: pl.BlockSpec((None, n, w), lambda b: (b, 0, 0))
    full = lambda a: pl.BlockSpec(a.shape, lambda b: (0,) * a.ndim)
    widths = (A_QW, B_W, A_KVW, A_KVW, B_W, B_W)
    return pl.pallas_call(
        functools.partial(_sample_attn_kernel, s=s, la=la, lb=lb),
        grid=(nb,),
        in_specs=[pl.BlockSpec(memory_space=pltpu.SMEM), row(IN_WIDTH),
                  cache(la, A_KVW), cache(la, A_KVW), cache(lb, B_W), cache(lb, B_W), full(bsa), full(bsb),
                  pl.BlockSpec((BF16_ROWS, LANES), lambda b: (0, 0))],
        out_specs=tuple(row(w) for w in widths),
        out_shape=tuple(jax.ShapeDtypeStruct((nb * s, w), F32) for w in widths),
        scratch_shapes=[pltpu.VMEM((ka_pad, A_KVW), BF16), pltpu.VMEM((ka_pad, A_KVW), BF16),
                        pltpu.VMEM((kb_pad, B_W), BF16), pltpu.VMEM((kb_pad, B_W), BF16)],
        compiler_params=pltpu.CompilerParams(dimension_semantics=("arbitrary",)),
        name="sample_attn",
    )(sink, p, cak, cav, cbk, cbv, bsa, bsb, jnp.zeros((BF16_ROWS, LANES), jnp.int32))


def _sample_tail_kernel(x_ref, oa_ref, ob_ref, p_ref, woa_ref, wob_ref, wout_ref,
                        gpost_ref, gpre2_ref, gpost2_ref, never_ref, init_ref,
                        wup_ref, wgate_ref, cw_ref, cb_ref, wdown_ref,
                        x2_ref, u_ref, x1_s, xn_s, acc_s, *, s, n_sub):
    c = pl.program_id(0)

    @pl.when(c == 0)
    def _():
        ya = _dot(oa_ref[...].astype(BF16), woa_ref[...])
        yb = _dot(ob_ref[...].astype(BF16), wob_ref[...])
        ga = p_ref[:, _C_GA:_C_GA + D_MODEL]
        gb = p_ref[:, _C_GB:_C_GB + D_MODEL]
        merged = jax.nn.sigmoid(ga) * ya + jax.nn.sigmoid(gb) * yb
        z = _dot(merged.astype(BF16), wout_ref[...])
        x1 = x_ref[...] + _rms(z, gpost_ref[...])
        x1_s[...] = x1
        xn_s[...] = _rms(x1, gpre2_ref[...]).astype(BF16)
        acc_s[...] = jnp.zeros_like(acc_s)

    xn = xn_s[...]
    never = never_ref[...] != 0
    rows = xn.shape[0]
    pos = lax.broadcasted_iota(jnp.int32, (rows, FF_CHUNK), 0) % s
    sub = lambda k: slice(FF_CHUNK * k, FF_CHUNK * (k + 1))
    up_gate = lambda k: (_dot(xn, wup_ref[:, sub(k)]), _dot(xn, wgate_ref[:, sub(k)]))
    ahead = [up_gate(k) for k in range(n_sub)]
    acc = acc_s[...]
    for k in range(n_sub):
        u, gt = ahead[k]
        u_ref[:, sub(k)] = u
        init = init_ref[:, sub(k)]
        u1 = jnp.where(pos == 0, pltpu.roll(init, rows - 1, 0), pltpu.roll(u, 1, 0))
        u2 = jnp.where(pos < 2, init, pltpu.roll(u, 2, 0))
        act = _conv_gelu_gate(u, gt, u1, u2, cw_ref[:, sub(k)], cb_ref[:, sub(k)])
        if k == 0:
            act = _add_to_corner(act, _ordering_zero(ahead[-1][0][0:BF16_ROWS, 0:LANES], never))
        acc = acc + _dot(act.astype(BF16), wdown_ref[sub(k), :])
    acc_s[...] = acc

    @pl.when(c == pl.num_programs(0) - 1)
    def _():
        x2_ref[...] = x1_s[...] + _rms(acc_s[...], gpost2_ref[...])


def _sample_tail(xs, oa, ob, p, w, gpost, gpre2, gpost2, init, s):
    rows = xs.shape[0]
    n_sub = SAMPLE_FF_SUB
    width = n_sub * FF_CHUNK
    n_steps = D_FF // width
    assert n_steps * width == D_FF
    full = lambda a: pl.BlockSpec(a.shape, lambda c: (0,) * a.ndim)
    cols = lambda r: pl.BlockSpec((r, width), lambda c: (0, c))
    gate_cols = pl.BlockSpec((D_MODEL, width), lambda c: (0, n_steps + c))
    down_rows = pl.BlockSpec((width, D_MODEL), lambda c: (c, 0))
    args = [xs, oa, ob, p, w["oa"], w["ob"], w["out"], gpost, gpre2, gpost2, jnp.zeros((BF16_ROWS, LANES), jnp.int32)]
    return pl.pallas_call(
        functools.partial(_sample_tail_kernel, s=s, n_sub=n_sub),
        grid=(n_steps,),
        in_specs=[full(a) for a in args] + [cols(rows), cols(D_MODEL), gate_cols, cols(3), cols(1), down_rows],
        out_specs=(full(xs), cols(rows)),
        out_shape=(jax.ShapeDtypeStruct((rows, D_MODEL), F32), jax.ShapeDtypeStruct((rows, D_FF), F32)),
        scratch_shapes=[pltpu.VMEM((rows, D_MODEL), F32), pltpu.VMEM((rows, D_MODEL), BF16),
                        pltpu.VMEM((rows, D_MODEL), F32)],
        compiler_params=pltpu.CompilerParams(dimension_semantics=("arbitrary",), vmem_limit_bytes=VMEM_LIMIT),
        name="sample_tail",
    )(*args, init, w["upg"], w["upg"], w["cw"], w["cb"], w["down"])


def _prep_weights(w_in, w_oa, w_ob, w_out, conv_w, conv_b):
    wb = w_in.astype(BF16)
    qv = jnp.concatenate([wb[:, _C_QA:_C_QA + A_QW], wb[:, _C_QB:_C_QB + B_W],
                          wb[:, _C_VA:_C_VA + A_KVW], wb[:, _C_VB:_C_VB + B_W],
                          wb[:, _C_KA:_C_KA + A_KVW]], axis=1)
    return {
        "in": wb, "qvt": qv.T,
        "oa": w_oa.astype(BF16), "ob": w_ob.astype(BF16), "out": w_out.astype(BF16),
        "cw": conv_w, "cb": conv_b.reshape(1, D_FF),
    }


def _round_up(n, m):
    return (n + m - 1) // m * m


def kernel(x_prompt, x_sample, cache_a_k, cache_a_v, cache_b_k, cache_b_v, state_conv,
           w_in, w_oa, w_ob, w_out, sink_a, t5_table, rel_table_b,
           g_pre_mix, g_post_mix, g_pre_ffn, g_post_ffn, w_upg, conv_w, conv_b, w_down):
    depth = w_in.shape[0]
    bsz, seq, _ = x_prompt.shape
    nb, s, _ = x_sample.shape
    la, lb = cache_a_k.shape[2], cache_b_k.shape[2]
    ka_pad = _round_up(la + s, LANES)
    kb_pad = _round_up(lb + s, LANES)

    xp = x_prompt
    xs = x_sample.reshape(nb * s, D_MODEL)
    prompt_states = [[] for _ in range(5)]
    sample_states = [[] for _ in range(5)]
    for l in range(depth):
        w = _prep_weights(w_in[l], w_oa[l], w_ob[l], w_out[l], conv_w[l], conv_b[l])
        row = lambda g: g[l].reshape(1, D_MODEL)
        bta, btb, bsa, bsb = _build_bias(t5_table, rel_table_b[l], s, la, lb, ka_pad, kb_pad)
        sink_row = jnp.repeat(sink_a[l], GROUP).reshape(1, A_HEADS * GROUP)

        x1, ka_t, va_t, kb_t, vb_t, w["upg"], w["down"] = _prompt_mixer(
            xp, row(g_pre_mix), row(g_post_mix), w, sink_row, bta, btb, w_upg[l], w_down[l])
        xp, conv_t = _prompt_ffn(x1, row(g_pre_ffn), row(g_post_ffn), w)
        prompt_states[0].append(ka_t.reshape(bsz, A_PREV, A_KV_HEADS, HEAD_DIM))
        prompt_states[1].append(va_t.reshape(bsz, A_PREV, A_KV_HEADS, HEAD_DIM))
        prompt_states[2].append(kb_t.reshape(bsz, B_PREV, B_HEADS, HEAD_DIM))
        prompt_states[3].append(vb_t.reshape(bsz, B_PREV, B_HEADS, HEAD_DIM))
        prompt_states[4].append(conv_t)

        p = _sample_proj(xs, row(g_pre_mix), w["in"])
        oa, ob, ka_n, va_n, kb_n, vb_n = _sample_attn(
            sink_a[l], p, cache_a_k[l].reshape(nb, la, A_KVW), cache_a_v[l].reshape(nb, la, A_KVW),
            cache_b_k[l].reshape(nb, lb, B_W), cache_b_v[l].reshape(nb, lb, B_W), bsa, bsb, s)
        init = jnp.pad(state_conv[l], ((0, 0), (0, s - 2), (0, 0))).reshape(nb * s, D_FF)
        xs, u = _sample_tail(xs, oa, ob, p, w, row(g_post_mix), row(g_pre_ffn), row(g_post_ffn), init, s)
        sample_states[0].append(ka_n.reshape(nb, s, A_KV_HEADS, HEAD_DIM))
        sample_states[1].append(va_n.reshape(nb, s, A_KV_HEADS, HEAD_DIM))
        sample_states[2].append(kb_n.reshape(nb, s, B_HEADS, HEAD_DIM))
        sample_states[3].append(vb_n.reshape(nb, s, B_HEADS, HEAD_DIM))
        sample_states[4].append(u.reshape(nb, s, D_FF)[:, -2:])

    ps = [jnp.stack(a, axis=0) for a in prompt_states]
    ss = [jnp.stack(a, axis=0) for a in sample_states]
    return (xp, xs.reshape(nb, s, D_MODEL), ps[0], ps[1], ps[2], ps[3], ps[4],
            ss[0], ss[1], ss[2], ss[3], ss[4])
```

```python
import functools
import math

import numpy as np
import jax
import jax.numpy as jnp
from jax import lax
from jax.experimental import pallas as pl
from jax.experimental.pallas import tpu as pltpu

F32 = jnp.float32
BF16 = jnp.bfloat16

D_MODEL = 1024
CHUNK = 64
HEAD_DIM = 64
EPS = 1e-6
NEG_INF = -1e30
QK_SCALE = HEAD_DIM ** -0.5
LOG2E = math.log2(math.e)

A_HEADS = 8
A_KV_HEADS = 2
A_GROUP = A_HEADS // A_KV_HEADS
A_PREV = 128
T5_BUCKETS = 32
T5_MAX_DIST = 128

B_HEADS = 8
B_PREV = 512
B_REL_CLIP = 128

A_QW = A_HEADS * HEAD_DIM
A_KVW = A_KV_HEADS * HEAD_DIM
B_W = B_HEADS * HEAD_DIM
D_FF = 3072

_C_QA = 0
_C_KA = _C_QA + A_QW
_C_VA = _C_KA + A_KVW
_C_QB = _C_VA + A_KVW
_C_KB = _C_QB + B_W
_C_VB = _C_KB + B_W
_C_GA = _C_VB + B_W
_C_GB = _C_GA + D_MODEL
IN_WIDTH = _C_GB + D_MODEL

LANES = 128
BF16_ROWS = 16
MXU_COLS = 256
GROUP = 2 * CHUNK
A_WIN = A_PREV + GROUP
B_WIN = B_PREV + GROUP
SEQ_TILE = 512
FF_CHUNK = 512
N_FF = D_FF // FF_CHUNK
SAMPLE_FF_SUB = 3
FF_LOOKAHEAD = 4
VMEM_LIMIT = 56 * 1024 * 1024
LOOKAHEAD = 1
GATE_FIRST_STEP = 1
GATE_STEP_STRIDE = 3


def _rms(x, g):
    y = x * lax.rsqrt(jnp.mean(x * x, axis=-1, keepdims=True) + EPS)
    return y * g


def _dot(a, b):
    return jnp.dot(a, b, preferred_element_type=F32)


def _dot_nt(a, b):
    return lax.dot_general(a, b, (((1,), (1,)), ((), ())), preferred_element_type=F32)


def _t5_bucket(n):
    half = T5_BUCKETS // 2
    max_exact = half // 2
    ret = jnp.where(n < 0, half, 0)
    a = jnp.abs(n)
    af = jnp.maximum(a, 1).astype(jnp.float32)
    large = max_exact + (jnp.log(af / max_exact) / math.log(T5_MAX_DIST / max_exact)
                         * (half - max_exact)).astype(jnp.int32)
    large = jnp.minimum(large, half - 1)
    return ret + jnp.where(a < max_exact, a, large)


def _window_valid(n_prev, win):
    j = np.arange(win)[:, None]
    ci = np.arange(GROUP)[None, :] // CHUNK
    return ((j >= CHUNK * ci) & (j < CHUNK * ci + n_prev + CHUNK)).astype(np.int32)


def _lookup(idx, table, lo, hi):
    acc = jnp.zeros((table.shape[0], idx.shape[1]), F32)
    for b in range(lo, hi):
        acc = jnp.where(idx == b, table[:, b:b + 1], acc)
    return acc


def _toeplitz(vec, rows, first):
    w = vec.shape[1]
    return pltpu.roll(jnp.broadcast_to(vec, (rows, w)), (w - first) % w, 1, stride=1, stride_axis=0)


def _bias_kernel(t5_ref, rel_ref, ja_ref, jb_ref, jsa_ref, jsb_ref, va_ref, vb_ref, vsa_ref, vsb_ref,
                 bta_ref, btb_ref, bsa_ref, bsb_ref, *, s, jb_range, jsb_range):
    t5 = t5_ref[...]
    rel = rel_ref[...]

    fa = _lookup(ja_ref[...], t5, 0, T5_BUCKETS) * LOG2E
    va = va_ref[...] != 0
    for h in range(A_HEADS):
        t = _toeplitz(fa[h:h + 1, :], A_WIN, A_WIN - 1)[:, 0:GROUP]
        bta_ref[:, GROUP * h:GROUP * (h + 1)] = jnp.where(va, t, NEG_INF)

    fb = _lookup(jb_ref[...], rel, *jb_range) * LOG2E
    for r in range(B_WIN // LANES):
        base = B_WIN - LANES * r
        vb = vb_ref[LANES * r:LANES * (r + 1), :] != 0
        for h in range(B_HEADS):
            seg = fb[h:h + 1, base - LANES:base + LANES]
            t = _toeplitz(seg, LANES, LANES)[:, 0:GROUP]
            btb_ref[h, LANES * r:LANES * (r + 1), :] = jnp.where(vb, t, NEG_INF)

    fsa = _lookup(jsa_ref[...], t5, 0, T5_BUCKETS)
    fsb = _lookup(jsb_ref[...], rel, *jsb_range)
    vsa = vsa_ref[...] != 0
    vsb = vsb_ref[...] != 0
    for h in range(A_HEADS):
        bsa_ref[h] = jnp.where(vsa, _toeplitz(fsa[h:h + 1, :], s, s - 1), NEG_INF)
    for h in range(B_HEADS):
        bsb_ref[h] = jnp.where(vsb, _toeplitz(fsb[h:h + 1, :], s, s - 1), NEG_INF)


def _build_bias(t5_table, rel_table, s, la, lb, ka_pad, kb_pad):
    wa = 3 * LANES
    assert A_WIN - 1 + GROUP <= wa and la + 2 * s - 2 < ka_pad and lb + 2 * s - 2 < kb_pad
    bucket = lambda n: (_t5_bucket(n).astype(jnp.int32) & (T5_BUCKETS - 1)).reshape(1, -1)
    ja = bucket(jnp.arange(wa) - (A_WIN - 1) + A_PREV)
    jb = np.clip(np.arange(B_WIN + GROUP) - B_WIN + B_PREV, -B_REL_CLIP, B_REL_CLIP) + B_REL_CLIP
    jsa = bucket(la + s - 1 - jnp.arange(ka_pad))
    jsb = np.clip(lb + s - 1 - np.arange(kb_pad), -B_REL_CLIP, B_REL_CLIP) + B_REL_CLIP
    key = lambda n: np.broadcast_to(np.arange(n)[None, :], (s, n))
    vsa = (key(ka_pad) < la + s).astype(np.int32)
    vsb = (key(kb_pad) < lb + s).astype(np.int32)
    as_row = lambda v: jnp.asarray(v.astype(np.int32).reshape(1, -1))
    rng = lambda v: (int(v.min()), int(v.max()) + 1)

    vmem = pl.BlockSpec(memory_space=pltpu.VMEM)
    return pl.pallas_call(
        functools.partial(_bias_kernel, s=s, jb_range=rng(jb), jsb_range=rng(jsb)),
        out_shape=(
            jax.ShapeDtypeStruct((A_WIN, A_HEADS * GROUP), F32),
            jax.ShapeDtypeStruct((B_HEADS, B_WIN, GROUP), F32),
            jax.ShapeDtypeStruct((A_HEADS, s, ka_pad), F32),
            jax.ShapeDtypeStruct((B_HEADS, s, kb_pad), F32),
        ),
        in_specs=[vmem] * 10,
        out_specs=(vmem,) * 4,
        name="bias_tables",
    )(t5_table, rel_table, ja, as_row(jb), jsa, as_row(jsb),
      jnp.asarray(_window_valid(A_PREV, A_WIN)), jnp.asarray(_window_valid(B_PREV, B_WIN)),
      jnp.asarray(vsa), jnp.asarray(vsb))


def _ordering_zero(x, never):
    return jnp.where(never, x, 0.0)


def _add_to_corner(x, z):
    head = jnp.concatenate([x[0:BF16_ROWS, 0:LANES] + z, x[0:BF16_ROWS, LANES:]], axis=1)
    return jnp.concatenate([head, x[BF16_ROWS:, :]], axis=0)


def _scores_a(g, ka_s, qt_s):
    tok = slice(GROUP * g, GROUP * (g + 1))
    kwin = ka_s[GROUP * g:GROUP * g + A_WIN, :]
    q = qt_s[0:A_QW, tok]
    zero = jnp.zeros((HEAD_DIM, GROUP), BF16)
    cols = []
    for i in range(A_HEADS):
        qi = q[HEAD_DIM * i:HEAD_DIM * (i + 1), :]
        cols.append(jnp.concatenate([qi, zero] if i < A_GROUP else [zero, qi], axis=0))
    return _dot(kwin, jnp.concatenate(cols, axis=1))


def _scores_b(g, pp, kb_s, qt_s):
    tok = slice(GROUP * g, GROUP * (g + 1))
    kwin = kb_s[GROUP * g:GROUP * g + B_WIN, 2 * LANES * pp:2 * LANES * (pp + 1)]
    qrow = A_QW + 2 * LANES * pp
    zero = jnp.zeros((HEAD_DIM, GROUP), BF16)
    cols = []
    for i in range(4):
        qi = qt_s[qrow + HEAD_DIM * i:qrow + HEAD_DIM * (i + 1), tok]
        cols.append(jnp.concatenate([zero] * i + [qi] + [zero] * (3 - i), axis=0))
    return _dot(kwin, jnp.concatenate(cols, axis=1))


def _softmax_keys(st, bias, sink_row, order_after):
    st = st + bias
    m = jnp.max(st, axis=0, keepdims=True)
    if sink_row is not None:
        m = jnp.maximum(m, sink_row)
    p = jnp.exp2(st - m)
    if order_after is not None:
        p = _add_to_corner(p, order_after)
    return p.astype(BF16), m


def _weighted_values(vwin, present, p, m, sink_row):
    ot = _dot(jnp.concatenate([vwin, present], axis=0), p)
    l = ot[2 * HEAD_DIM:2 * HEAD_DIM + 1, :]
    if sink_row is not None:
        l = l + jnp.exp2(sink_row - m)
    return ot[0:2 * HEAD_DIM, :] * (1.0 / l)


def _values_a(g, vat_s, present_s, p, m, sink_row, oa_s):
    tok = slice(GROUP * g, GROUP * (g + 1))
    c0 = B_PREV - A_PREV + GROUP * g
    ot = _weighted_values(vat_s[:, GROUP * g:GROUP * g + A_WIN], present_s[:, c0:c0 + A_WIN], p, m, sink_row)
    for j in range(A_HEADS // 2):
        parts = []
        for i in (2 * j, 2 * j + 1):
            rows = slice(0, HEAD_DIM) if i < A_GROUP else slice(HEAD_DIM, 2 * HEAD_DIM)
            parts.append(ot[rows, GROUP * i:GROUP * (i + 1)])
        blk = jnp.concatenate(parts, axis=0)
        oa_s[tok, LANES * j:LANES * (j + 1)] = blk.astype(BF16).T


def _values_b(g, p2, vbt_s, present_s, p, m, ob_s):
    tok = slice(GROUP * g, GROUP * (g + 1))
    keys = slice(GROUP * g, GROUP * g + B_WIN)
    lanes = slice(LANES * p2, LANES * (p2 + 1))
    ot = _weighted_values(vbt_s[lanes, keys], present_s[:, keys], p, m, None)
    blk = jnp.concatenate([ot[0:HEAD_DIM, 0:GROUP], ot[HEAD_DIM:2 * HEAD_DIM, GROUP:2 * GROUP]], axis=0)
    ob_s[tok, lanes] = blk.astype(BF16).T


def _attend_tile(n_groups, never, ka_s, vat_s, kb_s, vbt_s, present_s, qt_s, bta_ref, btb_ref, sink_row, oa_s, ob_s,
                 fillers):
    steps = []
    for g in range(n_groups):
        steps.append(("a", g, 0))
        steps.extend(("b", g, pp) for pp in range(B_HEADS // 4))

    def scores(step):
        kind, g, pp = step
        return _scores_a(g, ka_s, qt_s) if kind == "a" else _scores_b(g, pp, kb_s, qt_s)

    ahead = [scores(step) for step in steps[:LOOKAHEAD]]
    for k, (kind, g, pp) in enumerate(steps):
        st, order_after = ahead.pop(0), None
        if k + LOOKAHEAD < len(steps):
            ahead.append(scores(steps[k + LOOKAHEAD]))
            spare = present_s.shape[1] - LANES
            present_s[:, spare:spare + LANES] = ahead[-1][0:BF16_ROWS, 0:LANES].astype(BF16)
        if fillers.get(k) is not None:
            z = _ordering_zero(fillers[k](), never)
            order_after = z if order_after is None else order_after + z
        if kind == "a":
            p, m = _softmax_keys(st, bta_ref[...], sink_row, order_after)
            _values_a(g, vat_s, present_s, p, m, sink_row, oa_s)
        else:
            bias = jnp.concatenate([btb_ref[4 * pp + i] for i in range(4)], axis=1)
            p, m = _softmax_keys(st, bias, None, order_after)
            for j in range(2):
                cols = slice(2 * GROUP * j, 2 * GROUP * (j + 1))
                _values_b(g, 2 * pp + j, vbt_s, present_s, p[:, cols], m[:, cols], ob_s)


def _mixer_kernel(x_ref, gpre_ref, gpost_ref, win_ref, wt_ref,
                  woa_ref, wob_ref, wout_ref, sink_ref, bta_ref, btb_ref, never_ref, wupg_ref, wdown_ref,
                  x1_ref, kat_ref, vat_ref, kbt_ref, vbt_ref, wupg_b_ref, wdown_b_ref,
                  h_s, qt_s, ka_s, vat_s, kb_s, vbt_s, present_s, oa_s, ob_s, mg_s, sg_s,
                  ka_f, va_f, kb_f, vb_f, *, tile, n_tiles):
    i = pl.program_id(1)

    wupg_b_ref[...] = wupg_ref[...].astype(BF16)
    wdown_b_ref[...] = wdown_ref[...].astype(BF16)

    @pl.when(i == 0)
    def _():
        ka_s[0:A_PREV, :] = jnp.zeros((A_PREV, A_KVW), BF16)
        vat_s[:, 0:A_PREV] = jnp.zeros((A_KVW, A_PREV), BF16)
        kb_s[0:B_PREV, :] = jnp.zeros((B_PREV, B_W), BF16)
        vbt_s[:, 0:B_PREV] = jnp.zeros((B_W, B_PREV), BF16)
        present_s[:, 0:B_PREV] = jnp.zeros((BF16_ROWS, B_PREV), BF16)

    @pl.when(i > 0)
    def _():
        ka_s[0:A_PREV, :] = ka_s[tile:tile + A_PREV, :]
        vat_s[:, 0:A_PREV] = vat_s[:, tile:tile + A_PREV]
        kb_s[0:B_PREV, :] = kb_s[tile:tile + B_PREV, :]
        vbt_s[:, 0:B_PREV] = vbt_s[:, tile:tile + B_PREV]
        present_s[:, 0:B_PREV] = present_s[:, tile:tile + B_PREV]

    x = x_ref[...]
    h = _rms(x, gpre_ref[...]).astype(BF16)
    h_s[...] = h

    kb = _dot(h, win_ref[:, _C_KB:_C_KB + B_W])
    kb_s[B_PREV:B_PREV + tile, :] = kb.astype(BF16)
    nq = A_QW + B_W
    pt = _dot_nt(wt_ref[...], h)
    qt_s[...] = (pt[0:nq, :] * (QK_SCALE * LOG2E)).astype(BF16)
    vat = pt[nq:nq + A_KVW, :]
    vbt = pt[nq + A_KVW:nq + A_KVW + B_W, :]
    ka = pt[nq + A_KVW + B_W:, :].T
    ka_s[A_PREV:A_PREV + tile, :] = ka.astype(BF16)
    vat_s[:, A_PREV:A_PREV + tile] = vat.astype(BF16)
    vbt_s[:, B_PREV:B_PREV + tile] = vbt.astype(BF16)
    first_row = lax.broadcasted_iota(jnp.int32, (BF16_ROWS, tile), 0) == 0
    present_s[:, B_PREV:B_PREV + tile] = jnp.where(first_row, 1.0, 0.0).astype(BF16)
    ka_f[...] = ka[tile - A_PREV:, :]
    kb_f[...] = kb[tile - B_PREV:, :]
    va_f[...] = vat[:, tile - A_PREV:]
    vb_f[...] = vbt[:, tile - B_PREV:]

    gate_w = 2 * MXU_COLS
    n_steps = (tile // GROUP) * (1 + B_HEADS // 4)

    def gate_piece(j):
        def run():
            g = _dot(h_s[...], win_ref[:, _C_GA + gate_w * j:_C_GA + gate_w * (j + 1)])
            sg_s[:, gate_w * j:gate_w * (j + 1)] = jax.nn.sigmoid(g)
            return g[0:BF16_ROWS, 0:LANES]
        return run

    n_pieces = 2 * D_MODEL // gate_w
    fillers = {GATE_FIRST_STEP + GATE_STEP_STRIDE * j: gate_piece(j) for j in range(n_pieces)}
    assert max(fillers) < n_steps
    _attend_tile(tile // GROUP, never_ref[...] != 0, ka_s, vat_s, kb_s, vbt_s, present_s, qt_s,
                 bta_ref, btb_ref, sink_ref[...] * LOG2E, oa_s, ob_s, fillers)

    oa = oa_s[...]
    ob = ob_s[...]
    cw = 2 * MXU_COLS
    for c in range(D_MODEL // cw):
        cs = slice(cw * c, cw * (c + 1))
        ya = _dot(oa, woa_ref[:, cs])
        yb = _dot(ob, wob_ref[:, cs])
        sgb = sg_s[:, D_MODEL + cw * c:D_MODEL + cw * (c + 1)]
        mg_s[:, cs] = (sg_s[:, cs] * ya + sgb * yb).astype(BF16)
    z = _dot(mg_s[...], wout_ref[...])
    x1_ref[...] = x_ref[...] + _rms(z, gpost_ref[...])

    @pl.when(i == n_tiles - 1)
    def _():
        kat_ref[...] = ka_f[...]
        kbt_ref[...] = kb_f[...]
        vat_ref[...] = va_f[...].T
        vbt_ref[...] = vb_f[...].T


def _const_spec(shape):
    nd = len(shape)
    return pl.BlockSpec(shape, lambda *_: (0,) * nd, pipeline_mode=pl.Buffered(1))


def _prompt_mixer(x, gpre, gpost, w, sink_row, bta, btb, w_upg, w_down):
    bsz, seq, _ = x.shape
    tile = SEQ_TILE
    assert seq % tile == 0 and tile % GROUP == 0 and tile >= B_PREV
    n_tiles = seq // tile
    steps = bsz * n_tiles
    up_rows, down_rows = w_upg.shape[0] // steps, w_down.shape[0] // steps
    assert up_rows * steps == w_upg.shape[0] and down_rows * steps == w_down.shape[0]
    assert up_rows % 16 == 0 and down_rows % 16 == 0
    step_rows = lambda rows, width: pl.BlockSpec((rows, width), lambda b, i: (b * n_tiles + i, 0))
    row_spec = pl.BlockSpec((None, tile, D_MODEL), lambda b, i: (b, i, 0))

    def tail_spec(rows, width):
        return pl.BlockSpec((None, rows, width), lambda b, i: (b, 0, 0))

    consts = [gpre, gpost, w["in"], w["qvt"],
              w["oa"], w["ob"], w["out"], sink_row, bta, btb, jnp.zeros((BF16_ROWS, LANES), jnp.int32)]
    return pl.pallas_call(
        functools.partial(_mixer_kernel, tile=tile, n_tiles=n_tiles),
        grid=(bsz, n_tiles),
        in_specs=[row_spec] + [_const_spec(c.shape) for c in consts]
        + [step_rows(up_rows, w_upg.shape[1]), step_rows(down_rows, w_down.shape[1])],
        out_specs=(row_spec, tail_spec(A_PREV, A_KVW), tail_spec(A_PREV, A_KVW),
                   tail_spec(B_PREV, B_W), tail_spec(B_PREV, B_W),
                   step_rows(up_rows, w_upg.shape[1]), step_rows(down_rows, w_down.shape[1])),
        out_shape=(
            jax.ShapeDtypeStruct((bsz, seq, D_MODEL), F32),
            jax.ShapeDtypeStruct((bsz, A_PREV, A_KVW), F32),
            jax.ShapeDtypeStruct((bsz, A_PREV, A_KVW), F32),
            jax.ShapeDtypeStruct((bsz, B_PREV, B_W), F32),
            jax.ShapeDtypeStruct((bsz, B_PREV, B_W), F32),
            jax.ShapeDtypeStruct(w_upg.shape, BF16),
            jax.ShapeDtypeStruct(w_down.shape, BF16),
        ),
        scratch_shapes=[
            pltpu.VMEM((tile, D_MODEL), BF16),
            pltpu.VMEM((A_QW + B_W, tile), BF16),
            pltpu.VMEM((A_PREV + tile, A_KVW), BF16),
            pltpu.VMEM((A_KVW, A_PREV + tile), BF16),
            pltpu.VMEM((B_PREV + tile, B_W), BF16),
            pltpu.VMEM((B_W, B_PREV + tile), BF16),
            pltpu.VMEM((BF16_ROWS, B_PREV + tile + LANES), BF16),
            pltpu.VMEM((tile, A_QW), BF16),
            pltpu.VMEM((tile, B_W), BF16),
            pltpu.VMEM((tile, D_MODEL), BF16),
            pltpu.VMEM((tile, 2 * D_MODEL), F32),
            pltpu.VMEM((A_PREV, A_KVW), F32),
            pltpu.VMEM((A_KVW, A_PREV), F32),
            pltpu.VMEM((B_PREV, B_W), F32),
            pltpu.VMEM((B_W, B_PREV), F32),
        ],
        compiler_params=pltpu.CompilerParams(
            dimension_semantics=("arbitrary", "arbitrary"), vmem_limit_bytes=VMEM_LIMIT),
        name="prompt_mixer",
    )(x, *consts, w_upg, w_down)


def _conv_gelu_gate(u, gt, u1, u2, cw, cb):
    c = cb + cw[0:1, :] * u2
    c = c + cw[1:2, :] * u1
    c = c + cw[2:3, :] * u
    return jax.nn.gelu(c, approximate=True) * gt


def _ffn_kernel(x_ref, gpre_ref, gpost_ref, wupg_ref, cw_ref, cb_ref, wdown_ref, never_ref,
                x2_ref, tail_ref, carry_s, next_s, *, tile):
    i = pl.program_id(1)

    @pl.when(i == 0)
    def _():
        carry_s[...] = jnp.zeros_like(carry_s)

    @pl.when(i > 0)
    def _():
        carry_s[...] = next_s[...]

    x1 = x_ref[...]
    xn = _rms(x1, gpre_ref[...]).astype(BF16)
    never = never_ref[...] != 0
    row = lax.broadcasted_iota(jnp.int32, (tile, FF_CHUNK), 0)
    acc = jnp.zeros((tile, D_MODEL), F32)

    def up_gate(c):
        cols = slice(FF_CHUNK * c, FF_CHUNK * (c + 1))
        gcols = slice(D_FF + FF_CHUNK * c, D_FF + FF_CHUNK * (c + 1))
        return _dot(xn, wupg_ref[:, cols]), _dot(xn, wupg_ref[:, gcols])

    ahead = [up_gate(c) for c in range(min(FF_LOOKAHEAD, N_FF))]
    for c in range(N_FF):
        u, gt = ahead.pop(0)
        cols = slice(FF_CHUNK * c, FF_CHUNK * (c + 1))
        prev = carry_s[c]
        u1 = jnp.where(row == 0, prev[7:8, :], pltpu.roll(u, 1, 0))
        u2 = jnp.where(row == 0, prev[6:7, :], jnp.where(row == 1, prev[7:8, :], pltpu.roll(u, 2, 0)))
        next_s[c] = u[tile - 8:tile, :]
        tail_ref[:, cols] = u[tile - 2:tile, :]
        act = _conv_gelu_gate(u, gt, u1, u2, cw_ref[:, cols], cb_ref[:, cols])
        if c + FF_LOOKAHEAD < N_FF:
            ahead.append(up_gate(c + FF_LOOKAHEAD))
            act = _add_to_corner(act, _ordering_zero(ahead[-1][0][0:BF16_ROWS, 0:LANES], never))
        acc = acc + _dot(act.astype(BF16), wdown_ref[cols, :])
    x2_ref[...] = x1 + _rms(acc, gpost_ref[...])


def _prompt_ffn(x1, gpre, gpost, w):
    bsz, seq, _ = x1.shape
    tile = SEQ_TILE
    assert seq % tile == 0
    row_spec = pl.BlockSpec((None, tile, D_MODEL), lambda b, i: (b, i, 0))
    consts = [gpre, gpost, w["upg"], w["cw"], w["cb"], w["down"], jnp.zeros((BF16_ROWS, LANES), jnp.int32)]
    return pl.pallas_call(
        functools.partial(_ffn_kernel, tile=tile),
        grid=(bsz, seq // tile),
        in_specs=[row_spec] + [_const_spec(c.shape) for c in consts],
        out_specs=(row_spec, pl.BlockSpec((None, 2, D_FF), lambda b, i: (b, 0, 0))),
        out_shape=(jax.ShapeDtypeStruct((bsz, seq, D_MODEL), F32),
                   jax.ShapeDtypeStruct((bsz, 2, D_FF), F32)),
        scratch_shapes=[pltpu.VMEM((N_FF, 8, FF_CHUNK), F32), pltpu.VMEM((N_FF, 8, FF_CHUNK), F32)],
        compiler_params=pltpu.CompilerParams(
            dimension_semantics=("arbitrary", "arbitrary"), vmem_limit_bytes=VMEM_LIMIT),
        name="prompt_ffn",
    )(x1, *consts)


def _sample_proj_kernel(x_ref, g_ref, w_ref, p_ref):
    h = _rms(x_ref[...], g_ref[...]).astype(BF16)
    p_ref[...] = _dot(h, w_ref[...])


def _sample_proj(xs, gpre, w_in_b):
    rows = xs.shape[0]
    nblk = 2
    width = IN_WIDTH // nblk
    assert width % LANES == 0
    return pl.pallas_call(
        _sample_proj_kernel,
        grid=(nblk,),
        in_specs=[pl.BlockSpec((rows, D_MODEL), lambda j: (0, 0)),
                  pl.BlockSpec((1, D_MODEL), lambda j: (0, 0)),
                  pl.BlockSpec((D_MODEL, width), lambda j: (0, j))],
        out_specs=pl.BlockSpec((rows, width), lambda j: (0, j)),
        out_shape=jax.ShapeDtypeStruct((rows, IN_WIDTH), F32),
        compiler_params=pltpu.CompilerParams(dimension_semantics=("arbitrary",)),
        name="sample_proj",
    )(xs, gpre, w_in_b)


def _sample_attn_kernel(sink_ref, p_ref, cak_ref, cav_ref, cbk_ref, cbv_ref, bsa_ref, bsb_ref, never_ref,
                        oa_ref, ob_ref, kan_ref, van_ref, kbn_ref, vbn_ref, ka_s, va_s, kb_s, vb_s, *, s, la, lb):
    p = p_ref[...]
    kan_ref[...] = p[:, _C_KA:_C_KA + A_KVW]
    van_ref[...] = p[:, _C_VA:_C_VA + A_KVW]
    kbn_ref[...] = p[:, _C_KB:_C_KB + B_W]
    vbn_ref[...] = p[:, _C_VB:_C_VB + B_W]
    ka_pad = ka_s.shape[0]
    kb_pad = kb_s.shape[0]

    def fill(dst, cache_ref, new, n_cache, n_pad):
        dst[0:n_cache, :] = cache_ref[...].astype(BF16)
        dst[n_cache:n_cache + s, :] = new.astype(BF16)
        dst[n_cache + s:n_pad, :] = jnp.zeros((n_pad - n_cache - s, dst.shape[1]), BF16)

    fill(ka_s, cak_ref, p[:, _C_KA:_C_KA + A_KVW], la, ka_pad)
    fill(va_s, cav_ref, p[:, _C_VA:_C_VA + A_KVW], la, ka_pad)
    fill(kb_s, cbk_ref, p[:, _C_KB:_C_KB + B_W], lb, kb_pad)
    fill(vb_s, cbv_ref, p[:, _C_VB:_C_VB + B_W], lb, kb_pad)

    lo = lax.broadcasted_iota(jnp.int32, (s, LANES), 1) < HEAD_DIM
    top = lax.broadcasted_iota(jnp.int32, (2 * s, 1), 0) < s
    never = never_ref[...] != 0
    ka = ka_s[...]
    va = va_s[...]

    def scores(step):
        kind, j = step
        if kind == "a":
            q = p[:, _C_QA + LANES * j:_C_QA + LANES * (j + 1)] * QK_SCALE
            qr = pltpu.roll(q, HEAD_DIM, 1)
            if 2 * j < A_GROUP:
                qs = jnp.concatenate([jnp.where(lo, q, 0.0), jnp.where(lo, qr, 0.0)], axis=0)
            else:
                qs = jnp.concatenate([jnp.where(lo, 0.0, qr), jnp.where(lo, 0.0, q)], axis=0)
            bias = jnp.concatenate([bsa_ref[2 * j], bsa_ref[2 * j + 1]], axis=0)
            return _dot_nt(qs.astype(BF16), ka) + bias
        q = p[:, _C_QB + LANES * j:_C_QB + LANES * (j + 1)] * QK_SCALE
        qs = jnp.concatenate([jnp.where(lo, q, 0.0), jnp.where(lo, 0.0, q)], axis=0)
        bias = jnp.concatenate([bsb_ref[2 * j], bsb_ref[2 * j + 1]], axis=0)
        return _dot_nt(qs.astype(BF16), kb_s[:, LANES * j:LANES * (j + 1)]) + bias

    def softmax_pv(st, v, sink, order_after):
        m = jnp.max(st, axis=-1, keepdims=True)
        if sink is not None:
            m = jnp.maximum(m, sink)
        e = jnp.exp(st - m)
        l = jnp.sum(e, axis=-1, keepdims=True)
        if sink is not None:
            l = l + jnp.exp(sink - m)
        if order_after is not None:
            e = _add_to_corner(e, order_after)
        return _dot(e.astype(BF16), v) * (1.0 / l)

    steps = [("a", j) for j in range(A_HEADS // 2)] + [("b", j) for j in range(B_HEADS // 2)]
    st_next = scores(steps[0])
    for k, (kind, j) in enumerate(steps):
        st, order_after = st_next, None
        if k + 1 < len(steps):
            st_next = scores(steps[k + 1])
            order_after = _ordering_zero(st_next[0:BF16_ROWS, 0:LANES], never)
        lanes = slice(LANES * j, LANES * (j + 1))
        if kind == "a":
            sink = jnp.where(top, sink_ref[2 * j], sink_ref[2 * j + 1])
            o = softmax_pv(st, va, sink, order_after)
            oe, oo = o[0:s], o[s:2 * s]
            if 2 * j < A_GROUP:
                oa_ref[:, lanes] = jnp.where(lo, oe, pltpu.roll(oo, HEAD_DIM, 1))
            else:
                oa_ref[:, lanes] = jnp.where(lo, pltpu.roll(oe, HEAD_DIM, 1), oo)
        else:
            o = softmax_pv(st, vb_s[:, lanes], None, order_after)
            ob_ref[:, lanes] = jnp.where(lo, o[0:s], o[s:2 * s])


def _sample_attn(sink, p, cak, cav, cbk, cbv, bsa, bsb, s):
    nb, la, _ = cak.shape
    lb = cbk.shape[1]
    ka_pad, kb_pad = bsa.shape[-1], bsb.shape[-1]
    assert s % 16 == 0 and la % 16 == 0 and lb % 16 == 0
    row = lambda w: pl.BlockSpec((s, w), lambda b: (b, 0))
    cache = lambda n, w: pl.BlockSpec((None, n, w), lambda b: (b, 0, 0))
    full = lambda a: pl.BlockSpec(a.shape, lambda b: (0,) * a.ndim)
    widths = (A_QW, B_W, A_KVW, A_KVW, B_W, B_W)
    return pl.pallas_call(
        functools.partial(_sample_attn_kernel, s=s, la=la, lb=lb),
        grid=(nb,),
        in_specs=[pl.BlockSpec(memory_space=pltpu.SMEM), row(IN_WIDTH),
                  cache(la, A_KVW), cache(la, A_KVW), cache(lb, B_W), cache(lb, B_W), full(bsa), full(bsb),
                  pl.BlockSpec((BF16_ROWS, LANES), lambda b: (0, 0))],
        out_specs=tuple(row(w) for w in widths),
        out_shape=tuple(jax.ShapeDtypeStruct((nb * s, w), F32) for w in widths),
        scratch_shapes=[pltpu.VMEM((ka_pad, A_KVW), BF16), pltpu.VMEM((ka_pad, A_KVW), BF16),
                        pltpu.VMEM((kb_pad, B_W), BF16), pltpu.VMEM((kb_pad, B_W), BF16)],
        compiler_params=pltpu.CompilerParams(dimension_semantics=("arbitrary",)),
        name="sample_attn",
    )(sink, p, cak, cav, cbk, cbv, bsa, bsb, jnp.zeros((BF16_ROWS, LANES), jnp.int32))


def _sample_tail_kernel(x_ref, oa_ref, ob_ref, p_ref, woa_ref, wob_ref, wout_ref,
                        gpost_ref, gpre2_ref, gpost2_ref, never_ref, init_ref,
                        wup_ref, wgate_ref, cw_ref, cb_ref, wdown_ref,
                        x2_ref, u_ref, x1_s, xn_s, acc_s, *, s, n_sub):
    c = pl.program_id(0)

    @pl.when(c == 0)
    def _():
        ya = _dot(oa_ref[...].astype(BF16), woa_ref[...])
        yb = _dot(ob_ref[...].astype(BF16), wob_ref[...])
        ga = p_ref[:, _C_GA:_C_GA + D_MODEL]
        gb = p_ref[:, _C_GB:_C_GB + D_MODEL]
        merged = jax.nn.sigmoid(ga) * ya + jax.nn.sigmoid(gb) * yb
        z = _dot(merged.astype(BF16), wout_ref[...])
        x1 = x_ref[...] + _rms(z, gpost_ref[...])
        x1_s[...] = x1
        xn_s[...] = _rms(x1, gpre2_ref[...]).astype(BF16)
        acc_s[...] = jnp.zeros_like(acc_s)

    xn = xn_s[...]
    never = never_ref[...] != 0
    rows = xn.shape[0]
    pos = lax.broadcasted_iota(jnp.int32, (rows, FF_CHUNK), 0) % s
    sub = lambda k: slice(FF_CHUNK * k, FF_CHUNK * (k + 1))
    up_gate = lambda k: (_dot(xn, wup_ref[:, sub(k)]), _dot(xn, wgate_ref[:, sub(k)]))
    ahead = [up_gate(k) for k in range(n_sub)]
    acc = acc_s[...]
    for k in range(n_sub):
        u, gt = ahead[k]
        u_ref[:, sub(k)] = u
        init = init_ref[:, sub(k)]
        u1 = jnp.where(pos == 0, pltpu.roll(init, rows - 1, 0), pltpu.roll(u, 1, 0))
        u2 = jnp.where(pos < 2, init, pltpu.roll(u, 2, 0))
        act = _conv_gelu_gate(u, gt, u1, u2, cw_ref[:, sub(k)], cb_ref[:, sub(k)])
        if k == 0:
            act = _add_to_corner(act, _ordering_zero(ahead[-1][0][0:BF16_ROWS, 0:LANES], never))
        acc = acc + _dot(act.astype(BF16), wdown_ref[sub(k), :])
    acc_s[...] = acc

    @pl.when(c == pl.num_programs(0) - 1)
    def _():
        x2_ref[...] = x1_s[...] + _rms(acc_s[...], gpost2_ref[...])


def _sample_tail(xs, oa, ob, p, w, gpost, gpre2, gpost2, init, s):
    rows = xs.shape[0]
    n_sub = SAMPLE_FF_SUB
    width = n_sub * FF_CHUNK
    n_steps = D_FF // width
    assert n_steps * width == D_FF
    full = lambda a: pl.BlockSpec(a.shape, lambda c: (0,) * a.ndim)
    cols = lambda r: pl.BlockSpec((r, width), lambda c: (0, c))
    gate_cols = pl.BlockSpec((D_MODEL, width), lambda c: (0, n_steps + c))
    down_rows = pl.BlockSpec((width, D_MODEL), lambda c: (c, 0))
    args = [xs, oa, ob, p, w["oa"], w["ob"], w["out"], gpost, gpre2, gpost2, jnp.zeros((BF16_ROWS, LANES), jnp.int32)]
    return pl.pallas_call(
        functools.partial(_sample_tail_kernel, s=s, n_sub=n_sub),
        grid=(n_steps,),
        in_specs=[full(a) for a in args] + [cols(rows), cols(D_MODEL), gate_cols, cols(3), cols(1), down_rows],
        out_specs=(full(xs), cols(rows)),
        out_shape=(jax.ShapeDtypeStruct((rows, D_MODEL), F32), jax.ShapeDtypeStruct((rows, D_FF), F32)),
        scratch_shapes=[pltpu.VMEM((rows, D_MODEL), F32), pltpu.VMEM((rows, D_MODEL), BF16),
                        pltpu.VMEM((rows, D_MODEL), F32)],
        compiler_params=pltpu.CompilerParams(dimension_semantics=("arbitrary",), vmem_limit_bytes=VMEM_LIMIT),
        name="sample_tail",
    )(*args, init, w["upg"], w["upg"], w["cw"], w["cb"], w["down"])


def _prep_weights(w_in, w_oa, w_ob, w_out, conv_w, conv_b):
    wb = w_in.astype(BF16)
    qv = jnp.concatenate([wb[:, _C_QA:_C_QA + A_QW], wb[:, _C_QB:_C_QB + B_W],
                          wb[:, _C_VA:_C_VA + A_KVW], wb[:, _C_VB:_C_VB + B_W],
                          wb[:, _C_KA:_C_KA + A_KVW]], axis=1)
    return {
        "in": wb, "qvt": qv.T,
        "oa": w_oa.astype(BF16), "ob": w_ob.astype(BF16), "out": w_out.astype(BF16),
        "cw": conv_w, "cb": conv_b.reshape(1, D_FF),
    }


def _round_up(n, m):
    return (n + m - 1) // m * m


def kernel(x_prompt, x_sample, cache_a_k, cache_a_v, cache_b_k, cache_b_v, state_conv,
           w_in, w_oa, w_ob, w_out, sink_a, t5_table, rel_table_b,
           g_pre_mix, g_post_mix, g_pre_ffn, g_post_ffn, w_upg, conv_w, conv_b, w_down):
    depth = w_in.shape[0]
    bsz, seq, _ = x_prompt.shape
    nb, s, _ = x_sample.shape
    la, lb = cache_a_k.shape[2], cache_b_k.shape[2]
    ka_pad = _round_up(la + s, LANES)
    kb_pad = _round_up(lb + s, LANES)

    xp = x_prompt
    xs = x_sample.reshape(nb * s, D_MODEL)
    prompt_states = [[] for _ in range(5)]
    sample_states = [[] for _ in range(5)]
    for l in range(depth):
        w = _prep_weights(w_in[l], w_oa[l], w_ob[l], w_out[l], conv_w[l], conv_b[l])
        row = lambda g: g[l].reshape(1, D_MODEL)
        bta, btb, bsa, bsb = _build_bias(t5_table, rel_table_b[l], s, la, lb, ka_pad, kb_pad)
        sink_row = jnp.repeat(sink_a[l], GROUP).reshape(1, A_HEADS * GROUP)

        x1, ka_t, va_t, kb_t, vb_t, w["upg"], w["down"] = _prompt_mixer(
            xp, row(g_pre_mix), row(g_post_mix), w, sink_row, bta, btb, w_upg[l], w_down[l])
        xp, conv_t = _prompt_ffn(x1, row(g_pre_ffn), row(g_post_ffn), w)
        prompt_states[0].append(ka_t.reshape(bsz, A_PREV, A_KV_HEADS, HEAD_DIM))
        prompt_states[1].append(va_t.reshape(bsz, A_PREV, A_KV_HEADS, HEAD_DIM))
        prompt_states[2].append(kb_t.reshape(bsz, B_PREV, B_HEADS, HEAD_DIM))
        prompt_states[3].append(vb_t.reshape(bsz, B_PREV, B_HEADS, HEAD_DIM))
        prompt_states[4].append(conv_t)

        p = _sample_proj(xs, row(g_pre_mix), w["in"])
        oa, ob, ka_n, va_n, kb_n, vb_n = _sample_attn(
            sink_a[l], p, cache_a_k[l].reshape(nb, la, A_KVW), cache_a_v[l].reshape(nb, la, A_KVW),
            cache_b_k[l].reshape(nb, lb, B_W), cache_b_v[l].reshape(nb, lb, B_W), bsa, bsb, s)
        init = jnp.pad(state_conv[l], ((0, 0), (0, s - 2), (0, 0))).reshape(nb * s, D_FF)
        xs, u = _sample_tail(xs, oa, ob, p, w, row(g_post_mix), row(g_pre_ffn), row(g_post_ffn), init, s)
        sample_states[0].append(ka_n.reshape(nb, s, A_KV_HEADS, HEAD_DIM))
        sample_states[1].append(va_n.reshape(nb, s, A_KV_HEADS, HEAD_DIM))
        sample_states[2].append(kb_n.reshape(nb, s, B_HEADS, HEAD_DIM))
        sample_states[3].append(vb_n.reshape(nb, s, B_HEADS, HEAD_DIM))
        sample_states[4].append(u.reshape(nb, s, D_FF)[:, -2:])

    ps = [jnp.stack(a, axis=0) for a in prompt_states]
    ss = [jnp.stack(a, axis=0) for a in sample_states]
    return (xp, xs.reshape(nb, s, D_MODEL), ps[0], ps[1], ps[2], ps[3], ps[4],
            ss[0], ss[1], ss[2], ss[3], ss[4])
```
